```python
import math
import jax, jax.numpy as jnp
from jax import lax
import numpy as np

D_MODEL = 1024
BATCH = 16
SEQ = 256
DEPTH = 4
DEC_BATCH = 8
DEC_SEQ = 4096
PAST_LEN = 512

GRID_W = 64
N_MIXERS = 2
N_ATTN_LAYERS = (DEPTH + 1) // 2
N_HYENA_LAYERS = DEPTH // 2
N_HEADS = 8
HEAD_DIM = D_MODEL // N_HEADS // 2
V_DIM = 2 * HEAD_DIM
ROPE_PAIRS = HEAD_DIM // 4
ROPE_BASE = 10000.0
Q_BLOCK = 128
FILT_BANDS = 16
FILT_EMB = 1 + 2 * FILT_BANDS
FILT_ORDER = 64
FILT_TARGET = 1e-2
FILT_FAST_PCT = 0.3
FILT_SLOW_PCT = 1.5
FILT_EPS = 1e-6
D_FF = -(-8 * D_MODEL // (3 * 256)) * 256
EPS = 1e-6
SUBLN_EPS = 1e-5

kernel_name = "diffattn_hyena_hybrid_dit_step"


def rmsnorm(x, g, eps=EPS):
    xf = x.astype(jnp.float32)
    y = xf * lax.rsqrt(jnp.mean(xf * xf, axis=-1, keepdims=True) + eps)
    return (y * g.astype(jnp.float32)).astype(x.dtype)


def adaln(cvec, w, b):
    m = jax.nn.silu(cvec) @ w + b
    return jnp.split(m, 6, axis=-1)


def modulate(h, shift, scale):
    return h * (1.0 + scale) + shift


def swiglu(h, w_gu, w_down):
    g, u = jnp.split(h @ w_gu, 2, axis=-1)
    return (jax.nn.silu(g) * u) @ w_down


def axial_rope(L):
    rows = L // GRID_W
    r, col = jnp.meshgrid(jnp.arange(rows, dtype=jnp.float32), jnp.arange(GRID_W, dtype=jnp.float32), indexing="ij")
    inv = ROPE_BASE ** (-jnp.arange(ROPE_PAIRS, dtype=jnp.float32) / ROPE_PAIRS)
    ang = jnp.stack([r.reshape(-1)[:, None] * inv, col.reshape(-1)[:, None] * inv], axis=1)
    return jnp.cos(ang), jnp.sin(ang)


def apply_rope(x, cos, sin):
    xs = x.reshape(x.shape[:-1] + (2, 2, ROPE_PAIRS))
    x1, x2 = xs[..., 0, :], xs[..., 1, :]
    c = cos[:, None, None].astype(x.dtype)
    s = sin[:, None, None].astype(x.dtype)
    out = jnp.stack([x1 * c - x2 * s, x2 * c + x1 * s], axis=-2)
    return out.reshape(x.shape)


def diff_qkv(h, w_qkv):
    B, L, _ = h.shape
    q, k, v = jnp.split(h @ w_qkv, 3, axis=-1)
    return (q.reshape(B, L, N_HEADS, 2, HEAD_DIM),
            k.reshape(B, L, N_HEADS, 2, HEAD_DIM),
            v.reshape(B, L, N_HEADS, V_DIM))


def diff_lambda(lam_params, layer_idx):
    lp = lam_params.astype(jnp.float32)
    lam_init = 0.8 - 0.6 * math.exp(-0.3 * layer_idx)
    lam = jnp.exp(jnp.sum(lp[0] * lp[1])) - jnp.exp(jnp.sum(lp[2] * lp[3])) + lam_init
    return lam, lam_init


def diff_attend(q, k, v, lam):
    B, Lq = q.shape[:2]
    nblk = Lq // Q_BLOCK
    scale = HEAD_DIM ** -0.5
    qb = q.reshape(B, nblk, Q_BLOCK, N_HEADS, 2, HEAD_DIM).transpose(1, 0, 2, 3, 4, 5)

    def block(qi):
        s = jnp.einsum("bqhpd,bkhpd->bhpqk", qi, k, preferred_element_type=jnp.float32) * scale
        p = jax.nn.softmax(s, axis=-1)
        a = p[:, :, 0] - lam * p[:, :, 1]
        return jnp.einsum("bhqk,bkhe->bqhe", a.astype(v.dtype), v)

    o = lax.map(block, qb)
    return o.transpose(1, 0, 2, 3, 4).reshape(B, Lq, N_HEADS, V_DIM)


def diff_out(o, subln_g, lam_init, w_o):
    o = rmsnorm(o, subln_g, SUBLN_EPS) * (1.0 - lam_init)
    B, L = o.shape[:2]
    return o.reshape(B, L, N_HEADS * V_DIM) @ w_o


def implicit_filter(L, w1, b1, w2, b2, w3, b3, freq):
    f32 = jnp.float32
    pos = jnp.arange(L, dtype=f32)
    t = pos / max(L - 1, 1)
    w = 2.0 * math.pi * pos / L
    bands = jnp.linspace(1e-4, FILT_BANDS - 1, FILT_BANDS, dtype=f32)
    z = jnp.concatenate([t[:, None], jnp.cos(w[:, None] * bands), -jnp.sin(w[:, None] * bands)], axis=-1)
    fr = freq.astype(f32)
    hdn = jnp.sin(fr * (z @ w1.astype(f32) + b1.astype(f32)))
    hdn = jnp.sin(fr * (hdn @ w2.astype(f32) + b2.astype(f32)))
    h = (hdn @ w3.astype(f32) + b3.astype(f32)).reshape(L, 2, D_MODEL)
    deltas = jnp.abs(jnp.linspace(math.log(FILT_TARGET) / FILT_SLOW_PCT,
                                  math.log(FILT_TARGET) / FILT_FAST_PCT, D_MODEL, dtype=f32))
    h = h * jnp.exp(-t[:, None, None] * deltas)
    return h / (jnp.sum(jnp.abs(h), axis=(0, 1), keepdims=True) + FILT_EPS)


def bidir_fftconv(v, filt):
    B, L, D = v.shape
    n = 2 * L
    h_f, h_b = filt[:, 0], filt[:, 1]
    kk = jnp.concatenate([h_f, jnp.zeros((1, D), filt.dtype), h_b[:0:-1]], axis=0)
    vf = jnp.fft.rfft(v.astype(jnp.float32), n=n, axis=1)
    kf = jnp.fft.rfft(kk, n=n, axis=0)
    return jnp.fft.irfft(vf * kf[None], n=n, axis=1)[:, :L]


def hyena(h, w_in, b_in, conv_w, conv_b, fw1, fb1, fw2, fb2, fw3, fb3, freq, skip, w_out, b_out):
    B, L, D = h.shape
    u = h @ w_in + b_in
    up = jnp.pad(u, ((0, 0), (1, 1), (0, 0)))
    u = up[:, :-2] * conv_w[0] + up[:, 1:-1] * conv_w[1] + up[:, 2:] * conv_w[2] + conv_b
    x0, x1, v = jnp.split(u, 3, axis=-1)
    filt = implicit_filter(L, fw1, fb1, fw2, fb2, fw3, fb3, freq)
    v = v * x1
    y = bidir_fftconv(v, filt).astype(h.dtype) + v * skip
    return (y * x0) @ w_out + b_out


def setup_inputs(seed: int = 0) -> dict:
    key = jax.random.key(seed)
    ks = jax.random.split(key, 40)
    f32 = jnp.float32

    def nrm(i, shape, scale=1.0):
        return jax.random.normal(ks[i], shape, f32) * scale

    D = D_MODEL
    NA, NH = N_ATTN_LAYERS, N_HYENA_LAYERS
    return {
        "x_prompt": nrm(0, (BATCH, SEQ, D)),
        "x_sample": nrm(1, (DEC_BATCH, DEC_SEQ, D)),
        "cache_k": nrm(2, (DEC_BATCH, NA, PAST_LEN, N_HEADS, 2, HEAD_DIM)),
        "cache_v": nrm(3, (DEC_BATCH, NA, PAST_LEN, N_HEADS, V_DIM)),
        "c": nrm(4, (DEC_BATCH, D)),
        "c_ctx": nrm(5, (D,)),
        "ada_w": nrm(6, (DEPTH, D, 6 * D), 0.5 * D ** -0.5),
        "ada_b": nrm(7, (DEPTH, 6 * D), 0.01),
        "norm1_g": 1.0 + nrm(8, (DEPTH, D), 0.01),
        "norm2_g": 1.0 + nrm(9, (DEPTH, D), 0.01),
        "attn_w_qkv": nrm(10, (NA, D, 3 * D), D ** -0.5),
        "attn_lambda": nrm(11, (NA, 4, HEAD_DIM), 0.1),
        "attn_subln_g": 1.0 + nrm(12, (NA, V_DIM), 0.01),
        "attn_w_o": nrm(13, (NA, D, D), D ** -0.5),
        "hy_w_in": nrm(14, (NH, D, 3 * D), D ** -0.5),
        "hy_b_in": nrm(15, (NH, 3 * D), 0.01),
        "hy_conv_w": nrm(16, (NH, 3, 3 * D), 3 ** -0.5),
        "hy_conv_b": nrm(17, (NH, 3 * D), 0.01),
        "filt_w1": nrm(18, (NH, FILT_EMB, FILT_ORDER), FILT_EMB ** -0.5),
        "filt_b1": nrm(19, (NH, FILT_ORDER), 0.1),
        "filt_w2": nrm(20, (NH, FILT_ORDER, FILT_ORDER), FILT_ORDER ** -0.5),
        "filt_b2": nrm(21, (NH, FILT_ORDER), 0.1),
        "filt_w3": nrm(22, (NH, FILT_ORDER, 2 * D), FILT_ORDER ** -0.5),
        "filt_b3": nrm(23, (NH, 2 * D), 0.1),
        "filt_freq": 1.0 + nrm(24, (NH, FILT_ORDER), 0.01),
        "hy_skip": nrm(25, (NH, D)),
        "hy_w_out": nrm(26, (NH, D, D), D ** -0.5),
        "hy_b_out": nrm(27, (NH, D), 0.01),
        "ffn_w_gu": nrm(28, (DEPTH, D, 2 * D_FF), D ** -0.5),
        "ffn_w_down": nrm(29, (DEPTH, D_FF, D), D_FF ** -0.5),
        "final_g": 1.0 + nrm(30, (D,), 0.01),
    }


def reference(x_prompt, x_sample, cache_k, cache_v, c, c_ctx, ada_w, ada_b, norm1_g, norm2_g,
              attn_w_qkv, attn_lambda, attn_subln_g, attn_w_o,
              hy_w_in, hy_b_in, hy_conv_w, hy_conv_b, filt_w1, filt_b1, filt_w2, filt_b2,
              filt_w3, filt_b3, filt_freq, hy_skip, hy_w_out, hy_b_out,
              ffn_w_gu, ffn_w_down, final_g):
    def hy_params(j):
        return (hy_w_in[j], hy_b_in[j], hy_conv_w[j], hy_conv_b[j], filt_w1[j], filt_b1[j],
                filt_w2[j], filt_b2[j], filt_w3[j], filt_b3[j], filt_freq[j], hy_skip[j],
                hy_w_out[j], hy_b_out[j])

    xp = x_prompt
    ctx_k, ctx_v = [], []
    for i in range(DEPTH):
        j = i // N_MIXERS
        sh1, sc1, g1, sh2, sc2, g2 = adaln(c_ctx, ada_w[i], ada_b[i])
        h = modulate(rmsnorm(xp, norm1_g[i]), sh1, sc1)
        if i % N_MIXERS == 0:
            lam, lam_init = diff_lambda(attn_lambda[j], i)
            q, k, v = diff_qkv(h, attn_w_qkv[j])
            out = diff_out(diff_attend(q, k, v, lam), attn_subln_g[j], lam_init, attn_w_o[j])
            ctx_k.append(k)
            ctx_v.append(v)
        else:
            out = hyena(h, *hy_params(j))
        xp = xp + g1 * out
        h = modulate(rmsnorm(xp, norm2_g[i]), sh2, sc2)
        xp = xp + g2 * swiglu(h, ffn_w_gu[i], ffn_w_down[i])
    y_prompt = rmsnorm(xp, final_g)
    new_cache_k = jnp.stack(ctx_k, axis=1)
    new_cache_v = jnp.stack(ctx_v, axis=1)

    xs = x_sample
    cos, sin = axial_rope(xs.shape[1])
    cmod = c[:, None, :]
    for i in range(DEPTH):
        j = i // N_MIXERS
        sh1, sc1, g1, sh2, sc2, g2 = adaln(cmod, ada_w[i], ada_b[i])
        h = modulate(rmsnorm(xs, norm1_g[i]), sh1, sc1)
        if i % N_MIXERS == 0:
            lam, lam_init = diff_lambda(attn_lambda[j], i)
            q, k, v = diff_qkv(h, attn_w_qkv[j])
            q = apply_rope(q, cos, sin)
            k = apply_rope(k, cos, sin)
            k_all = jnp.concatenate([cache_k[:, j].astype(k.dtype), k], axis=1)
            v_all = jnp.concatenate([cache_v[:, j].astype(v.dtype), v], axis=1)
            out = diff_out(diff_attend(q, k_all, v_all, lam), attn_subln_g[j], lam_init, attn_w_o[j])
        else:
            out = hyena(h, *hy_params(j))
        xs = xs + g1 * out
        h = modulate(rmsnorm(xs, norm2_g[i]), sh2, sc2)
        xs = xs + g2 * swiglu(h, ffn_w_gu[i], ffn_w_down[i])
    y_sample = rmsnorm(xs, final_g)

    return (y_prompt, y_sample, new_cache_k, new_cache_v)
```

```python
import functools
import math

import jax
import jax.numpy as jnp
from jax import lax
from jax.experimental import pallas as pl
from jax.experimental.pallas import tpu as pltpu

D_MODEL = 1024
DEPTH = 4
GRID_W = 64
N_HEADS = 8
HEAD_DIM = 64
V_DIM = 2 * HEAD_DIM
ROPE_PAIRS = HEAD_DIM // 4
ROPE_BASE = 10000.0
FILT_BANDS = 16
FILT_ORDER = 64
FILT_TARGET = 1e-2
FILT_FAST_PCT = 0.3
FILT_SLOW_PCT = 1.5
FILT_EPS = 1e-6
D_FF = 2816
EPS = 1e-6
SUBLN_EPS = 1e-5

LANES = 128
MOD_ROWS = 16
CTX_ROW = 8
VMEM_LIMIT = 56 * 1024 * 1024

F32 = jnp.float32
BF16 = jnp.bfloat16
HI = lax.Precision.HIGHEST


def _params(*sem):
    return pltpu.CompilerParams(dimension_semantics=sem, vmem_limit_bytes=VMEM_LIMIT)


def _norm_mod(x, g, shift, scale):
    ms = jnp.mean(x * x, axis=-1, keepdims=True)
    return (x * lax.rsqrt(ms + EPS) * g) * (1.0 + scale) + shift


def _mod_kernel(c_ref, w_ref, b_ref, o_ref):
    c = c_ref[...]
    a = c * jax.nn.sigmoid(c)
    o_ref[...] = jnp.dot(a, w_ref[...], preferred_element_type=F32, precision=HI) + b_ref[...]


def _adaln_all(cmat, ada_w, ada_b):
    n6 = 6 * D_MODEL
    tn = 1536
    return pl.pallas_call(
        _mod_kernel,
        grid=(DEPTH, n6 // tn),
        in_specs=[
            pl.BlockSpec((MOD_ROWS, D_MODEL), lambda i, n: (0, 0)),
            pl.BlockSpec((None, D_MODEL, tn), lambda i, n: (i, 0, n)),
            pl.BlockSpec((None, 1, tn), lambda i, n: (i, 0, n)),
        ],
        out_specs=pl.BlockSpec((None, MOD_ROWS, tn), lambda i, n: (i, 0, n)),
        out_shape=jax.ShapeDtypeStruct((DEPTH, MOD_ROWS, n6), F32),
        compiler_params=_params("parallel", "parallel"),
        name="adaln",
    )(cmat, ada_w, ada_b.reshape(DEPTH, 1, n6))


def _rope_chunk(x, cos, sin_a, sin_b):
    return x * cos + pltpu.roll(x, LANES - ROPE_PAIRS, 1) * sin_a + pltpu.roll(x, ROPE_PAIRS, 1) * sin_b


def _qkv_kernel(*refs, rope, tn):
    if rope:
        x_ref, g_ref, sh_ref, sc_ref, w_ref, cos_ref, sa_ref, sb_ref, q_ref, k_ref, v_ref = refs
    else:
        x_ref, g_ref, sh_ref, sc_ref, w_ref, q_ref, k_ref, v_ref = refs
    h = _norm_mod(x_ref[...], g_ref[...], sh_ref[...], sc_ref[...]).astype(BF16)
    outs = (q_ref, k_ref, v_ref)
    qk_scale = HEAD_DIM ** -0.5
    for part in range(3):
        for n in range(D_MODEL // tn):
            col = part * D_MODEL + n * tn
            y = jnp.dot(h, w_ref[:, col:col + tn], preferred_element_type=F32)
            for j in range(tn // LANES):
                yj = y[:, j * LANES:(j + 1) * LANES]
                if rope and part < 2:
                    yj = _rope_chunk(yj, cos_ref[...], sa_ref[...], sb_ref[...])
                if part == 0:
                    yj = yj * qk_scale
                lo = n * tn + j * LANES
                outs[part][:, lo:lo + LANES] = yj.astype(outs[part].dtype)


def _qkv(x, g, shift, scale, w, rowmap, rope_tabs, kv_dtype, tm):
    B, L, D = x.shape
    rope = rope_tabs is not None
    row = pl.BlockSpec((None, tm, D), lambda b, i: (b, i, 0))
    vec = pl.BlockSpec((None, 1, D), lambda b, i: (rowmap(b), 0, 0))
    in_specs = [row, pl.BlockSpec((1, D), lambda b, i: (0, 0)), vec, vec,
                pl.BlockSpec((D, 3 * D), lambda b, i: (0, 0))]
    args = [x, g, shift, scale, w]
    if rope:
        tab = pl.BlockSpec((tm, LANES), lambda b, i: (i, 0))
        in_specs += [tab, tab, tab]
        args += list(rope_tabs)
    return pl.pallas_call(
        functools.partial(_qkv_kernel, rope=rope, tn=512),
        grid=(B, L // tm),
        in_specs=in_specs,
        out_specs=[row, row, row],
        out_shape=[jax.ShapeDtypeStruct((B, L, D), BF16),
                   jax.ShapeDtypeStruct((B, L, D), kv_dtype),
                   jax.ShapeDtypeStruct((B, L, D), kv_dtype)],
        compiler_params=_params("parallel", "parallel"),
        name="qkv",
    )(*args)


def _in_proj_kernel(x_ref, g_ref, sh_ref, sc_ref, w_ref, b_ref, o_ref, *, tn):
    h = _norm_mod(x_ref[...], g_ref[...], sh_ref[...], sc_ref[...]).astype(BF16)
    for n in range(o_ref.shape[-1] // tn):
        sl = slice(n * tn, (n + 1) * tn)
        o_ref[:, sl] = jnp.dot(h, w_ref[:, sl], preferred_element_type=F32) + b_ref[:, sl]


def _in_proj(x, g, shift, scale, w, b, rowmap, tm):
    B, L, D = x.shape
    N = w.shape[1]
    vec = pl.BlockSpec((None, 1, D), lambda b_, i: (rowmap(b_), 0, 0))
    return pl.pallas_call(
        functools.partial(_in_proj_kernel, tn=512),
        grid=(B, L // tm),
        in_specs=[pl.BlockSpec((None, tm, D), lambda b_, i: (b_, i, 0)),
                  pl.BlockSpec((1, D), lambda b_, i: (0, 0)), vec, vec,
                  pl.BlockSpec((D, N), lambda b_, i: (0, 0)),
                  pl.BlockSpec((1, N), lambda b_, i: (0, 0))],
        out_specs=pl.BlockSpec((None, tm, N), lambda b_, i: (b_, i, 0)),
        out_shape=jax.ShapeDtypeStruct((B, L, N), F32),
        compiler_params=_params("parallel", "parallel"),
        name="hyena_in_proj",
    )(x, g, shift, scale, w, b)


def _attn_kernel(lam_ref, g_ref, q_ref, k_ref, v_ref, o_ref, m_sc, l_sc, acc_sc, *, tk, lam_init):
    q = q_ref[...]
    lane = lax.broadcasted_iota(jnp.int32, q.shape, 1)
    zero = jnp.zeros_like(q)
    q_maps = (jnp.where(lane < HEAD_DIM, q, zero), jnp.where(lane >= HEAD_DIM, q, zero))
    m_sc[...] = jnp.full(m_sc.shape, -jnp.inf, F32)
    l_sc[...] = jnp.zeros(l_sc.shape, F32)
    acc_sc[...] = jnp.zeros(acc_sc.shape, F32)

    def body(j, carry):
        off = pl.multiple_of(j * tk, tk)
        k = k_ref[pl.ds(off, tk), :].astype(BF16)
        v = v_ref[pl.ds(off, tk), :].astype(BF16)
        for p in range(2):
            s = lax.dot_general(q_maps[p], k, (((1,), (1,)), ((), ())), preferred_element_type=F32)
            m_prev = m_sc[p]
            m_new = jnp.maximum(m_prev, jnp.max(s, axis=1, keepdims=True))
            alpha = jnp.exp(m_prev - m_new)
            e = jnp.exp(s - pltpu.repeat(m_new, tk // LANES, axis=1))
            l_sc[p] = alpha * l_sc[p] + jnp.sum(e, axis=1, keepdims=True)
            acc_sc[p] = alpha * acc_sc[p] + jnp.dot(e.astype(v.dtype), v, preferred_element_type=F32)
            m_sc[p] = m_new
        return carry

    lax.fori_loop(0, k_ref.shape[0] // tk, body, 0)

    lp = lam_ref[...]
    lam = (jnp.exp(jnp.sum(lp[0:1] * lp[1:2], axis=1, keepdims=True))
           - jnp.exp(jnp.sum(lp[2:3] * lp[3:4], axis=1, keepdims=True)) + lam_init)
    o = acc_sc[0] / l_sc[0] - lam * (acc_sc[1] / l_sc[1])
    ms = jnp.mean(o * o, axis=-1, keepdims=True)
    o_ref[...] = ((o * lax.rsqrt(ms + SUBLN_EPS) * g_ref[...]) * (1.0 - lam_init)).astype(o_ref.dtype)


def _attention(q, k, v, lam_params, subln_g, lam_init, tq, tk):
    B, Lq, D = q.shape
    Lk = k.shape[1]
    kv = pl.BlockSpec((None, Lk, LANES), lambda b, h, i: (b, 0, h))
    return pl.pallas_call(
        functools.partial(_attn_kernel, tk=tk, lam_init=lam_init),
        grid=(B, N_HEADS, Lq // tq),
        in_specs=[pl.BlockSpec((4, HEAD_DIM), lambda b, h, i: (0, 0)),
                  pl.BlockSpec((1, V_DIM), lambda b, h, i: (0, 0)),
                  pl.BlockSpec((None, tq, LANES), lambda b, h, i: (b, i, h)),
                  kv, kv],
        out_specs=pl.BlockSpec((None, tq, LANES), lambda b, h, i: (b, i, h)),
        out_shape=jax.ShapeDtypeStruct((B, Lq, D), BF16),
        scratch_shapes=[pltpu.VMEM((2, tq, LANES), F32),
                        pltpu.VMEM((2, tq, LANES), F32),
                        pltpu.VMEM((2, tq, V_DIM), F32)],
        compiler_params=_params("parallel", "parallel", "parallel"),
        name="diff_attention",
    )(lam_params, subln_g, q, k, v)


def _proj_res_kernel(a_ref, w_ref, x_ref, gate_ref, o_ref):
    y = jnp.dot(a_ref[...], w_ref[...], preferred_element_type=F32)
    o_ref[...] = x_ref[...] + gate_ref[...] * y


def _proj_res(a, w, x, gate, rowmap, tm):
    B, L, D = x.shape
    row = pl.BlockSpec((None, tm, D), lambda b, i: (b, i, 0))
    return pl.pallas_call(
        _proj_res_kernel,
        grid=(B, L // tm),
        in_specs=[row, pl.BlockSpec((D, D), lambda b, i: (0, 0)), row,
                  pl.BlockSpec((None, 1, D), lambda b, i: (rowmap(b), 0, 0))],
        out_specs=row,
        out_shape=jax.ShapeDtypeStruct((B, L, D), F32),
        compiler_params=_params("parallel", "parallel"),
        name="attn_out_proj",
    )(a, w, x, gate)


def _ffn_kernel(*refs, fc, final):
    if final:
        x_ref, g_ref, sh_ref, sc_ref, gate_ref, wgu_ref, wd_ref, fg_ref, o_ref = refs
    else:
        x_ref, g_ref, sh_ref, sc_ref, gate_ref, wgu_ref, wd_ref, o_ref = refs
    x = x_ref[...]
    h = _norm_mod(x, g_ref[...], sh_ref[...], sc_ref[...]).astype(BF16)
    acc = jnp.zeros(x.shape, F32)
    for f in range(D_FF // fc):
        gg = jnp.dot(h, wgu_ref[:, f * fc:(f + 1) * fc], preferred_element_type=F32)
        uu = jnp.dot(h, wgu_ref[:, D_FF + f * fc:D_FF + (f + 1) * fc], preferred_element_type=F32)
        a = (gg * jax.nn.sigmoid(gg)) * uu
        acc = acc + jnp.dot(a.astype(BF16), wd_ref[f * fc:(f + 1) * fc, :], preferred_element_type=F32)
    y = x + gate_ref[...] * acc
    if final:
        ms = jnp.mean(y * y, axis=-1, keepdims=True)
        y = y * lax.rsqrt(ms + EPS) * fg_ref[...]
    o_ref[...] = y


def _ffn(x, g, shift, scale, gate, w_gu, w_down, final_g, rowmap, tm):
    B, L, D = x.shape
    final = final_g is not None
    row = pl.BlockSpec((None, tm, D), lambda b, i: (b, i, 0))
    vec = pl.BlockSpec((None, 1, D), lambda b, i: (rowmap(b), 0, 0))
    const = pl.BlockSpec((1, D), lambda b, i: (0, 0))
    in_specs = [row, const, vec, vec, vec,
                pl.BlockSpec((D, 2 * D_FF), lambda b, i: (0, 0)),
                pl.BlockSpec((D_FF, D), lambda b, i: (0, 0))]
    args = [x, g, shift, scale, gate, w_gu, w_down]
    if final:
        in_specs.append(const)
        args.append(final_g)
    return pl.pallas_call(
        functools.partial(_ffn_kernel, fc=256, final=final),
        grid=(B, L // tm),
        in_specs=in_specs,
        out_specs=row,
        out_shape=jax.ShapeDtypeStruct((B, L, D), F32),
        compiler_params=_params("parallel", "parallel"),
        name="ffn",
    )(*args)


def _conv3(u, w, b):
    L = u.shape[0]
    row = lax.broadcasted_iota(jnp.int32, u.shape, 0)
    prev = jnp.where(row == 0, 0.0, pltpu.roll(u, 1, 0))
    nxt = jnp.where(row == L - 1, 0.0, pltpu.roll(u, L - 1, 0))
    return prev * w[0:1] + u * w[1:2] + nxt * w[2:3] + b


def _short_conv_kernel(u0_ref, u1_ref, u2_ref, w0_ref, w1_ref, w2_ref, b0_ref, b1_ref, b2_ref,
                       vx_ref, vxb_ref, x0_ref):
    x0_ref[...] = _conv3(u0_ref[...], w0_ref[...], b0_ref[...])
    vx = _conv3(u2_ref[...], w2_ref[...], b2_ref[...]) * _conv3(u1_ref[...], w1_ref[...], b1_ref[...])
    vx_ref[...] = vx
    vxb_ref[...] = vx.astype(BF16)


def _short_conv(u, conv_w, conv_b, tc):
    B, L, D3 = u.shape
    D = D3 // 3
    nb = D // tc
    specs = []
    for arr_block in ((None, L, tc), (3, tc), (1, tc)):
        for part in range(3):
            if len(arr_block) == 3:
                specs.append(pl.BlockSpec(arr_block, lambda b, c, part=part: (b, 0, part * nb + c)))
            else:
                specs.append(pl.BlockSpec(arr_block, lambda b, c, part=part: (0, part * nb + c)))
    out = pl.BlockSpec((None, L, tc), lambda b, c: (b, 0, c))
    return pl.pallas_call(
        _short_conv_kernel,
        grid=(B, nb),
        in_specs=specs,
        out_specs=[out, out, out],
        out_shape=[jax.ShapeDtypeStruct((B, L, D), F32),
                   jax.ShapeDtypeStruct((B, L, D), BF16),
                   jax.ShapeDtypeStruct((B, L, D), F32)],
        compiler_params=_params("parallel", "parallel"),
        name="hyena_short_conv",
    )(u, u, u, conv_w, conv_w, conv_w, conv_b, conv_b, conv_b)


def _filter_kernel(bands_ref, w1t_ref, w1c_ref, w1s_ref, b1_ref, w2_ref, b2_ref, fr_ref,
                   w3f_ref, w3b_ref, b3f_ref, b3b_ref, delta_ref, h_ref, hid_sc):
    L = hid_sc.shape[0]
    pos = lax.broadcasted_iota(jnp.int32, (L, LANES), 0).astype(F32)
    t = pos / max(L - 1, 1)

    @pl.when(pl.program_id(0) == 0)
    def _():
        ang = (2.0 * math.pi * pos / L) * bands_ref[...]
        fr = fr_ref[...]
        pre = (t * w1t_ref[...]
               + jnp.dot(jnp.cos(ang), w1c_ref[...], preferred_element_type=F32, precision=HI)
               - jnp.dot(jnp.sin(ang), w1s_ref[...], preferred_element_type=F32, precision=HI)
               + b1_ref[...])
        hid = jnp.sin(fr * pre)
        hid_sc[...] = jnp.sin(fr * (jnp.dot(hid, w2_ref[...], preferred_element_type=F32, precision=HI)
                                    + b2_ref[...]))

    hid = hid_sc[...]
    window = jnp.exp(-t[:, 0:1] * delta_ref[...])
    hf = (jnp.dot(hid, w3f_ref[...], preferred_element_type=F32, precision=HI) + b3f_ref[...]) * window
    hb = (jnp.dot(hid, w3b_ref[...], preferred_element_type=F32, precision=HI) + b3b_ref[...]) * window
    norm = (jnp.sum(jnp.abs(hf), axis=0, keepdims=True) + jnp.sum(jnp.abs(hb), axis=0, keepdims=True)
            + FILT_EPS)
    row = lax.broadcasted_iota(jnp.int32, hb.shape, 0)
    h_ref[0] = (hf / norm).astype(h_ref.dtype)
    h_ref[1] = jnp.where(row == 0, 0.0, hb / norm).astype(h_ref.dtype)


def _pad2(a, rows, cols):
    return jnp.pad(a, ((0, rows - a.shape[0]), (0, cols - a.shape[1])))


def _implicit_filter(L, w1, b1, w2, b2, w3, b3, freq, tc):
    D = D_MODEL
    bands = _pad2(jnp.linspace(1e-4, FILT_BANDS - 1, FILT_BANDS, dtype=F32)[None], 1, LANES)
    deltas = jnp.abs(jnp.linspace(math.log(FILT_TARGET) / FILT_SLOW_PCT,
                                  math.log(FILT_TARGET) / FILT_FAST_PCT, D, dtype=F32))[None]
    w1 = w1.astype(F32)
    args = [bands,
            _pad2(w1[0:1], 1, LANES),
            _pad2(w1[1:1 + FILT_BANDS], LANES, LANES),
            _pad2(w1[1 + FILT_BANDS:], LANES, LANES),
            _pad2(b1[None], 1, LANES), _pad2(w2, LANES, LANES), _pad2(b2[None], 1, LANES),
            _pad2(freq[None], 1, LANES),
            _pad2(w3, LANES, 2 * D), _pad2(w3, LANES, 2 * D), b3[None], b3[None], deltas]
    small = lambda shape: pl.BlockSpec(shape, lambda c: (0, 0))
    nb = D // tc
    in_specs = [small((1, LANES)), small((1, LANES)), small((LANES, LANES)), small((LANES, LANES)),
                small((1, LANES)), small((LANES, LANES)), small((1, LANES)), small((1, LANES)),
                pl.BlockSpec((LANES, tc), lambda c: (0, c)),
                pl.BlockSpec((LANES, tc), lambda c: (0, nb + c)),
                pl.BlockSpec((1, tc), lambda c: (0, c)),
                pl.BlockSpec((1, tc), lambda c: (0, nb + c)),
                pl.BlockSpec((1, tc), lambda c: (0, c))]
    return pl.pallas_call(
        _filter_kernel,
        grid=(nb,),
        in_specs=in_specs,
        out_specs=pl.BlockSpec((2, L, tc), lambda c: (0, 0, c)),
        out_shape=jax.ShapeDtypeStruct((2, L, D), BF16),
        scratch_shapes=[pltpu.VMEM((L, LANES), F32)],
        compiler_params=_params("arbitrary"),
        name="hyena_filter",
    )(*args)


def _dft_gen_kernel(fre_ref, fim_ref, gim_ref, cb_sc, sb_sc, *, L):
    n = 2 * L
    tr = fre_ref.shape[0]
    step = 2.0 * math.pi / n
    i = pl.program_id(0)
    row = lax.broadcasted_iota(jnp.int32, (tr, L), 0)
    col = lax.broadcasted_iota(jnp.int32, (tr, L), 1)

    @pl.when(i == 0)
    def _():
        ang = ((row * col) & (n - 1)).astype(F32) * step
        cb_sc[...] = jnp.cos(ang)
        sb_sc[...] = jnp.sin(ang)

    col1 = lax.broadcasted_iota(jnp.int32, (1, L), 1)
    a = (((i * tr) * col1) & (n - 1)).astype(F32) * step
    ca, sa = jnp.cos(a), jnp.sin(a)
    cb, sb = cb_sc[...], sb_sc[...]
    c = cb * ca - sb * sa
    s = -(sb * ca + cb * sa)
    grow = row + i * tr
    fre_ref[...] = c.astype(fre_ref.dtype)
    fim_ref[...] = jnp.where(grow == 0, jnp.where((col & 1) == 0, 1.0, -1.0), s).astype(fim_ref.dtype)
    gim_ref[...] = jnp.where(col == 0, jnp.where((grow & 1) == 0, 1.0, -1.0), s).astype(gim_ref.dtype)


def _dft_tables(L, tr):
    spec = pl.BlockSpec((tr, L), lambda i: (i, 0))
    shape = jax.ShapeDtypeStruct((L, L), BF16)
    return pl.pallas_call(
        functools.partial(_dft_gen_kernel, L=L),
        grid=(L // tr,),
        out_specs=[spec, spec, spec],
        out_shape=[shape, shape, shape],
        scratch_shapes=[pltpu.VMEM((tr, L), F32), pltpu.VMEM((tr, L), F32)],
        compiler_params=_params("arbitrary"),
        name="dft_tables",
    )()


def _fwd_dft_kernel(*refs, with_filter):
    if with_filter:
        fre_ref, fim_ref, v_ref, kr_ref, ki_ref, y_ref = refs
    else:
        fre_ref, fim_ref, v_ref, y_ref = refs
    v = v_ref[...]
    vr = jnp.dot(fre_ref[...], v, preferred_element_type=F32)
    vi = jnp.dot(fim_ref[...], v, preferred_element_type=F32)
    if with_filter:
        kr, ki = kr_ref[...], ki_ref[...]
        tm = vr.shape[0]
        first = (lax.broadcasted_iota(jnp.int32, vr.shape, 0) + pl.program_id(1) * tm) == 0
        yr = vr * kr - jnp.where(first, 0.0, vi * ki)
        yi = jnp.where(first, vi * ki, vr * ki + vi * kr)
        y_ref[0] = yr.astype(y_ref.dtype)
        y_ref[1] = yi.astype(y_ref.dtype)
    else:
        y_ref[0] = vr.astype(y_ref.dtype)
        y_ref[1] = vi.astype(y_ref.dtype)


def _fwd_dft(fre, fim, v, kf, tm, out_dtype):
    B, L, D = v.shape
    ftile = pl.BlockSpec((tm, L), lambda b, m: (m, 0))
    in_specs = [ftile, ftile, pl.BlockSpec((None, L, D), lambda b, m: (b, 0, 0))]
    args = [fre, fim, v]
    if kf is not None:
        in_specs += [pl.BlockSpec((None, tm, D), lambda b, m: (0, m, 0)),
                     pl.BlockSpec((None, tm, D), lambda b, m: (1, m, 0))]
        args += [kf, kf]
    return pl.pallas_call(
        functools.partial(_fwd_dft_kernel, with_filter=kf is not None),
        grid=(B, L // tm),
        in_specs=in_specs,
        out_specs=pl.BlockSpec((None, 2, tm, D), lambda b, m: (b, 0, m, 0)),
        out_shape=jax.ShapeDtypeStruct((B, 2, L, D), out_dtype),
        compiler_params=_params("parallel", "parallel"),
        name="hyena_fwd_dft",
    )(*args)


def _filter_spectrum_kernel(hf_ref, hb_ref, k_ref, *, n):
    hfr, hfi = hf_ref[0], hf_ref[1]
    hbr, hbi = hb_ref[0], hb_ref[1]
    first = (lax.broadcasted_iota(jnp.int32, hfr.shape, 0) + pl.program_id(0) * hfr.shape[0]) == 0
    wgt = jnp.where(first, 1.0 / n, 2.0 / n)
    k_ref[0] = (hfr + hbr) * wgt
    k_ref[1] = jnp.where(first, hfi + hbi, hfi - hbi) * wgt


def _filter_spectrum(hspec, tm):
    _, _, L, D = hspec.shape
    return pl.pallas_call(
        functools.partial(_filter_spectrum_kernel, n=2 * L),
        grid=(L // tm,),
        in_specs=[pl.BlockSpec((None, 2, tm, D), lambda m: (0, 0, m, 0)),
                  pl.BlockSpec((None, 2, tm, D), lambda m: (1, 0, m, 0))],
        out_specs=pl.BlockSpec((2, tm, D), lambda m: (0, m, 0)),
        out_shape=jax.ShapeDtypeStruct((2, L, D), F32),
        compiler_params=_params("parallel"),
        name="hyena_filter_spectrum",
    )(hspec, hspec)


def _inv_dft_kernel(gre_ref, gim_ref, yr_ref, yi_ref, vx_ref, x0_ref, skip_ref, w_ref, b_ref,
                    x_ref, gate_ref, o_ref, acc_sc):
    kk = pl.program_id(2)

    @pl.when(kk == 0)
    def _():
        acc_sc[...] = jnp.zeros(acc_sc.shape, F32)

    acc_sc[...] += (jnp.dot(gre_ref[...], yr_ref[...], preferred_element_type=F32)
                    + jnp.dot(gim_ref[...], yi_ref[...], preferred_element_type=F32))

    @pl.when(kk == pl.num_programs(2) - 1)
    def _():
        z = (acc_sc[...] + vx_ref[...] * skip_ref[...]) * x0_ref[...]
        out = jnp.dot(z.astype(BF16), w_ref[...], preferred_element_type=F32) + b_ref[...]
        o_ref[...] = x_ref[...] + gate_ref[...] * out


def _inv_dft(gre, gim, y, vx, x0, skip, w_out, b_out, x, gate, rowmap, tm, tk):
    B, L, D = x.shape
    gtile = pl.BlockSpec((tm, tk), lambda b, i, k: (i, k))
    row = pl.BlockSpec((None, tm, D), lambda b, i, k: (b, i, 0))
    const = pl.BlockSpec((1, D), lambda b, i, k: (0, 0))
    return pl.pallas_call(
        _inv_dft_kernel,
        grid=(B, L // tm, L // tk),
        in_specs=[gtile, gtile,
                  pl.BlockSpec((None, None, tk, D), lambda b, i, k: (b, 0, k, 0)),
                  pl.BlockSpec((None, None, tk, D), lambda b, i, k: (b, 1, k, 0)),
                  row, row, const,
                  pl.BlockSpec((D, D), lambda b, i, k: (0, 0)), const,
                  row, pl.BlockSpec((None, 1, D), lambda b, i, k: (rowmap(b), 0, 0))],
        out_specs=row,
        out_shape=jax.ShapeDtypeStruct((B, L, D), F32),
        scratch_shapes=[pltpu.VMEM((tm, D), F32)],
        compiler_params=_params("parallel", "parallel", "arbitrary"),
        name="hyena_inv_dft",
    )(gre, gim, y, y, vx, x0, skip, w_out, b_out, x, gate)


def _rope_tables(L):
    pos = jnp.arange(L, dtype=jnp.int32)
    rows = (pos // GRID_W).astype(F32)
    cols = (pos % GRID_W).astype(F32)
    inv = ROPE_BASE ** (-jnp.arange(ROPE_PAIRS, dtype=F32) / ROPE_PAIRS)
    lane = jnp.arange(LANES)
    within = lane % HEAD_DIM
    axis = within // (2 * ROPE_PAIRS)
    half = (within % (2 * ROPE_PAIRS)) // ROPE_PAIRS
    ang = jnp.where(axis[None, :] == 0, rows[:, None], cols[:, None]) * inv[within % ROPE_PAIRS][None, :]
    cos, sin = jnp.cos(ang), jnp.sin(ang)
    sin_a = jnp.where(half[None, :] == 0, -sin, 0.0)
    sin_b = jnp.where(half[None, :] == 1, sin, 0.0)
    return cos, sin_a, sin_b


def _tile(L, pref):
    return min(L, pref)


def _run_group(x, rowmap, mods, p, cache, rope_tabs, dft, kv_dtype):
    B, L, D = x.shape
    tm = _tile(L, 512)
    ctx_k, ctx_v = [], []
    for i in range(DEPTH):
        j = i // 2
        sh1, sc1, g1, sh2, sc2, g2 = (mods[i, :, s * D:(s + 1) * D].reshape(MOD_ROWS, 1, D) for s in range(6))
        n1 = p["norm1_g"][i][None]
        if i % 2 == 0:
            lam_init = 0.8 - 0.6 * math.exp(-0.3 * i)
            q, k, v = _qkv(x, n1, sh1, sc1, p["attn_w_qkv"][j], rowmap, rope_tabs, kv_dtype, tm)
            ctx_k.append(k)
            ctx_v.append(v)
            if cache is not None:
                ck, cv = cache
                k = jnp.concatenate([ck[:, j].reshape(B, -1, D).astype(k.dtype), k], axis=1)
                v = jnp.concatenate([cv[:, j].reshape(B, -1, D).astype(v.dtype), v], axis=1)
            o = _attention(q, k, v, p["attn_lambda"][j].astype(F32), p["attn_subln_g"][j][None],
                           lam_init, _tile(L, 512), _tile(k.shape[1], 512))
            x = _proj_res(o, p["attn_w_o"][j], x, g1, rowmap, tm)
        else:
            fre, fim, gim = dft
            u = _in_proj(x, n1, sh1, sc1, p["hy_w_in"][j], p["hy_b_in"][j][None], rowmap, tm)
            vx, vxb, x0 = _short_conv(u, p["hy_conv_w"][j], p["hy_conv_b"][j][None], LANES)
            filt = _implicit_filter(L, p["filt_w1"][j], p["filt_b1"][j], p["filt_w2"][j], p["filt_b2"][j],
                                    p["filt_w3"][j], p["filt_b3"][j], p["filt_freq"][j], 256)
            tmf = _tile(L, 256)
            kf = _filter_spectrum(_fwd_dft(fre, fim, filt, None, tmf, F32), tmf)
            y = _fwd_dft(fre, fim, vxb, kf, tmf, BF16)
            x = _inv_dft(fre, gim, y, vx, x0, p["hy_skip"][j][None], p["hy_w_out"][j],
                         p["hy_b_out"][j][None], x, g1, rowmap, tm, _tile(L, 512))
        final_g = p["final_g"][None] if i == DEPTH - 1 else None
        x = _ffn(x, p["norm2_g"][i][None], sh2, sc2, g2, p["ffn_w_gu"][i], p["ffn_w_down"][i],
                 final_g, rowmap, tm)
    return x, ctx_k, ctx_v


def kernel(x_prompt, x_sample, cache_k, cache_v, c, c_ctx, ada_w, ada_b, norm1_g, norm2_g, attn_w_qkv, attn_lambda, attn_subln_g, attn_w_o, hy_w_in, hy_b_in, hy_conv_w, hy_conv_b, filt_w1, filt_b1, filt_w2, filt_b2, filt_w3, filt_b3, filt_freq, hy_skip, hy_w_out, hy_b_out, ffn_w_gu, ffn_w_down, final_g):
    D = D_MODEL
    nb = c.shape[0]
    cmat = jnp.concatenate([c, c_ctx[None], jnp.zeros((MOD_ROWS - nb - 1, D), F32)], axis=0)
    mods = _adaln_all(cmat, ada_w, ada_b)

    p = dict(norm1_g=norm1_g, norm2_g=norm2_g, attn_lambda=attn_lambda, attn_subln_g=attn_subln_g,
             hy_b_in=hy_b_in, hy_conv_w=hy_conv_w, hy_conv_b=hy_conv_b,
             filt_w1=filt_w1, filt_b1=filt_b1, filt_w2=filt_w2, filt_b2=filt_b2, filt_w3=filt_w3,
             filt_b3=filt_b3, filt_freq=filt_freq, hy_skip=hy_skip, hy_b_out=hy_b_out, final_g=final_g,
             attn_w_qkv=attn_w_qkv.astype(BF16), attn_w_o=attn_w_o.astype(BF16),
             hy_w_in=hy_w_in.astype(BF16), hy_w_out=hy_w_out.astype(BF16),
             ffn_w_gu=ffn_w_gu.astype(BF16), ffn_w_down=ffn_w_down.astype(BF16))

    Lp, Ls = x_prompt.shape[1], x_sample.shape[1]
    y_prompt, ctx_k, ctx_v = _run_group(x_prompt, lambda b: CTX_ROW, mods, p, None, None,
                                        _dft_tables(Lp, min(Lp, 256)), F32)
    y_sample, _, _ = _run_group(x_sample, lambda b: b, mods, p, (cache_k, cache_v), _rope_tables(Ls),
                                _dft_tables(Ls, 256), BF16)

    Bp = x_prompt.shape[0]
    new_k = jnp.stack(ctx_k, axis=1).reshape(Bp, len(ctx_k), Lp, N_HEADS, 2, HEAD_DIM)
    new_v = jnp.stack(ctx_v, axis=1).reshape(Bp, len(ctx_v), Lp, N_HEADS, V_DIM)
    return y_prompt, y_sample, new_k, new_v
```

```python
import functools
import math

import jax
import jax.numpy as jnp
from jax import lax
from jax.experimental import pallas as pl
from jax.experimental.pallas import tpu as pltpu

D_MODEL = 1024
DEPTH = 4
GRID_W = 64
N_HEADS = 8
HEAD_DIM = 64
V_DIM = 2 * HEAD_DIM
ROPE_PAIRS = HEAD_DIM // 4
ROPE_BASE = 10000.0
FILT_BANDS = 16
FILT_ORDER = 64
FILT_TARGET = 1e-2
FILT_FAST_PCT = 0.3
FILT_SLOW_PCT = 1.5
FILT_EPS = 1e-6
D_FF = 2816
EPS = 1e-6
SUBLN_EPS = 1e-5

LANES = 128
MOD_ROWS = 16
CTX_ROW = 8
VMEM_LIMIT = 56 * 1024 * 1024

F32 = jnp.float32
BF16 = jnp.bfloat16
HI = lax.Precision.HIGHEST


def _params(*sem):
    return pltpu.CompilerParams(dimension_semantics=sem, vmem_limit_bytes=VMEM_LIMIT)


def _norm_mod(x, g, shift, scale):
    ms = jnp.mean(x * x, axis=-1, keepdims=True)
    return (x * lax.rsqrt(ms + EPS) * g) * (1.0 + scale) + shift


def _mod_kernel(c_ref, w_ref, b_ref, o_ref):
    c = c_ref[...]
    a = c * jax.nn.sigmoid(c)
    o_ref[...] = jnp.dot(a, w_ref[...], preferred_element_type=F32, precision=HI) + b_ref[...]


def _adaln_all(cmat, ada_w, ada_b):
    n6 = 6 * D_MODEL
    tn = 1536
    return pl.pallas_call(
        _mod_kernel,
        grid=(DEPTH, n6 // tn),
        in_specs=[
            pl.BlockSpec((MOD_ROWS, D_MODEL), lambda i, n: (0, 0)),
            pl.BlockSpec((None, D_MODEL, tn), lambda i, n: (i, 0, n)),
            pl.BlockSpec((None, 1, tn), lambda i, n: (i, 0, n)),
        ],
        out_specs=pl.BlockSpec((None, MOD_ROWS, tn), lambda i, n: (i, 0, n)),
        out_shape=jax.ShapeDtypeStruct((DEPTH, MOD_ROWS, n6), F32),
        compiler_params=_params("parallel", "parallel"),
        name="adaln",
    )(cmat, ada_w, ada_b.reshape(DEPTH, 1, n6))


def _rope_chunk(x, cos, sin_a, sin_b):
    return x * cos + pltpu.roll(x, LANES - ROPE_PAIRS, 1) * sin_a + pltpu.roll(x, ROPE_PAIRS, 1) * sin_b


def _qkv_kernel(*refs, rope, tn):
    if rope:
        x_ref, g_ref, sh_ref, sc_ref, w_ref, cos_ref, sa_ref, sb_ref, q_ref, k_ref, v_ref = refs
    else:
        x_ref, g_ref, sh_ref, sc_ref, w_ref, q_ref, k_ref, v_ref = refs
    h = _norm_mod(x_ref[...], g_ref[...], sh_ref[...], sc_ref[...]).astype(BF16)
    outs = (q_ref, k_ref, v_ref)
    qk_scale = HEAD_DIM ** -0.5 * math.log2(math.e)
    for part in range(3):
        for n in range(D_MODEL // tn):
            col = part * D_MODEL + n * tn
            y = jnp.dot(h, w_ref[:, col:col + tn], preferred_element_type=F32)
            for j in range(tn // LANES):
                yj = y[:, j * LANES:(j + 1) * LANES]
                if rope and part < 2:
                    yj = _rope_chunk(yj, cos_ref[...], sa_ref[...], sb_ref[...])
                if part == 0:
                    yj = yj * qk_scale
                lo = n * tn + j * LANES
                outs[part][:, lo:lo + LANES] = yj.astype(outs[part].dtype)


def _qkv(x, g, shift, scale, w, rowmap, rope_tabs, kv_dtype, tm):
    B, L, D = x.shape
    rope = rope_tabs is not None
    row = pl.BlockSpec((None, tm, D), lambda b, i: (b, i, 0))
    vec = pl.BlockSpec((None, 1, D), lambda b, i: (rowmap(b), 0, 0))
    in_specs = [row, pl.BlockSpec((1, D), lambda b, i: (0, 0)), vec, vec,
                pl.BlockSpec((D, 3 * D), lambda b, i: (0, 0))]
    args = [x, g, shift, scale, w]
    if rope:
        tab = pl.BlockSpec((tm, LANES), lambda b, i: (i, 0))
        in_specs += [tab, tab, tab]
        args += list(rope_tabs)
    return pl.pallas_call(
        functools.partial(_qkv_kernel, rope=rope, tn=512),
        grid=(B, L // tm),
        in_specs=in_specs,
        out_specs=[row, row, row],
        out_shape=[jax.ShapeDtypeStruct((B, L, D), BF16),
                   jax.ShapeDtypeStruct((B, L, D), kv_dtype),
                   jax.ShapeDtypeStruct((B, L, D), kv_dtype)],
        compiler_params=_params("parallel", "parallel"),
        name="qkv",
    )(*args)


def _in_proj_kernel(x_ref, g_ref, sh_ref, sc_ref, w_ref, b_ref, o_ref, *, tn):
    h = _norm_mod(x_ref[...], g_ref[...], sh_ref[...], sc_ref[...]).astype(BF16)
    for n in range(o_ref.shape[-1] // tn):
        sl = slice(n * tn, (n + 1) * tn)
        o_ref[:, sl] = jnp.dot(h, w_ref[:, sl], preferred_element_type=F32) + b_ref[:, sl]


def _in_proj(x, g, shift, scale, w, b, rowmap, tm):
    B, L, D = x.shape
    N = w.shape[1]
    vec = pl.BlockSpec((None, 1, D), lambda b_, i: (rowmap(b_), 0, 0))
    return pl.pallas_call(
        functools.partial(_in_proj_kernel, tn=512),
        grid=(B, L // tm),
        in_specs=[pl.BlockSpec((None, tm, D), lambda b_, i: (b_, i, 0)),
                  pl.BlockSpec((1, D), lambda b_, i: (0, 0)), vec, vec,
                  pl.BlockSpec((D, N), lambda b_, i: (0, 0)),
                  pl.BlockSpec((1, N), lambda b_, i: (0, 0))],
        out_specs=pl.BlockSpec((None, tm, N), lambda b_, i: (b_, i, 0)),
        out_shape=jax.ShapeDtypeStruct((B, L, N), F32),
        compiler_params=_params("parallel", "parallel"),
        name="hyena_in_proj",
    )(x, g, shift, scale, w, b)


def _attn_kernel(lam_ref, g_ref, q_ref, k_ref, v_ref, o_ref, m_sc, acc_sc, vx_sc, q2_sc,
                 s0_sc, s1_sc, s2_sc, t0_sc, t1_sc, t2_sc, *, tq, tk, lam_init):
    nk = k_ref.shape[0] // tk
    nq = q_ref.shape[0] // tq
    vx_sc[:, :V_DIM] = v_ref[...].astype(BF16)
    vx_sc[:, V_DIM:] = jnp.ones((vx_sc.shape[0], V_DIM), BF16)
    lp = lam_ref[...]
    lam = (jnp.exp(jnp.sum(lp[0:1] * lp[1:2], axis=1, keepdims=True))
           - jnp.exp(jnp.sum(lp[2:3] * lp[3:4], axis=1, keepdims=True)) + lam_init)
    lane = lax.broadcasted_iota(jnp.int32, (tq, LANES), 1)

    def slot(j):
        if j == 0:
            return s0_sc, t0_sc
        return (s1_sc, t1_sc) if j % 2 == 1 else (s2_sc, t2_sc)

    def load_q(i):
        q = q_ref[pl.ds(pl.multiple_of(i * tq, tq), tq), :]
        zero = jnp.zeros_like(q)
        q2_sc[:tq] = jnp.where(lane < HEAD_DIM, q, zero)
        q2_sc[tq:] = jnp.where(lane >= HEAD_DIM, q, zero)

    def scores(j):
        s_sc, t_sc = slot(j)
        s = lax.dot_general(q2_sc[...], k_ref[j * tk:(j + 1) * tk, :].astype(BF16),
                            (((1,), (1,)), ((), ())), preferred_element_type=F32)
        s_sc[...] = s
        t_sc[...] = jnp.broadcast_to(jnp.max(s, axis=1, keepdims=True), t_sc.shape)

    def accumulate(j):
        s_sc, t_sc = slot(j)
        m_prev = m_sc[...]
        m_new = jnp.maximum(m_prev, t_sc[...])
        alpha = jnp.exp2(m_prev - m_new)
        e = jnp.exp2(s_sc[...] - pltpu.repeat(m_new, tk // LANES, axis=1))
        acc_sc[...] = (pltpu.repeat(alpha, 2, axis=1) * acc_sc[...]
                       + jnp.dot(e.astype(BF16), vx_sc[j * tk:(j + 1) * tk, :], preferred_element_type=F32))
        m_sc[...] = m_new

    def init():
        m_sc[...] = jnp.full(m_sc.shape, -jnp.inf, F32)
        acc_sc[...] = jnp.zeros(acc_sc.shape, F32)

    def finalize(i):
        o = (acc_sc[:tq, :V_DIM] / acc_sc[:tq, V_DIM:] - lam * (acc_sc[tq:, :V_DIM] / acc_sc[tq:, V_DIM:]))
        ms = jnp.mean(o * o, axis=-1, keepdims=True)
        y = (o * lax.rsqrt(ms + SUBLN_EPS) * g_ref[...]) * (1.0 - lam_init)
        o_ref[pl.ds(pl.multiple_of(i * tq, tq), tq), :] = y.astype(o_ref.dtype)

    def query_tile(i, prefetch):
        for j in range(1, nk):
            accumulate(j - 1)
            scores(j)
        if prefetch:
            load_q(jnp.minimum(i + 1, nq - 1))
        accumulate(nk - 1)
        if prefetch:
            scores(0)
        finalize(i)
        init()

    init()
    load_q(0)
    scores(0)
    if nq == 1:
        query_tile(0, False)
    else:
        def body(i, carry):
            query_tile(i, True)
            return carry
        lax.fori_loop(0, nq, body, 0)


def _attention(q, k, v, lam_params, subln_g, lam_init, tq, tk):
    B, Lq, D = q.shape
    Lk = k.shape[1]
    kv = pl.BlockSpec((None, Lk, LANES), lambda b, h: (b, 0, h))
    qo = pl.BlockSpec((None, Lq, LANES), lambda b, h: (b, 0, h))
    return pl.pallas_call(
        functools.partial(_attn_kernel, tq=tq, tk=tk, lam_init=lam_init),
        grid=(B, N_HEADS),
        in_specs=[pl.BlockSpec((4, HEAD_DIM), lambda b, h: (0, 0)),
                  pl.BlockSpec((1, V_DIM), lambda b, h: (0, 0)),
                  qo, kv, kv],
        out_specs=qo,
        out_shape=jax.ShapeDtypeStruct((B, Lq, D), BF16),
        scratch_shapes=[pltpu.VMEM((2 * tq, LANES), F32),
                        pltpu.VMEM((2 * tq, 2 * V_DIM), F32),
                        pltpu.VMEM((Lk, 2 * V_DIM), BF16),
                        pltpu.VMEM((2 * tq, LANES), BF16)]
                       + [pltpu.VMEM((2 * tq, tk), F32)] * 3 + [pltpu.VMEM((2 * tq, LANES), F32)] * 3,
        compiler_params=_params("parallel", "parallel"),
        name="diff_attention",
    )(lam_params, subln_g, q, k, v)


def _proj_res_kernel(a_ref, w_ref, x_ref, gate_ref, o_ref):
    y = jnp.dot(a_ref[...], w_ref[...], preferred_element_type=F32)
    o_ref[...] = x_ref[...] + gate_ref[...] * y


def _proj_res(a, w, x, gate, rowmap, tm):
    B, L, D = x.shape
    row = pl.BlockSpec((None, tm, D), lambda b, i: (b, i, 0))
    return pl.pallas_call(
        _proj_res_kernel,
        grid=(B, L // tm),
        in_specs=[row, pl.BlockSpec((D, D), lambda b, i: (0, 0)), row,
                  pl.BlockSpec((None, 1, D), lambda b, i: (rowmap(b), 0, 0))],
        out_specs=row,
        out_shape=jax.ShapeDtypeStruct((B, L, D), F32),
        compiler_params=_params("parallel", "parallel"),
        name="attn_out_proj",
    )(a, w, x, gate)


def _ffn_kernel(*refs, fc, final):
    if final:
        x_ref, g_ref, sh_ref, sc_ref, gate_ref, wgu_ref, wd_ref, fg_ref, o_ref = refs
    else:
        x_ref, g_ref, sh_ref, sc_ref, gate_ref, wgu_ref, wd_ref, o_ref = refs
    x = x_ref[...]
    h = _norm_mod(x, g_ref[...], sh_ref[...], sc_ref[...]).astype(BF16)
    acc = jnp.zeros(x.shape, F32)
    for f in range(D_FF // fc):
        gg = jnp.dot(h, wgu_ref[:, f * fc:(f + 1) * fc], preferred_element_type=F32)
        uu = jnp.dot(h, wgu_ref[:, D_FF + f * fc:D_FF + (f + 1) * fc], preferred_element_type=F32)
        a = (gg * jax.nn.sigmoid(gg)) * uu
        acc = acc + jnp.dot(a.astype(BF16), wd_ref[f * fc:(f + 1) * fc, :], preferred_element_type=F32)
    y = x + gate_ref[...] * acc
    if final:
        ms = jnp.mean(y * y, axis=-1, keepdims=True)
        y = y * lax.rsqrt(ms + EPS) * fg_ref[...]
    o_ref[...] = y


def _ffn(x, g, shift, scale, gate, w_gu, w_down, final_g, rowmap, tm):
    B, L, D = x.shape
    final = final_g is not None
    row = pl.BlockSpec((None, tm, D), lambda b, i: (b, i, 0))
    vec = pl.BlockSpec((None, 1, D), lambda b, i: (rowmap(b), 0, 0))
    const = pl.BlockSpec((1, D), lambda b, i: (0, 0))
    in_specs = [row, const, vec, vec, vec,
                pl.BlockSpec((D, 2 * D_FF), lambda b, i: (0, 0)),
                pl.BlockSpec((D_FF, D), lambda b, i: (0, 0))]
    args = [x, g, shift, scale, gate, w_gu, w_down]
    if final:
        in_specs.append(const)
        args.append(final_g)
    return pl.pallas_call(
        functools.partial(_ffn_kernel, fc=256, final=final),
        grid=(B, L // tm),
        in_specs=in_specs,
        out_specs=row,
        out_shape=jax.ShapeDtypeStruct((B, L, D), F32),
        compiler_params=_params("parallel", "parallel"),
        name="ffn",
    )(*args)


def _conv3(u, w, b):
    L = u.shape[0]
    row = lax.broadcasted_iota(jnp.int32, u.shape, 0)
    prev = jnp.where(row == 0, 0.0, pltpu.roll(u, 1, 0))
    nxt = jnp.where(row == L - 1, 0.0, pltpu.roll(u, L - 1, 0))
    return prev * w[0:1] + u * w[1:2] + nxt * w[2:3] + b


def _short_conv_kernel(u0_ref, u1_ref, u2_ref, w0_ref, w1_ref, w2_ref, b0_ref, b1_ref, b2_ref,
                       vx_ref, vxb_ref, x0_ref):
    x0_ref[...] = _conv3(u0_ref[...], w0_ref[...], b0_ref[...])
    vx = _conv3(u2_ref[...], w2_ref[...], b2_ref[...]) * _conv3(u1_ref[...], w1_ref[...], b1_ref[...])
    vx_ref[...] = vx
    vxb_ref[...] = vx.astype(BF16)


def _short_conv(u, conv_w, conv_b, tc):
    B, L, D3 = u.shape
    D = D3 // 3
    nb = D // tc
    specs = []
    for arr_block in ((None, L, tc), (3, tc), (1, tc)):
        for part in range(3):
            if len(arr_block) == 3:
                specs.append(pl.BlockSpec(arr_block, lambda b, c, part=part: (b, 0, part * nb + c)))
            else:
                specs.append(pl.BlockSpec(arr_block, lambda b, c, part=part: (0, part * nb + c)))
    out = pl.BlockSpec((None, L, tc), lambda b, c: (b, 0, c))
    return pl.pallas_call(
        _short_conv_kernel,
        grid=(B, nb),
        in_specs=specs,
        out_specs=[out, out, out],
        out_shape=[jax.ShapeDtypeStruct((B, L, D), F32),
                   jax.ShapeDtypeStruct((B, L, D), BF16),
                   jax.ShapeDtypeStruct((B, L, D), F32)],
        compiler_params=_params("parallel", "parallel"),
        name="hyena_short_conv",
    )(u, u, u, conv_w, conv_w, conv_w, conv_b, conv_b, conv_b)


def _filter_kernel(bands_ref, w1t_ref, w1c_ref, w1s_ref, b1_ref, w2_ref, b2_ref, fr_ref,
                   w3f_ref, w3b_ref, b3f_ref, b3b_ref, delta_ref, h_ref, hid_sc):
    L = hid_sc.shape[0]
    pos = lax.broadcasted_iota(jnp.int32, (L, LANES), 0).astype(F32)
    t = pos / max(L - 1, 1)

    @pl.when(pl.program_id(0) == 0)
    def _():
        ang = (2.0 * math.pi * pos / L) * bands_ref[...]
        fr = fr_ref[...]
        pre = (t * w1t_ref[...]
               + jnp.dot(jnp.cos(ang), w1c_ref[...], preferred_element_type=F32, precision=HI)
               - jnp.dot(jnp.sin(ang), w1s_ref[...], preferred_element_type=F32, precision=HI)
               + b1_ref[...])
        hid = jnp.sin(fr * pre)
        hid_sc[...] = jnp.sin(fr * (jnp.dot(hid, w2_ref[...], preferred_element_type=F32, precision=HI)
                                    + b2_ref[...]))

    hid = hid_sc[...]
    window = jnp.exp(-t[:, 0:1] * delta_ref[...])
    hf = (jnp.dot(hid, w3f_ref[...], preferred_element_type=F32, precision=HI) + b3f_ref[...]) * window
    hb = (jnp.dot(hid, w3b_ref[...], preferred_element_type=F32, precision=HI) + b3b_ref[...]) * window
    norm = (jnp.sum(jnp.abs(hf), axis=0, keepdims=True) + jnp.sum(jnp.abs(hb), axis=0, keepdims=True)
            + FILT_EPS)
    row = lax.broadcasted_iota(jnp.int32, hb.shape, 0)
    h_ref[0] = (hf / norm).astype(h_ref.dtype)
    h_ref[1] = jnp.where(row == 0, 0.0, hb / norm).astype(h_ref.dtype)


def _pad2(a, rows, cols):
    return jnp.pad(a, ((0, rows - a.shape[0]), (0, cols - a.shape[1])))


def _implicit_filter(L, w1, b1, w2, b2, w3, b3, freq, tc):
    D = D_MODEL
    bands = _pad2(jnp.linspace(1e-4, FILT_BANDS - 1, FILT_BANDS, dtype=F32)[None], 1, LANES)
    deltas = jnp.abs(jnp.linspace(math.log(FILT_TARGET) / FILT_SLOW_PCT,
                                  math.log(FILT_TARGET) / FILT_FAST_PCT, D, dtype=F32))[None]
    w1 = w1.astype(F32)
    args = [bands,
            _pad2(w1[0:1], 1, LANES),
            _pad2(w1[1:1 + FILT_BANDS], LANES, LANES),
            _pad2(w1[1 + FILT_BANDS:], LANES, LANES),
            _pad2(b1[None], 1, LANES), _pad2(w2, LANES, LANES), _pad2(b2[None], 1, LANES),
            _pad2(freq[None], 1, LANES),
            _pad2(w3, LANES, 2 * D), _pad2(w3, LANES, 2 * D), b3[None], b3[None], deltas]
    small = lambda shape: pl.BlockSpec(shape, lambda c: (0, 0))
    nb = D // tc
    in_specs = [small((1, LANES)), small((1, LANES)), small((LANES, LANES)), small((LANES, LANES)),
                small((1, LANES)), small((LANES, LANES)), small((1, LANES)), small((1, LANES)),
                pl.BlockSpec((LANES, tc), lambda c: (0, c)),
                pl.BlockSpec((LANES, tc), lambda c: (0, nb + c)),
                pl.BlockSpec((1, tc), lambda c: (0, c)),
                pl.BlockSpec((1, tc), lambda c: (0, nb + c)),
                pl.BlockSpec((1, tc), lambda c: (0, c))]
    return pl.pallas_call(
        _filter_kernel,
        grid=(nb,),
        in_specs=in_specs,
        out_specs=pl.BlockSpec((2, L, tc), lambda c: (0, 0, c)),
        out_shape=jax.ShapeDtypeStruct((2, L, D), BF16),
        scratch_shapes=[pltpu.VMEM((L, LANES), F32)],
        compiler_params=_params("arbitrary"),
        name="hyena_filter",
    )(*args)


def _dft_gen_kernel(fre_ref, fim_ref, gim_ref, cb_sc, sb_sc, *, L):
    n = 2 * L
    tr = fre_ref.shape[0]
    step = 2.0 * math.pi / n
    i = pl.program_id(0)
    row = lax.broadcasted_iota(jnp.int32, (tr, L), 0)
    col = lax.broadcasted_iota(jnp.int32, (tr, L), 1)

    @pl.when(i == 0)
    def _():
        ang = ((row * col) & (n - 1)).astype(F32) * step
        cb_sc[...] = jnp.cos(ang)
        sb_sc[...] = jnp.sin(ang)

    col1 = lax.broadcasted_iota(jnp.int32, (1, L), 1)
    a = (((i * tr) * col1) & (n - 1)).astype(F32) * step
    ca, sa = jnp.cos(a), jnp.sin(a)
    cb, sb = cb_sc[...], sb_sc[...]
    c = cb * ca - sb * sa
    s = -(sb * ca + cb * sa)
    grow = row + i * tr
    fre_ref[...] = c.astype(fre_ref.dtype)
    fim_ref[...] = jnp.where(grow == 0, jnp.where((col & 1) == 0, 1.0, -1.0), s).astype(fim_ref.dtype)
    gim_ref[...] = jnp.where(col == 0, jnp.where((grow & 1) == 0, 1.0, -1.0), s).astype(gim_ref.dtype)


def _dft_tables(L, tr):
    spec = pl.BlockSpec((tr, L), lambda i: (i, 0))
    shape = jax.ShapeDtypeStruct((L, L), BF16)
    return pl.pallas_call(
        functools.partial(_dft_gen_kernel, L=L),
        grid=(L // tr,),
        out_specs=[spec, spec, spec],
        out_shape=[shape, shape, shape],
        scratch_shapes=[pltpu.VMEM((tr, L), F32), pltpu.VMEM((tr, L), F32)],
        compiler_params=_params("arbitrary"),
        name="dft_tables",
    )()


def _fwd_dft_kernel(*refs, with_filter):
    if with_filter:
        fre_ref, fim_ref, v_ref, kr_ref, ki_ref, y_ref = refs
    else:
        fre_ref, fim_ref, v_ref, y_ref = refs
    v = v_ref[...]
    vr = jnp.dot(fre_ref[...], v, preferred_element_type=F32)
    vi = jnp.dot(fim_ref[...], v, preferred_element_type=F32)
    if with_filter:
        kr, ki = kr_ref[...], ki_ref[...]
        tm = vr.shape[0]
        first = (lax.broadcasted_iota(jnp.int32, vr.shape, 0) + pl.program_id(1) * tm) == 0
        yr = vr * kr - jnp.where(first, 0.0, vi * ki)
        yi = jnp.where(first, vi * ki, vr * ki + vi * kr)
        y_ref[0] = yr.astype(y_ref.dtype)
        y_ref[1] = yi.astype(y_ref.dtype)
    else:
        y_ref[0] = vr.astype(y_ref.dtype)
        y_ref[1] = vi.astype(y_ref.dtype)


def _fwd_dft(fre, fim, v, kf, tm, out_dtype):
    B, L, D = v.shape
    ftile = pl.BlockSpec((tm, L), lambda b, m: (m, 0))
    in_specs = [ftile, ftile, pl.BlockSpec((None, L, D), lambda b, m: (b, 0, 0))]
    args = [fre, fim, v]
    if kf is not None:
        in_specs += [pl.BlockSpec((None, tm, D), lambda b, m: (0, m, 0)),
                     pl.BlockSpec((None, tm, D), lambda b, m: (1, m, 0))]
        args += [kf, kf]
    return pl.pallas_call(
        functools.partial(_fwd_dft_kernel, with_filter=kf is not None),
        grid=(B, L // tm),
        in_specs=in_specs,
        out_specs=pl.BlockSpec((None, 2, tm, D), lambda b, m: (b, 0, m, 0)),
        out_shape=jax.ShapeDtypeStruct((B, 2, L, D), out_dtype),
        compiler_params=_params("parallel", "parallel"),
        name="hyena_fwd_dft",
    )(*args)


def _filter_spectrum_kernel(hf_ref, hb_ref, k_ref, *, n):
    hfr, hfi = hf_ref[0], hf_ref[1]
    hbr, hbi = hb_ref[0], hb_ref[1]
    first = (lax.broadcasted_iota(jnp.int32, hfr.shape, 0) + pl.program_id(0) * hfr.shape[0]) == 0
    wgt = jnp.where(first, 1.0 / n, 2.0 / n)
    k_ref[0] = (hfr + hbr) * wgt
    k_ref[1] = jnp.where(first, hfi + hbi, hfi - hbi) * wgt


def _filter_spectrum(hspec, tm):
    _, _, L, D = hspec.shape
    return pl.pallas_call(
        functools.partial(_filter_spectrum_kernel, n=2 * L),
        grid=(L // tm,),
        in_specs=[pl.BlockSpec((None, 2, tm, D), lambda m: (0, 0, m, 0)),
                  pl.BlockSpec((None, 2, tm, D), lambda m: (1, 0, m, 0))],
        out_specs=pl.BlockSpec((2, tm, D), lambda m: (0, m, 0)),
        out_shape=jax.ShapeDtypeStruct((2, L, D), F32),
        compiler_params=_params("parallel"),
        name="hyena_filter_spectrum",
    )(hspec, hspec)


def _inv_dft_kernel(gre_ref, gim_ref, yr_ref, yi_ref, vx_ref, x0_ref, skip_ref, w_ref, b_ref,
                    x_ref, gate_ref, o_ref, acc_sc):
    kk = pl.program_id(2)

    @pl.when(kk == 0)
    def _():
        acc_sc[...] = jnp.zeros(acc_sc.shape, F32)

    acc_sc[...] += (jnp.dot(gre_ref[...], yr_ref[...], preferred_element_type=F32)
                    + jnp.dot(gim_ref[...], yi_ref[...], preferred_element_type=F32))

    @pl.when(kk == pl.num_programs(2) - 1)
    def _():
        z = (acc_sc[...] + vx_ref[...] * skip_ref[...]) * x0_ref[...]
        out = jnp.dot(z.astype(BF16), w_ref[...], preferred_element_type=F32) + b_ref[...]
        o_ref[...] = x_ref[...] + gate_ref[...] * out


def _inv_dft(gre, gim, y, vx, x0, skip, w_out, b_out, x, gate, rowmap, tm, tk):
    B, L, D = x.shape
    gtile = pl.BlockSpec((tm, tk), lambda b, i, k: (i, k))
    row = pl.BlockSpec((None, tm, D), lambda b, i, k: (b, i, 0))
    const = pl.BlockSpec((1, D), lambda b, i, k: (0, 0))
    return pl.pallas_call(
        _inv_dft_kernel,
        grid=(B, L // tm, L // tk),
        in_specs=[gtile, gtile,
                  pl.BlockSpec((None, None, tk, D), lambda b, i, k: (b, 0, k, 0)),
                  pl.BlockSpec((None, None, tk, D), lambda b, i, k: (b, 1, k, 0)),
                  row, row, const,
                  pl.BlockSpec((D, D), lambda b, i, k: (0, 0)), const,
                  row, pl.BlockSpec((None, 1, D), lambda b, i, k: (rowmap(b), 0, 0))],
        out_specs=row,
        out_shape=jax.ShapeDtypeStruct((B, L, D), F32),
        scratch_shapes=[pltpu.VMEM((tm, D), F32)],
        compiler_params=_params("parallel", "parallel", "arbitrary"),
        name="hyena_inv_dft",
    )(gre, gim, y, y, vx, x0, skip, w_out, b_out, x, gate)


def _rope_tables(L):
    pos = jnp.arange(L, dtype=jnp.int32)
    rows = (pos // GRID_W).astype(F32)
    cols = (pos % GRID_W).astype(F32)
    inv = ROPE_BASE ** (-jnp.arange(ROPE_PAIRS, dtype=F32) / ROPE_PAIRS)
    lane = jnp.arange(LANES)
    within = lane % HEAD_DIM
    axis = within // (2 * ROPE_PAIRS)
    half = (within % (2 * ROPE_PAIRS)) // ROPE_PAIRS
    ang = jnp.where(axis[None, :] == 0, rows[:, None], cols[:, None]) * inv[within % ROPE_PAIRS][None, :]
    cos, sin = jnp.cos(ang), jnp.sin(ang)
    sin_a = jnp.where(half[None, :] == 0, -sin, 0.0)
    sin_b = jnp.where(half[None, :] == 1, sin, 0.0)
    return cos, sin_a, sin_b


def _tile(L, pref):
    return min(L, pref)


def _run_group(x, rowmap, mods, p, cache, rope_tabs, dft, kv_dtype):
    B, L, D = x.shape
    tm = _tile(L, 512)
    ctx_k, ctx_v = [], []
    for i in range(DEPTH):
        j = i // 2
        sh1, sc1, g1, sh2, sc2, g2 = (mods[i, :, s * D:(s + 1) * D].reshape(MOD_ROWS, 1, D) for s in range(6))
        n1 = p["norm1_g"][i][None]
        if i % 2 == 0:
            lam_init = 0.8 - 0.6 * math.exp(-0.3 * i)
            q, k, v = _qkv(x, n1, sh1, sc1, p["attn_w_qkv"][j], rowmap, rope_tabs, kv_dtype, tm)
            ctx_k.append(k)
            ctx_v.append(v)
            if cache is not None:
                ck, cv = cache
                k = jnp.concatenate([ck[:, j].reshape(B, -1, D).astype(k.dtype), k], axis=1)
                v = jnp.concatenate([cv[:, j].reshape(B, -1, D).astype(v.dtype), v], axis=1)
            o = _attention(q, k, v, p["attn_lambda"][j].astype(F32), p["attn_subln_g"][j][None],
                           lam_init, _tile(L, 512), _tile(k.shape[1], 512))
            x = _proj_res(o, p["attn_w_o"][j], x, g1, rowmap, tm)
        else:
            fre, fim, gim = dft
            u = _in_proj(x, n1, sh1, sc1, p["hy_w_in"][j], p["hy_b_in"][j][None], rowmap, tm)
            vx, vxb, x0 = _short_conv(u, p["hy_conv_w"][j], p["hy_conv_b"][j][None], LANES)
            filt = _implicit_filter(L, p["filt_w1"][j], p["filt_b1"][j], p["filt_w2"][j], p["filt_b2"][j],
                                    p["filt_w3"][j], p["filt_b3"][j], p["filt_freq"][j], 256)
            tmf = _tile(L, 256)
            kf = _filter_spectrum(_fwd_dft(fre, fim, filt, None, tmf, F32), tmf)
            y = _fwd_dft(fre, fim, vxb, kf, tmf, BF16)
            x = _inv_dft(fre, gim, y, vx, x0, p["hy_skip"][j][None], p["hy_w_out"][j],
                         p["hy_b_out"][j][None], x, g1, rowmap, tm, _tile(L, 512))
        final_g = p["final_g"][None] if i == DEPTH - 1 else None
        x = _ffn(x, p["norm2_g"][i][None], sh2, sc2, g2, p["ffn_w_gu"][i], p["ffn_w_down"][i],
                 final_g, rowmap, tm)
    return x, ctx_k, ctx_v


def kernel(x_prompt, x_sample, cache_k, cache_v, c, c_ctx, ada_w, ada_b, norm1_g, norm2_g, attn_w_qkv, attn_lambda, attn_subln_g, attn_w_o, hy_w_in, hy_b_in, hy_conv_w, hy_conv_b, filt_w1, filt_b1, filt_w2, filt_b2, filt_w3, filt_b3, filt_freq, hy_skip, hy_w_out, hy_b_out, ffn_w_gu, ffn_w_down, final_g):
    D = D_MODEL
    nb = c.shape[0]
    cmat = jnp.concatenate([c, c_ctx[None], jnp.zeros((MOD_ROWS - nb - 1, D), F32)], axis=0)
    mods = _adaln_all(cmat, ada_w, ada_b)

    p = dict(norm1_g=norm1_g, norm2_g=norm2_g, attn_lambda=attn_lambda, attn_subln_g=attn_subln_g,
             hy_b_in=hy_b_in, hy_conv_w=hy_conv_w, hy_conv_b=hy_conv_b,
             filt_w1=filt_w1, filt_b1=filt_b1, filt_w2=filt_w2, filt_b2=filt_b2, filt_w3=filt_w3,
             filt_b3=filt_b3, filt_freq=filt_freq, hy_skip=hy_skip, hy_b_out=hy_b_out, final_g=final_g,
             attn_w_qkv=attn_w_qkv.astype(BF16), attn_w_o=attn_w_o.astype(BF16),
             hy_w_in=hy_w_in.astype(BF16), hy_w_out=hy_w_out.astype(BF16),
             ffn_w_gu=ffn_w_gu.astype(BF16), ffn_w_down=ffn_w_down.astype(BF16))

    Lp, Ls = x_prompt.shape[1], x_sample.shape[1]
    y_prompt, ctx_k, ctx_v = _run_group(x_prompt, lambda b: CTX_ROW, mods, p, None, None,
                                        _dft_tables(Lp, min(Lp, 256)), F32)
    y_sample, _, _ = _run_group(x_sample, lambda b: b, mods, p, (cache_k, cache_v), _rope_tables(Ls),
                                _dft_tables(Ls, 256), BF16)

    Bp = x_prompt.shape[0]
    new_k = jnp.stack(ctx_k, axis=1).reshape(Bp, len(ctx_k), Lp, N_HEADS, 2, HEAD_DIM)
    new_v = jnp.stack(ctx_v, axis=1).reshape(Bp, len(ctx_v), Lp, N_HEADS, V_DIM)
    return y_prompt, y_sample, new_k, new_v
```

```python
import functools
import math

import jax
import jax.numpy as jnp
from jax import lax
from jax.experimental import pallas as pl
from jax.experimental.pallas import tpu as pltpu

D_MODEL = 1024
DEPTH = 4
GRID_W = 64
N_HEADS = 8
HEAD_DIM = 64
V_DIM = 2 * HEAD_DIM
ROPE_PAIRS = HEAD_DIM // 4
ROPE_BASE = 10000.0
FILT_BANDS = 16
FILT_ORDER = 64
FILT_TARGET = 1e-2
FILT_FAST_PCT = 0.3
FILT_SLOW_PCT = 1.5
FILT_EPS = 1e-6
D_FF = 2816
EPS = 1e-6
SUBLN_EPS = 1e-5

LANES = 128
MOD_ROWS = 16
CTX_ROW = 8
VMEM_LIMIT = 56 * 1024 * 1024

F32 = jnp.float32
BF16 = jnp.bfloat16
HI = lax.Precision.HIGHEST


def _params(*sem):
    return pltpu.CompilerParams(dimension_semantics=sem, vmem_limit_bytes=VMEM_LIMIT)


def _lane_tile(x, reps):
    return x if reps == 1 else jnp.concatenate([x] * reps, axis=1)


def _norm_mod(x, g, shift, scale):
    ms = jnp.mean(x * x, axis=-1, keepdims=True)
    return (x * lax.rsqrt(ms + EPS) * g) * (1.0 + scale) + shift


def _mod_kernel(c_ref, w_ref, b_ref, o_ref):
    c = c_ref[...]
    a = c * jax.nn.sigmoid(c)
    o_ref[...] = jnp.dot(a, w_ref[...], preferred_element_type=F32, precision=HI) + b_ref[...]


def _adaln_all(cmat, ada_w, ada_b):
    n6 = 6 * D_MODEL
    tn = 1536
    return pl.pallas_call(
        _mod_kernel,
        grid=(DEPTH, n6 // tn),
        in_specs=[
            pl.BlockSpec((MOD_ROWS, D_MODEL), lambda i, n: (0, 0)),
            pl.BlockSpec((None, D_MODEL, tn), lambda i, n: (i, 0, n)),
            pl.BlockSpec((None, 1, tn), lambda i, n: (i, 0, n)),
        ],
        out_specs=pl.BlockSpec((None, MOD_ROWS, tn), lambda i, n: (i, 0, n)),
        out_shape=jax.ShapeDtypeStruct((DEPTH, MOD_ROWS, n6), F32),
        compiler_params=_params("parallel", "parallel"),
        name="adaln",
    )(cmat, ada_w, ada_b.reshape(DEPTH, 1, n6))


def _rope_chunk(x, cos, sin_a, sin_b):
    return x * cos + pltpu.roll(x, LANES - ROPE_PAIRS, 1) * sin_a + pltpu.roll(x, ROPE_PAIRS, 1) * sin_b


def _qkv_kernel(*refs, rope, tn):
    if rope:
        x_ref, g_ref, sh_ref, sc_ref, w_ref, cos_ref, sa_ref, sb_ref, q_ref, k_ref, v_ref = refs
    else:
        x_ref, g_ref, sh_ref, sc_ref, w_ref, q_ref, k_ref, v_ref = refs
    h = _norm_mod(x_ref[...], g_ref[...], sh_ref[...], sc_ref[...]).astype(BF16)
    outs = (q_ref, k_ref, v_ref)
    qk_scale = HEAD_DIM ** -0.5 * math.log2(math.e)
    for part in range(3):
        for n in range(D_MODEL // tn):
            col = part * D_MODEL + n * tn
            y = jnp.dot(h, w_ref[:, col:col + tn], preferred_element_type=F32)
            for j in range(tn // LANES):
                yj = y[:, j * LANES:(j + 1) * LANES]
                if rope and part < 2:
                    yj = _rope_chunk(yj, cos_ref[...], sa_ref[...], sb_ref[...])
                if part == 0:
                    yj = yj * qk_scale
                lo = n * tn + j * LANES
                outs[part][:, lo:lo + LANES] = yj.astype(outs[part].dtype)


def _qkv(x, g, shift, scale, w, rowmap, rope_tabs, kv_dtype, tm):
    B, L, D = x.shape
    rope = rope_tabs is not None
    row = pl.BlockSpec((None, tm, D), lambda b, i: (b, i, 0))
    vec = pl.BlockSpec((None, 1, D), lambda b, i: (rowmap(b), 0, 0))
    in_specs = [row, pl.BlockSpec((1, D), lambda b, i: (0, 0)), vec, vec,
                pl.BlockSpec((D, 3 * D), lambda b, i: (0, 0))]
    args = [x, g, shift, scale, w]
    if rope:
        tab = pl.BlockSpec((tm, LANES), lambda b, i: (i, 0))
        in_specs += [tab, tab, tab]
        args += list(rope_tabs)
    return pl.pallas_call(
        functools.partial(_qkv_kernel, rope=rope, tn=512),
        grid=(B, L // tm),
        in_specs=in_specs,
        out_specs=[row, row, row],
        out_shape=[jax.ShapeDtypeStruct((B, L, D), BF16),
                   jax.ShapeDtypeStruct((B, L, D), kv_dtype),
                   jax.ShapeDtypeStruct((B, L, D), kv_dtype)],
        compiler_params=_params("parallel", "parallel"),
        name="qkv",
    )(*args)


def _in_proj_kernel(x_ref, g_ref, sh_ref, sc_ref, w_ref, b_ref, o_ref, *, tn):
    h = _norm_mod(x_ref[...], g_ref[...], sh_ref[...], sc_ref[...]).astype(BF16)
    for n in range(o_ref.shape[-1] // tn):
        sl = slice(n * tn, (n + 1) * tn)
        o_ref[:, sl] = jnp.dot(h, w_ref[:, sl], preferred_element_type=F32) + b_ref[:, sl]


def _in_proj(x, g, shift, scale, w, b, rowmap, tm):
    B, L, D = x.shape
    N = w.shape[1]
    vec = pl.BlockSpec((None, 1, D), lambda b_, i: (rowmap(b_), 0, 0))
    return pl.pallas_call(
        functools.partial(_in_proj_kernel, tn=512),
        grid=(B, L // tm),
        in_specs=[pl.BlockSpec((None, tm, D), lambda b_, i: (b_, i, 0)),
                  pl.BlockSpec((1, D), lambda b_, i: (0, 0)), vec, vec,
                  pl.BlockSpec((D, N), lambda b_, i: (0, 0)),
                  pl.BlockSpec((1, N), lambda b_, i: (0, 0))],
        out_specs=pl.BlockSpec((None, tm, N), lambda b_, i: (b_, i, 0)),
        out_shape=jax.ShapeDtypeStruct((B, L, N), F32),
        compiler_params=_params("parallel", "parallel"),
        name="hyena_in_proj",
    )(x, g, shift, scale, w, b)


def _attn_kernel(*refs, tq, tk, lam_init, cached):
    if cached:
        lam_ref, g_ref, q_ref, ck_ref, cv_ref, k_ref, v_ref, o_ref = refs[:8]
    else:
        lam_ref, g_ref, q_ref, k_ref, v_ref, o_ref = refs[:6]
    m_sc, acc_sc, vx_sc, q2_sc, s0_sc, s1_sc, s2_sc, t0_sc, t1_sc, t2_sc = refs[-10:]
    nc = ck_ref.shape[0] // tk if cached else 0
    nk = nc + k_ref.shape[0] // tk
    nq = q_ref.shape[0] // tq
    if cached:
        vx_sc[:nc * tk, :V_DIM] = cv_ref[...].astype(BF16)
    vx_sc[nc * tk:, :V_DIM] = v_ref[...].astype(BF16)
    vx_sc[:, V_DIM:] = jnp.ones((vx_sc.shape[0], V_DIM), BF16)
    lp = lam_ref[...]
    lam = (jnp.exp(jnp.sum(lp[0:1] * lp[1:2], axis=1, keepdims=True))
           - jnp.exp(jnp.sum(lp[2:3] * lp[3:4], axis=1, keepdims=True)) + lam_init)
    lane = lax.broadcasted_iota(jnp.int32, (tq, LANES), 1)

    def slot(j):
        if j == 0:
            return s0_sc, t0_sc
        return (s1_sc, t1_sc) if j % 2 == 1 else (s2_sc, t2_sc)

    def key_tile(j):
        if j < nc:
            return ck_ref[j * tk:(j + 1) * tk, :].astype(BF16)
        return k_ref[(j - nc) * tk:(j - nc + 1) * tk, :].astype(BF16)

    def load_q(i):
        q = q_ref[pl.ds(pl.multiple_of(i * tq, tq), tq), :]
        zero = jnp.zeros_like(q)
        q2_sc[:tq] = jnp.where(lane < HEAD_DIM, q, zero)
        q2_sc[tq:] = jnp.where(lane >= HEAD_DIM, q, zero)

    def scores(j):
        s_sc, t_sc = slot(j)
        s = lax.dot_general(q2_sc[...], key_tile(j), (((1,), (1,)), ((), ())), preferred_element_type=F32)
        s_sc[...] = s
        t_sc[...] = jnp.broadcast_to(jnp.max(s, axis=1, keepdims=True), t_sc.shape)

    def accumulate(j):
        s_sc, t_sc = slot(j)
        m_prev = m_sc[...]
        m_new = jnp.maximum(m_prev, t_sc[...])
        alpha = jnp.exp2(m_prev - m_new)
        e = jnp.exp2(s_sc[...] - _lane_tile(m_new, tk // LANES))
        acc_sc[...] = (_lane_tile(alpha, 2) * acc_sc[...]
                       + jnp.dot(e.astype(BF16), vx_sc[j * tk:(j + 1) * tk, :], preferred_element_type=F32))
        m_sc[...] = m_new

    def init():
        m_sc[...] = jnp.full(m_sc.shape, -jnp.inf, F32)
        acc_sc[...] = jnp.zeros(acc_sc.shape, F32)

    def finalize(i):
        o = (acc_sc[:tq, :V_DIM] / acc_sc[:tq, V_DIM:] - lam * (acc_sc[tq:, :V_DIM] / acc_sc[tq:, V_DIM:]))
        ms = jnp.mean(o * o, axis=-1, keepdims=True)
        y = (o * lax.rsqrt(ms + SUBLN_EPS) * g_ref[...]) * (1.0 - lam_init)
        o_ref[pl.ds(pl.multiple_of(i * tq, tq), tq), :] = y.astype(o_ref.dtype)

    def query_tile(i, prefetch):
        for j in range(1, nk):
            accumulate(j - 1)
            scores(j)
        if prefetch:
            load_q(jnp.minimum(i + 1, nq - 1))
        accumulate(nk - 1)
        if prefetch:
            scores(0)
        finalize(i)
        init()

    init()
    load_q(0)
    scores(0)
    if nq == 1:
        query_tile(0, False)
    else:
        def body(i, carry):
            query_tile(i, True)
            return carry
        lax.fori_loop(0, nq, body, 0)


def _attention(q, k, v, cache, lam_params, subln_g, lam_init, tq, tk):
    B, Lq, D = q.shape
    Lk = k.shape[1]
    kv = pl.BlockSpec((None, Lk, LANES), lambda b, h: (b, 0, h))
    qo = pl.BlockSpec((None, Lq, LANES), lambda b, h: (b, 0, h))
    in_specs = [pl.BlockSpec((4, HEAD_DIM), lambda b, h: (0, 0)),
                pl.BlockSpec((1, V_DIM), lambda b, h: (0, 0)), qo]
    args = [lam_params, subln_g, q]
    past = 0
    if cache is not None:
        ck, cv, layer = cache
        past = ck.shape[2]
        cspec = pl.BlockSpec((None, None, past, LANES), lambda b, h: (b, layer, 0, h))
        in_specs += [cspec, cspec]
        args += [ck, cv]
    in_specs += [kv, kv]
    args += [k, v]
    return pl.pallas_call(
        functools.partial(_attn_kernel, tq=tq, tk=tk, lam_init=lam_init, cached=cache is not None),
        grid=(B, N_HEADS),
        in_specs=in_specs,
        out_specs=qo,
        out_shape=jax.ShapeDtypeStruct((B, Lq, D), BF16),
        scratch_shapes=[pltpu.VMEM((2 * tq, LANES), F32),
                        pltpu.VMEM((2 * tq, 2 * V_DIM), F32),
                        pltpu.VMEM((past + Lk, 2 * V_DIM), BF16),
                        pltpu.VMEM((2 * tq, LANES), BF16)]
                       + [pltpu.VMEM((2 * tq, tk), F32)] * 3 + [pltpu.VMEM((2 * tq, LANES), F32)] * 3,
        compiler_params=_params("parallel", "parallel"),
        name="diff_attention",
    )(*args)


def _proj_res_kernel(*refs, bias):
    if bias:
        a_ref, w_ref, b_ref, x_ref, gate_ref, o_ref = refs
    else:
        a_ref, w_ref, x_ref, gate_ref, o_ref = refs
    y = jnp.dot(a_ref[...], w_ref[...], preferred_element_type=F32)
    if bias:
        y = y + b_ref[...]
    o_ref[...] = x_ref[...] + gate_ref[...] * y


def _proj_res(a, w, b, x, gate, rowmap, tm):
    B, L, D = x.shape
    row = pl.BlockSpec((None, tm, D), lambda b_, i: (b_, i, 0))
    in_specs = [row, pl.BlockSpec((D, D), lambda b_, i: (0, 0))]
    args = [a, w]
    if b is not None:
        in_specs.append(pl.BlockSpec((1, D), lambda b_, i: (0, 0)))
        args.append(b)
    in_specs += [row, pl.BlockSpec((None, 1, D), lambda b_, i: (rowmap(b_), 0, 0))]
    args += [x, gate]
    return pl.pallas_call(
        functools.partial(_proj_res_kernel, bias=b is not None),
        grid=(B, L // tm),
        in_specs=in_specs,
        out_specs=row,
        out_shape=jax.ShapeDtypeStruct((B, L, D), F32),
        compiler_params=_params("parallel", "parallel"),
        name="out_proj",
    )(*args)


def _ffn_kernel(*refs, fc, final):
    if final:
        x_ref, g_ref, sh_ref, sc_ref, gate_ref, wgu_ref, wd_ref, fg_ref, o_ref = refs
    else:
        x_ref, g_ref, sh_ref, sc_ref, gate_ref, wgu_ref, wd_ref, o_ref = refs
    x = x_ref[...]
    h = _norm_mod(x, g_ref[...], sh_ref[...], sc_ref[...]).astype(BF16)
    acc = jnp.zeros(x.shape, F32)
    for f in range(D_FF // fc):
        gg = jnp.dot(h, wgu_ref[:, f * fc:(f + 1) * fc], preferred_element_type=F32)
        uu = jnp.dot(h, wgu_ref[:, D_FF + f * fc:D_FF + (f + 1) * fc], preferred_element_type=F32)
        a = (gg * jax.nn.sigmoid(gg)) * uu
        acc = acc + jnp.dot(a.astype(BF16), wd_ref[f * fc:(f + 1) * fc, :], preferred_element_type=F32)
    y = x + gate_ref[...] * acc
    if final:
        ms = jnp.mean(y * y, axis=-1, keepdims=True)
        y = y * lax.rsqrt(ms + EPS) * fg_ref[...]
    o_ref[...] = y


def _ffn(x, g, shift, scale, gate, w_gu, w_down, final_g, rowmap, tm):
    B, L, D = x.shape
    final = final_g is not None
    row = pl.BlockSpec((None, tm, D), lambda b, i: (b, i, 0))
    vec = pl.BlockSpec((None, 1, D), lambda b, i: (rowmap(b), 0, 0))
    const = pl.BlockSpec((1, D), lambda b, i: (0, 0))
    in_specs = [row, const, vec, vec, vec,
                pl.BlockSpec((D, 2 * D_FF), lambda b, i: (0, 0)),
                pl.BlockSpec((D_FF, D), lambda b, i: (0, 0))]
    args = [x, g, shift, scale, gate, w_gu, w_down]
    if final:
        in_specs.append(const)
        args.append(final_g)
    return pl.pallas_call(
        functools.partial(_ffn_kernel, fc=256, final=final),
        grid=(B, L // tm),
        in_specs=in_specs,
        out_specs=row,
        out_shape=jax.ShapeDtypeStruct((B, L, D), F32),
        compiler_params=_params("parallel", "parallel"),
        name="ffn",
    )(*args)


def _conv3(u, w, b):
    L = u.shape[0]
    row = lax.broadcasted_iota(jnp.int32, u.shape, 0)
    prev = jnp.where(row == 0, 0.0, pltpu.roll(u, 1, 0))
    nxt = jnp.where(row == L - 1, 0.0, pltpu.roll(u, L - 1, 0))
    return prev * w[0:1] + u * w[1:2] + nxt * w[2:3] + b


def _short_conv_kernel(u0_ref, u1_ref, u2_ref, w0_ref, w1_ref, w2_ref, b0_ref, b1_ref, b2_ref,
                       vx_ref, x0_ref):
    x0_ref[...] = _conv3(u0_ref[...], w0_ref[...], b0_ref[...])
    vx = _conv3(u2_ref[...], w2_ref[...], b2_ref[...]) * _conv3(u1_ref[...], w1_ref[...], b1_ref[...])
    vx_ref[...] = vx.astype(vx_ref.dtype)


def _short_conv(u, conv_w, conv_b, tc):
    B, L, D3 = u.shape
    D = D3 // 3
    nb = D // tc
    specs = []
    for arr_block in ((None, L, tc), (3, tc), (1, tc)):
        for part in range(3):
            if len(arr_block) == 3:
                specs.append(pl.BlockSpec(arr_block, lambda b, c, part=part: (b, 0, part * nb + c)))
            else:
                specs.append(pl.BlockSpec(arr_block, lambda b, c, part=part: (0, part * nb + c)))
    out = pl.BlockSpec((None, L, tc), lambda b, c: (b, 0, c))
    return pl.pallas_call(
        _short_conv_kernel,
        grid=(B, nb),
        in_specs=specs,
        out_specs=[out, out],
        out_shape=[jax.ShapeDtypeStruct((B, L, D), BF16),
                   jax.ShapeDtypeStruct((B, L, D), F32)],
        compiler_params=_params("parallel", "parallel"),
        name="hyena_short_conv",
    )(u, u, u, conv_w, conv_w, conv_w, conv_b, conv_b, conv_b)


def _filter_kernel(bands_ref, w1t_ref, w1c_ref, w1s_ref, b1_ref, w2_ref, b2_ref, fr_ref,
                   w3f_ref, w3b_ref, b3f_ref, b3b_ref, delta_ref, h_ref, hid_sc):
    L = hid_sc.shape[0]
    pos = lax.broadcasted_iota(jnp.int32, (L, LANES), 0).astype(F32)
    t = pos / max(L - 1, 1)

    @pl.when(pl.program_id(0) == 0)
    def _():
        ang = (2.0 * math.pi * pos / L) * bands_ref[...]
        fr = fr_ref[...]
        pre = (t * w1t_ref[...]
               + jnp.dot(jnp.cos(ang), w1c_ref[...], preferred_element_type=F32, precision=HI)
               - jnp.dot(jnp.sin(ang), w1s_ref[...], preferred_element_type=F32, precision=HI)
               + b1_ref[...])
        hid = jnp.sin(fr * pre)
        hid_sc[...] = jnp.sin(fr * (jnp.dot(hid, w2_ref[...], preferred_element_type=F32, precision=HI)
                                    + b2_ref[...]))

    hid = hid_sc[...]
    window = jnp.exp(-t[:, 0:1] * delta_ref[...])
    hf = (jnp.dot(hid, w3f_ref[...], preferred_element_type=F32, precision=HI) + b3f_ref[...]) * window
    hb = (jnp.dot(hid, w3b_ref[...], preferred_element_type=F32, precision=HI) + b3b_ref[...]) * window
    norm = (jnp.sum(jnp.abs(hf), axis=0, keepdims=True) + jnp.sum(jnp.abs(hb), axis=0, keepdims=True)
            + FILT_EPS)
    row = lax.broadcasted_iota(jnp.int32, hb.shape, 0)
    h_ref[0] = (hf / norm).astype(h_ref.dtype)
    h_ref[1] = jnp.where(row == 0, 0.0, hb / norm).astype(h_ref.dtype)


def _pad2(a, rows, cols):
    return jnp.pad(a, ((0, rows - a.shape[0]), (0, cols - a.shape[1])))


def _implicit_filter(L, w1, b1, w2, b2, w3, b3, freq, tc):
    D = D_MODEL
    bands = _pad2(jnp.linspace(1e-4, FILT_BANDS - 1, FILT_BANDS, dtype=F32)[None], 1, LANES)
    deltas = jnp.abs(jnp.linspace(math.log(FILT_TARGET) / FILT_SLOW_PCT,
                                  math.log(FILT_TARGET) / FILT_FAST_PCT, D, dtype=F32))[None]
    w1 = w1.astype(F32)
    args = [bands,
            _pad2(w1[0:1], 1, LANES),
            _pad2(w1[1:1 + FILT_BANDS], LANES, LANES),
            _pad2(w1[1 + FILT_BANDS:], LANES, LANES),
            _pad2(b1[None], 1, LANES), _pad2(w2, LANES, LANES), _pad2(b2[None], 1, LANES),
            _pad2(freq[None], 1, LANES),
            _pad2(w3, LANES, 2 * D), _pad2(w3, LANES, 2 * D), b3[None], b3[None], deltas]
    small = lambda shape: pl.BlockSpec(shape, lambda c: (0, 0))
    nb = D // tc
    in_specs = [small((1, LANES)), small((1, LANES)), small((LANES, LANES)), small((LANES, LANES)),
                small((1, LANES)), small((LANES, LANES)), small((1, LANES)), small((1, LANES)),
                pl.BlockSpec((LANES, tc), lambda c: (0, c)),
                pl.BlockSpec((LANES, tc), lambda c: (0, nb + c)),
                pl.BlockSpec((1, tc), lambda c: (0, c)),
                pl.BlockSpec((1, tc), lambda c: (0, nb + c)),
                pl.BlockSpec((1, tc), lambda c: (0, c))]
    return pl.pallas_call(
        _filter_kernel,
        grid=(nb,),
        in_specs=in_specs,
        out_specs=pl.BlockSpec((2, L, tc), lambda c: (0, 0, c)),
        out_shape=jax.ShapeDtypeStruct((2, L, D), BF16),
        scratch_shapes=[pltpu.VMEM((L, LANES), F32)],
        compiler_params=_params("arbitrary"),
        name="hyena_filter",
    )(*args)


def _dft_gen_kernel(fre_ref, fim_ref, gim_ref, cb_sc, sb_sc, *, L):
    n = 2 * L
    tr = fre_ref.shape[0]
    step = 2.0 * math.pi / n
    i = pl.program_id(0)
    row = lax.broadcasted_iota(jnp.int32, (tr, L), 0)
    col = lax.broadcasted_iota(jnp.int32, (tr, L), 1)

    @pl.when(i == 0)
    def _():
        ang = ((row * col) & (n - 1)).astype(F32) * step
        cb_sc[...] = jnp.cos(ang)
        sb_sc[...] = jnp.sin(ang)

    col1 = lax.broadcasted_iota(jnp.int32, (1, L), 1)
    a = (((i * tr) * col1) & (n - 1)).astype(F32) * step
    ca, sa = jnp.cos(a), jnp.sin(a)
    cb, sb = cb_sc[...], sb_sc[...]
    c = cb * ca - sb * sa
    s = -(sb * ca + cb * sa)
    grow = row + i * tr
    fre_ref[...] = c.astype(fre_ref.dtype)
    fim_ref[...] = jnp.where(grow == 0, jnp.where((col & 1) == 0, 1.0, -1.0), s).astype(fim_ref.dtype)
    gim_ref[...] = jnp.where(col == 0, jnp.where((grow & 1) == 0, 1.0, -1.0), s).astype(gim_ref.dtype)


def _dft_tables(L, tr):
    spec = pl.BlockSpec((tr, L), lambda i: (i, 0))
    shape = jax.ShapeDtypeStruct((L, L), BF16)
    return pl.pallas_call(
        functools.partial(_dft_gen_kernel, L=L),
        grid=(L // tr,),
        out_specs=[spec, spec, spec],
        out_shape=[shape, shape, shape],
        scratch_shapes=[pltpu.VMEM((tr, L), F32), pltpu.VMEM((tr, L), F32)],
        compiler_params=_params("arbitrary"),
        name="dft_tables",
    )()


def _fwd_dft_kernel(*refs, with_filter):
    if with_filter:
        fre_ref, fim_ref, v_ref, kr_ref, ki_ref, y_ref = refs
    else:
        fre_ref, fim_ref, v_ref, y_ref = refs
    v = v_ref[...]
    vr = jnp.dot(fre_ref[...], v, preferred_element_type=F32)
    vi = jnp.dot(fim_ref[...], v, preferred_element_type=F32)
    if with_filter:
        kr, ki = kr_ref[...], ki_ref[...]
        tm = vr.shape[0]
        first = (lax.broadcasted_iota(jnp.int32, vr.shape, 0) + pl.program_id(1) * tm) == 0
        yr = vr * kr - jnp.where(first, 0.0, vi * ki)
        yi = jnp.where(first, vi * ki, vr * ki + vi * kr)
        y_ref[0] = yr.astype(y_ref.dtype)
        y_ref[1] = yi.astype(y_ref.dtype)
    else:
        y_ref[0] = vr.astype(y_ref.dtype)
        y_ref[1] = vi.astype(y_ref.dtype)


def _fwd_dft(fre, fim, v, kf, tm, out_dtype):
    B, L, D = v.shape
    ftile = pl.BlockSpec((tm, L), lambda b, m: (m, 0))
    in_specs = [ftile, ftile, pl.BlockSpec((None, L, D), lambda b, m: (b, 0, 0))]
    args = [fre, fim, v]
    if kf is not None:
        in_specs += [pl.BlockSpec((None, tm, D), lambda b, m: (0, m, 0)),
                     pl.BlockSpec((None, tm, D), lambda b, m: (1, m, 0))]
        args += [kf, kf]
    return pl.pallas_call(
        functools.partial(_fwd_dft_kernel, with_filter=kf is not None),
        grid=(B, L // tm),
        in_specs=in_specs,
        out_specs=pl.BlockSpec((None, 2, tm, D), lambda b, m: (b, 0, m, 0)),
        out_shape=jax.ShapeDtypeStruct((B, 2, L, D), out_dtype),
        compiler_params=_params("parallel", "parallel"),
        name="hyena_fwd_dft",
    )(*args)


def _filter_spectrum_kernel(hf_ref, hb_ref, k_ref, *, n):
    hfr, hfi = hf_ref[0], hf_ref[1]
    hbr, hbi = hb_ref[0], hb_ref[1]
    first = (lax.broadcasted_iota(jnp.int32, hfr.shape, 0) + pl.program_id(0) * hfr.shape[0]) == 0
    wgt = jnp.where(first, 1.0 / n, 2.0 / n)
    k_ref[0] = (hfr + hbr) * wgt
    k_ref[1] = jnp.where(first, hfi + hbi, hfi - hbi) * wgt


def _filter_spectrum(hspec, tm):
    _, _, L, D = hspec.shape
    return pl.pallas_call(
        functools.partial(_filter_spectrum_kernel, n=2 * L),
        grid=(L // tm,),
        in_specs=[pl.BlockSpec((None, 2, tm, D), lambda m: (0, 0, m, 0)),
                  pl.BlockSpec((None, 2, tm, D), lambda m: (1, 0, m, 0))],
        out_specs=pl.BlockSpec((2, tm, D), lambda m: (0, m, 0)),
        out_shape=jax.ShapeDtypeStruct((2, L, D), F32),
        compiler_params=_params("parallel"),
        name="hyena_filter_spectrum",
    )(hspec, hspec)


def _inv_dft_kernel(gre_ref, gim_ref, yr_ref, yi_ref, vx_ref, x0_ref, skip_ref, w_ref, b_ref,
                    x_ref, gate_ref, o_ref, acc_sc):
    kk = pl.program_id(2)

    @pl.when(kk == 0)
    def _():
        acc_sc[...] = jnp.zeros(acc_sc.shape, F32)

    acc_sc[...] += (jnp.dot(gre_ref[...], yr_ref[...], preferred_element_type=F32)
                    + jnp.dot(gim_ref[...], yi_ref[...], preferred_element_type=F32))

    @pl.when(kk == pl.num_programs(2) - 1)
    def _():
        z = (acc_sc[...] + vx_ref[...].astype(F32) * skip_ref[...]) * x0_ref[...]
        out = jnp.dot(z.astype(BF16), w_ref[...], preferred_element_type=F32) + b_ref[...]
        o_ref[...] = x_ref[...] + gate_ref[...] * out


def _inv_dft(gre, gim, y, vx, x0, skip, w_out, b_out, x, gate, rowmap, tm, tk):
    B, L, D = x.shape
    gtile = pl.BlockSpec((tm, tk), lambda b, i, k: (i, k))
    row = pl.BlockSpec((None, tm, D), lambda b, i, k: (b, i, 0))
    const = pl.BlockSpec((1, D), lambda b, i, k: (0, 0))
    return pl.pallas_call(
        _inv_dft_kernel,
        grid=(B, L // tm, L // tk),
        in_specs=[gtile, gtile,
                  pl.BlockSpec((None, None, tk, D), lambda b, i, k: (b, 0, k, 0)),
                  pl.BlockSpec((None, None, tk, D), lambda b, i, k: (b, 1, k, 0)),
                  row, row, const,
                  pl.BlockSpec((D, D), lambda b, i, k: (0, 0)), const,
                  row, pl.BlockSpec((None, 1, D), lambda b, i, k: (rowmap(b), 0, 0))],
        out_specs=row,
        out_shape=jax.ShapeDtypeStruct((B, L, D), F32),
        scratch_shapes=[pltpu.VMEM((tm, D), F32)],
        compiler_params=_params("parallel", "parallel", "arbitrary"),
        name="hyena_inv_dft",
    )(gre, gim, y, y, vx, x0, skip, w_out, b_out, x, gate)


FFT_N1 = 128
FFT_SLOT_GROUP = 8
FFT_LANE_TILE = 8192


def _cos_units(idx, n):
    return jnp.cos((idx & (n - 1)).astype(F32) * (2.0 * math.pi / n))


def _fft_tables_kernel(fa_ref, gi_ref, m3_ref, m3i_ref, *, n1, n2):
    n = n1 * n2
    h1, quarter = n1 // 2, n // 4
    half = n2 // 2
    g = pl.program_id(0)

    r = lax.broadcasted_iota(jnp.int32, (2 * h1, h1), 0)
    t1 = lax.broadcasted_iota(jnp.int32, (2 * h1, h1), 1)
    sl, im = r & (h1 - 1), (r >= h1).astype(jnp.int32)
    idx = jnp.where((im == 1) & (sl == 0), (n // 2) * t1, n2 * t1 * sl + quarter * im)
    fa_ref[...] = _cos_units(idx, n).astype(fa_ref.dtype)
    t1 = lax.broadcasted_iota(jnp.int32, (h1, 2 * h1), 0)
    c = lax.broadcasted_iota(jnp.int32, (h1, 2 * h1), 1)
    sl, im = c & (h1 - 1), (c >= h1).astype(jnp.int32)
    idx = jnp.where((im == 1) & (sl == 0), (n // 2) * t1, n2 * t1 * sl + quarter * im)
    gi_ref[...] = (jnp.where(sl == 0, 1.0, 2.0) * _cos_units(idx, n)).astype(gi_ref.dtype)

    r = lax.broadcasted_iota(jnp.int32, (2 * n2, 2 * n2), 0)
    c = lax.broadcasted_iota(jnp.int32, (2 * n2, 2 * n2), 1)
    rj, rim = r & (n2 - 1), (r >= n2).astype(jnp.int32)
    cj, cim = c & (n2 - 1), (c >= n2).astype(jnp.int32)
    for si in range(m3_ref.shape[0]):
        s = g * m3_ref.shape[0] + si
        fwd = _cos_units(cj * (s + n1 * rj) + quarter * (rim - cim), n)
        inv = _cos_units(rj * (s + n1 * cj) + quarter * (cim - rim), n)
        if si == 0:
            def slot0(j, t2, t_im, out_im):
                lo = (j < half) & (t_im == 0)
                hi = (j >= half) & (t_im == 1)
                idx0 = jnp.where((out_im == 1) & (j == 0), (n // 2) * t2, t2 * n1 * j + quarter * out_im)
                idxh = t2 * (n1 // 2 + n1 * (j - half)) + quarter * out_im
                return lo, hi, jnp.where(lo, idx0, idxh)
            lo, hi, idx = slot0(rj, cj, cim, rim)
            fwd0 = jnp.where(lo | hi, _cos_units(idx, n), 0.0)
            lo, hi, idx = slot0(cj, rj, rim, cim)
            amp = jnp.where(lo & (cj == 0), 1.0, 2.0)
            inv0 = jnp.where(lo | hi, amp * _cos_units(idx, n), 0.0)
            fwd = jnp.where(g == 0, fwd0, fwd)
            inv = jnp.where(g == 0, inv0, inv)
        m3_ref[si] = fwd.astype(m3_ref.dtype)
        m3i_ref[si] = inv.astype(m3i_ref.dtype)


def _fft_tables(n1, n2):
    h1, sg = n1 // 2, FFT_SLOT_GROUP
    mspec = pl.BlockSpec((sg, 2 * n2, 2 * n2), lambda g: (g, 0, 0))
    return pl.pallas_call(
        functools.partial(_fft_tables_kernel, n1=n1, n2=n2),
        grid=(h1 // sg,),
        out_specs=[pl.BlockSpec((2 * h1, h1), lambda g: (0, 0)),
                   pl.BlockSpec((h1, 2 * h1), lambda g: (0, 0)), mspec, mspec],
        out_shape=[jax.ShapeDtypeStruct((2 * h1, h1), BF16),
                   jax.ShapeDtypeStruct((h1, 2 * h1), BF16),
                   jax.ShapeDtypeStruct((h1, 2 * n2, 2 * n2), BF16),
                   jax.ShapeDtypeStruct((h1, 2 * n2, 2 * n2), BF16)],
        compiler_params=_params("arbitrary"),
        name="fft_tables",
    )()


def _fft_step1_kernel(fa_ref, x_ref, a_ref):
    h1 = x_ref.shape[0]
    a = jnp.dot(fa_ref[...], x_ref[...], preferred_element_type=F32)
    a_ref[0] = a[:h1].astype(a_ref.dtype)
    a_ref[1] = a[h1:].astype(a_ref.dtype)


def _fft_step1(fa, x):
    B, L, D = x.shape
    h1 = fa.shape[1]
    W = L // h1 * D
    tn = FFT_LANE_TILE
    return pl.pallas_call(
        _fft_step1_kernel,
        grid=(B, W // tn),
        in_specs=[pl.BlockSpec(fa.shape, lambda b, c: (0, 0)),
                  pl.BlockSpec((None, h1, tn), lambda b, c: (b, 0, c))],
        out_specs=pl.BlockSpec((None, 2, h1, tn), lambda b, c: (b, 0, 0, c)),
        out_shape=jax.ShapeDtypeStruct((B, 2, h1, W), BF16),
        compiler_params=_params("parallel", "parallel"),
        name="fft_step1",
    )(fa, x.reshape(B, h1, W))


def _first_bin(shape, g):
    return (lax.broadcasted_iota(jnp.int32, shape, 0) == 0) & (g == 0)


def _fft_filter_spectrum_kernel(m3_ref, a_ref, k_ref, *, n):
    n2 = a_ref.shape[-2]
    g = pl.program_id(0)
    for si in range(m3_ref.shape[0]):
        hf = jnp.dot(m3_ref[si], jnp.concatenate([a_ref[0, 0, si], a_ref[0, 1, si]], axis=0),
                     preferred_element_type=F32)
        hb = jnp.dot(m3_ref[si], jnp.concatenate([a_ref[1, 0, si], a_ref[1, 1, si]], axis=0),
                     preferred_element_type=F32)
        k_ref[0, si] = (hf[:n2] + hb[:n2]) * (1.0 / n)
        ki = hf[n2:] - hb[n2:]
        if si == 0:
            ki = jnp.where(_first_bin(ki.shape, g), hf[n2:] + hb[n2:], ki)
        k_ref[1, si] = ki * (1.0 / n)


def _fft_filter_spectrum(m3, a, D):
    h1, n2 = m3.shape[0], m3.shape[1] // 2
    sg = FFT_SLOT_GROUP
    return pl.pallas_call(
        functools.partial(_fft_filter_spectrum_kernel, n=2 * h1 * n2),
        grid=(h1 // sg,),
        in_specs=[pl.BlockSpec((sg, 2 * n2, 2 * n2), lambda g: (g, 0, 0)),
                  pl.BlockSpec((2, 2, sg, n2, D), lambda g: (0, 0, g, 0, 0))],
        out_specs=pl.BlockSpec((2, sg, n2, D), lambda g: (0, g, 0, 0)),
        out_shape=jax.ShapeDtypeStruct((2, h1, n2, D), F32),
        compiler_params=_params("parallel"),
        name="fft_filter_spectrum",
    )(m3, a.reshape(2, 2, h1, n2, D))


def _fft_step2_kernel(m3_ref, m3i_ref, kr_ref, ki_ref, a_ref, b_ref):
    n2 = a_ref.shape[-2]
    g = pl.program_id(0)
    for si in range(m3_ref.shape[0]):
        xh = jnp.dot(m3_ref[si], jnp.concatenate([a_ref[0, si], a_ref[1, si]], axis=0),
                     preferred_element_type=F32)
        xr, xi = xh[:n2], xh[n2:]
        kr, ki = kr_ref[si], ki_ref[si]
        yr = xr * kr - xi * ki
        yi = xr * ki + xi * kr
        if si == 0:
            first = _first_bin(yr.shape, g)
            yr = jnp.where(first, xr * kr, yr)
            yi = jnp.where(first, xi * ki, yi)
        bb = jnp.dot(m3i_ref[si], jnp.concatenate([yr, yi], axis=0).astype(BF16),
                     preferred_element_type=F32)
        b_ref[0, si] = bb[:n2].astype(b_ref.dtype)
        b_ref[1, si] = bb[n2:].astype(b_ref.dtype)


def _fft_step2(m3, m3i, kf, a, D):
    B, _, h1, W = a.shape
    n2 = W // D
    sg = FFT_SLOT_GROUP
    mspec = pl.BlockSpec((sg, 2 * n2, 2 * n2), lambda g, b: (g, 0, 0))
    blk = pl.BlockSpec((None, 2, sg, n2, D), lambda g, b: (b, 0, g, 0, 0))
    out = pl.pallas_call(
        _fft_step2_kernel,
        grid=(h1 // sg, B),
        in_specs=[mspec, mspec,
                  pl.BlockSpec((None, sg, n2, D), lambda g, b: (0, g, 0, 0)),
                  pl.BlockSpec((None, sg, n2, D), lambda g, b: (1, g, 0, 0)),
                  blk],
        out_specs=blk,
        out_shape=jax.ShapeDtypeStruct((B, 2, h1, n2, D), BF16),
        compiler_params=_params("parallel", "parallel"),
        name="fft_step2",
    )(m3, m3i, kf, kf, a.reshape(B, 2, h1, n2, D))
    return out.reshape(B, 2, h1, W)


def _fft_inv_step1_kernel(gi_ref, b_ref, vx_ref, x0_ref, skip_ref, z_ref):
    y = jnp.dot(gi_ref[...], jnp.concatenate([b_ref[0], b_ref[1]], axis=0), preferred_element_type=F32)
    z_ref[...] = ((y + vx_ref[...].astype(F32) * skip_ref[...]) * x0_ref[...]).astype(z_ref.dtype)


def _fft_inv_step1(gi, bm, vx, x0, skip):
    B, L, D = vx.shape
    h1 = gi.shape[0]
    W = L // h1 * D
    tn = FFT_LANE_TILE
    row = pl.BlockSpec((None, h1, tn), lambda b, c: (b, 0, c))
    z = pl.pallas_call(
        _fft_inv_step1_kernel,
        grid=(B, W // tn),
        in_specs=[pl.BlockSpec(gi.shape, lambda b, c: (0, 0)),
                  pl.BlockSpec((None, 2, h1, tn), lambda b, c: (b, 0, 0, c)),
                  row, row, pl.BlockSpec((1, tn), lambda b, c: (0, 0))],
        out_specs=row,
        out_shape=jax.ShapeDtypeStruct((B, h1, W), BF16),
        compiler_params=_params("parallel", "parallel"),
        name="fft_inv_step1",
    )(gi, bm, vx.reshape(B, h1, W), x0.reshape(B, h1, W), jnp.tile(skip, (1, tn // D)))
    return z.reshape(B, L, D)


def _rope_tables(L):
    pos = jnp.arange(L, dtype=jnp.int32)
    rows = (pos // GRID_W).astype(F32)
    cols = (pos % GRID_W).astype(F32)
    inv = ROPE_BASE ** (-jnp.arange(ROPE_PAIRS, dtype=F32) / ROPE_PAIRS)
    lane = jnp.arange(LANES)
    within = lane % HEAD_DIM
    axis = within // (2 * ROPE_PAIRS)
    half = (within % (2 * ROPE_PAIRS)) // ROPE_PAIRS
    ang = jnp.where(axis[None, :] == 0, rows[:, None], cols[:, None]) * inv[within % ROPE_PAIRS][None, :]
    cos, sin = jnp.cos(ang), jnp.sin(ang)
    sin_a = jnp.where(half[None, :] == 0, -sin, 0.0)
    sin_b = jnp.where(half[None, :] == 1, sin, 0.0)
    return cos, sin_a, sin_b


def _tile(L, pref):
    return min(L, pref)


def _four_step(L):
    return (2 * L) // FFT_N1 >= 32


def _conv_tables(L):
    return _fft_tables(FFT_N1, 2 * L // FFT_N1) if _four_step(L) else _dft_tables(L, min(L, 256))


def _run_group(x, rowmap, mods, p, cache, rope_tabs, dft, kv_dtype):
    B, L, D = x.shape
    tm = _tile(L, 512)
    ctx_k, ctx_v = [], []
    for i in range(DEPTH):
        j = i // 2
        sh1, sc1, g1, sh2, sc2, g2 = (mods[i, :, s * D:(s + 1) * D].reshape(MOD_ROWS, 1, D) for s in range(6))
        n1 = p["norm1_g"][i][None]
        if i % 2 == 0:
            lam_init = 0.8 - 0.6 * math.exp(-0.3 * i)
            q, k, v = _qkv(x, n1, sh1, sc1, p["attn_w_qkv"][j], rowmap, rope_tabs, kv_dtype, tm)
            ctx_k.append(k)
            ctx_v.append(v)
            o = _attention(q, k, v, None if cache is None else cache + (j,),
                           p["attn_lambda"][j].astype(F32), p["attn_subln_g"][j][None],
                           lam_init, _tile(L, 512), _tile(L, 512))
            x = _proj_res(o, p["attn_w_o"][j], None, x, g1, rowmap, tm)
        else:
            u = _in_proj(x, n1, sh1, sc1, p["hy_w_in"][j], p["hy_b_in"][j][None], rowmap, tm)
            vx, x0 = _short_conv(u, p["hy_conv_w"][j], p["hy_conv_b"][j][None], LANES)
            filt = _implicit_filter(L, p["filt_w1"][j], p["filt_b1"][j], p["filt_w2"][j], p["filt_b2"][j],
                                    p["filt_w3"][j], p["filt_b3"][j], p["filt_freq"][j], 256)
            skip, w_out, b_out = p["hy_skip"][j][None], p["hy_w_out"][j], p["hy_b_out"][j][None]
            if _four_step(L):
                fa, gi, m3, m3i = dft
                kf = _fft_filter_spectrum(m3, _fft_step1(fa, filt), D)
                bm = _fft_step2(m3, m3i, kf, _fft_step1(fa, vx), D)
                z = _fft_inv_step1(gi, bm, vx, x0, skip)
                x = _proj_res(z, w_out, b_out, x, g1, rowmap, tm)
            else:
                fre, fim, gim = dft
                tmf = _tile(L, 256)
                kf = _filter_spectrum(_fwd_dft(fre, fim, filt, None, tmf, F32), tmf)
                y = _fwd_dft(fre, fim, vx, kf, tmf, BF16)
                x = _inv_dft(fre, gim, y, vx, x0, skip, w_out, b_out, x, g1, rowmap, tm, _tile(L, 512))
        final_g = p["final_g"][None] if i == DEPTH - 1 else None
        x = _ffn(x, p["norm2_g"][i][None], sh2, sc2, g2, p["ffn_w_gu"][i], p["ffn_w_down"][i],
                 final_g, rowmap, tm)
    return x, ctx_k, ctx_v


def kernel(x_prompt, x_sample, cache_k, cache_v, c, c_ctx, ada_w, ada_b, norm1_g, norm2_g, attn_w_qkv, attn_lambda, attn_subln_g, attn_w_o, hy_w_in, hy_b_in, hy_conv_w, hy_conv_b, filt_w1, filt_b1, filt_w2, filt_b2, filt_w3, filt_b3, filt_freq, hy_skip, hy_w_out, hy_b_out, ffn_w_gu, ffn_w_down, final_g):
    D = D_MODEL
    nb = c.shape[0]
    cmat = jnp.concatenate([c, c_ctx[None], jnp.zeros((MOD_ROWS - nb - 1, D), F32)], axis=0)
    mods = _adaln_all(cmat, ada_w, ada_b)

    p = dict(norm1_g=norm1_g, norm2_g=norm2_g, attn_lambda=attn_lambda, attn_subln_g=attn_subln_g,
             hy_b_in=hy_b_in, hy_conv_w=hy_conv_w, hy_conv_b=hy_conv_b,
             filt_w1=filt_w1, filt_b1=filt_b1, filt_w2=filt_w2, filt_b2=filt_b2, filt_w3=filt_w3,
             filt_b3=filt_b3, filt_freq=filt_freq, hy_skip=hy_skip, hy_b_out=hy_b_out, final_g=final_g,
             attn_w_qkv=attn_w_qkv.astype(BF16), attn_w_o=attn_w_o.astype(BF16),
             hy_w_in=hy_w_in.astype(BF16), hy_w_out=hy_w_out.astype(BF16),
             ffn_w_gu=ffn_w_gu.astype(BF16), ffn_w_down=ffn_w_down.astype(BF16))

    Lp, Ls = x_prompt.shape[1], x_sample.shape[1]
    y_prompt, ctx_k, ctx_v = _run_group(x_prompt, lambda b: CTX_ROW, mods, p, None, None,
                                        _conv_tables(Lp), F32)
    cache = tuple(a.reshape(a.shape[0], a.shape[1], a.shape[2], D) for a in (cache_k, cache_v))
    y_sample, _, _ = _run_group(x_sample, lambda b: b, mods, p, cache, _rope_tables(Ls),
                                _conv_tables(Ls), BF16)

    Bp = x_prompt.shape[0]
    new_k = jnp.stack(ctx_k, axis=1).reshape(Bp, len(ctx_k), Lp, N_HEADS, 2, HEAD_DIM)
    new_v = jnp.stack(ctx_v, axis=1).reshape(Bp, len(ctx_v), Lp, N_HEADS, V_DIM)
    return y_prompt, y_sample, new_k, new_v
```

```python
import functools
import math

import jax
import jax.numpy as jnp
from jax import lax
from jax.experimental import pallas as pl
from jax.experimental.pallas import tpu as pltpu

D_MODEL = 1024
DEPTH = 4
GRID_W = 64
N_HEADS = 8
HEAD_DIM = 64
V_DIM = 2 * HEAD_DIM
ROPE_PAIRS = HEAD_DIM // 4
ROPE_BASE = 10000.0
FILT_BANDS = 16
FILT_ORDER = 64
FILT_TARGET = 1e-2
FILT_FAST_PCT = 0.3
FILT_SLOW_PCT = 1.5
FILT_EPS = 1e-6
D_FF = 2816
EPS = 1e-6
SUBLN_EPS = 1e-5

LANES = 128
MOD_ROWS = 16
CTX_ROW = 8
VMEM_LIMIT = 56 * 1024 * 1024

F32 = jnp.float32
BF16 = jnp.bfloat16
HI = lax.Precision.HIGHEST


def _params(*sem):
    return pltpu.CompilerParams(dimension_semantics=sem, vmem_limit_bytes=VMEM_LIMIT)


def _lane_tile(x, reps):
    return x if reps == 1 else jnp.concatenate([x] * reps, axis=1)


def _norm_mod(x, g, shift, scale):
    ms = jnp.mean(x * x, axis=-1, keepdims=True)
    return (x * lax.rsqrt(ms + EPS) * g) * (1.0 + scale) + shift


def _mod_kernel(c_ref, w_ref, b_ref, o_ref):
    c = c_ref[...]
    a = c * jax.nn.sigmoid(c)
    o_ref[...] = jnp.dot(a, w_ref[...], preferred_element_type=F32, precision=HI) + b_ref[...]


def _adaln_all(cmat, ada_w, ada_b):
    n6 = 6 * D_MODEL
    tn = 1536
    return pl.pallas_call(
        _mod_kernel,
        grid=(DEPTH, n6 // tn),
        in_specs=[
            pl.BlockSpec((MOD_ROWS, D_MODEL), lambda i, n: (0, 0)),
            pl.BlockSpec((None, D_MODEL, tn), lambda i, n: (i, 0, n)),
            pl.BlockSpec((None, 1, tn), lambda i, n: (i, 0, n)),
        ],
        out_specs=pl.BlockSpec((None, MOD_ROWS, tn), lambda i, n: (i, 0, n)),
        out_shape=jax.ShapeDtypeStruct((DEPTH, MOD_ROWS, n6), F32),
        compiler_params=_params("parallel", "parallel"),
        name="adaln",
    )(cmat, ada_w, ada_b.reshape(DEPTH, 1, n6))


def _rope_chunk(x, cos, sin_a, sin_b):
    return x * cos + pltpu.roll(x, LANES - ROPE_PAIRS, 1) * sin_a + pltpu.roll(x, ROPE_PAIRS, 1) * sin_b


def _qkv_kernel(*refs, rope, tn):
    if rope:
        x_ref, g_ref, sh_ref, sc_ref, w_ref, cos_ref, sa_ref, sb_ref, q_ref, k_ref, v_ref = refs
    else:
        x_ref, g_ref, sh_ref, sc_ref, w_ref, q_ref, k_ref, v_ref = refs
    h = _norm_mod(x_ref[...], g_ref[...], sh_ref[...], sc_ref[...]).astype(BF16)
    outs = (q_ref, k_ref, v_ref)
    qk_scale = HEAD_DIM ** -0.5 * math.log2(math.e)
    for part in range(3):
        for n in range(D_MODEL // tn):
            col = part * D_MODEL + n * tn
            y = jnp.dot(h, w_ref[:, col:col + tn], preferred_element_type=F32)
            for j in range(tn // LANES):
                yj = y[:, j * LANES:(j + 1) * LANES]
                if rope and part < 2:
                    yj = _rope_chunk(yj, cos_ref[...], sa_ref[...], sb_ref[...])
                if part == 0:
                    yj = yj * qk_scale
                lo = n * tn + j * LANES
                outs[part][:, lo:lo + LANES] = yj.astype(outs[part].dtype)


def _qkv(x, g, shift, scale, w, rowmap, rope_tabs, kv_dtype, tm):
    B, L, D = x.shape
    rope = rope_tabs is not None
    row = pl.BlockSpec((None, tm, D), lambda b, i: (b, i, 0))
    vec = pl.BlockSpec((None, 1, D), lambda b, i: (rowmap(b), 0, 0))
    in_specs = [row, pl.BlockSpec((1, D), lambda b, i: (0, 0)), vec, vec,
                pl.BlockSpec((D, 3 * D), lambda b, i: (0, 0))]
    args = [x, g, shift, scale, w]
    if rope:
        tab = pl.BlockSpec((tm, LANES), lambda b, i: (i, 0))
        in_specs += [tab, tab, tab]
        args += list(rope_tabs)
    return pl.pallas_call(
        functools.partial(_qkv_kernel, rope=rope, tn=512),
        grid=(B, L // tm),
        in_specs=in_specs,
        out_specs=[row, row, row],
        out_shape=[jax.ShapeDtypeStruct((B, L, D), BF16),
                   jax.ShapeDtypeStruct((B, L, D), kv_dtype),
                   jax.ShapeDtypeStruct((B, L, D), kv_dtype)],
        compiler_params=_params("parallel", "parallel"),
        name="qkv",
    )(*args)


def _in_proj_kernel(x_ref, g_ref, sh_ref, sc_ref, w_ref, b_ref, o_ref, *, tn):
    h = _norm_mod(x_ref[...], g_ref[...], sh_ref[...], sc_ref[...]).astype(BF16)
    for n in range(o_ref.shape[-1] // tn):
        sl = slice(n * tn, (n + 1) * tn)
        o_ref[:, sl] = jnp.dot(h, w_ref[:, sl], preferred_element_type=F32) + b_ref[:, sl]


def _in_proj(x, g, shift, scale, w, b, rowmap, tm):
    B, L, D = x.shape
    N = w.shape[1]
    vec = pl.BlockSpec((None, 1, D), lambda b_, i: (rowmap(b_), 0, 0))
    return pl.pallas_call(
        functools.partial(_in_proj_kernel, tn=512),
        grid=(B, L // tm),
        in_specs=[pl.BlockSpec((None, tm, D), lambda b_, i: (b_, i, 0)),
                  pl.BlockSpec((1, D), lambda b_, i: (0, 0)), vec, vec,
                  pl.BlockSpec((D, N), lambda b_, i: (0, 0)),
                  pl.BlockSpec((1, N), lambda b_, i: (0, 0))],
        out_specs=pl.BlockSpec((None, tm, N), lambda b_, i: (b_, i, 0)),
        out_shape=jax.ShapeDtypeStruct((B, L, N), F32),
        compiler_params=_params("parallel", "parallel"),
        name="hyena_in_proj",
    )(x, g, shift, scale, w, b)


def _attn_kernel(*refs, tq, tk, lam_init, cached):
    if cached:
        lam_ref, g_ref, q_ref, ck_ref, cv_ref, k_ref, v_ref, o_ref = refs[:8]
    else:
        lam_ref, g_ref, q_ref, k_ref, v_ref, o_ref = refs[:6]
    m_sc, acc_sc, vx_sc, q2_sc, s0_sc, s1_sc, s2_sc, t0_sc, t1_sc, t2_sc = refs[-10:]
    nc = ck_ref.shape[0] // tk if cached else 0
    nk = nc + k_ref.shape[0] // tk
    nq = q_ref.shape[0] // tq
    if cached:
        vx_sc[:nc * tk, :V_DIM] = cv_ref[...].astype(BF16)
    vx_sc[nc * tk:, :V_DIM] = v_ref[...].astype(BF16)
    vx_sc[:, V_DIM:] = jnp.ones((vx_sc.shape[0], V_DIM), BF16)
    lp = lam_ref[...]
    lam = (jnp.exp(jnp.sum(lp[0:1] * lp[1:2], axis=1, keepdims=True))
           - jnp.exp(jnp.sum(lp[2:3] * lp[3:4], axis=1, keepdims=True)) + lam_init)
    lane = lax.broadcasted_iota(jnp.int32, (tq, LANES), 1)

    def slot(j):
        if j == 0:
            return s0_sc, t0_sc
        return (s1_sc, t1_sc) if j % 2 == 1 else (s2_sc, t2_sc)

    def key_tile(j):
        if j < nc:
            return ck_ref[j * tk:(j + 1) * tk, :].astype(BF16)
        return k_ref[(j - nc) * tk:(j - nc + 1) * tk, :].astype(BF16)

    def load_q(i):
        q = q_ref[pl.ds(pl.multiple_of(i * tq, tq), tq), :]
        zero = jnp.zeros_like(q)
        q2_sc[:tq] = jnp.where(lane < HEAD_DIM, q, zero)
        q2_sc[tq:] = jnp.where(lane >= HEAD_DIM, q, zero)

    def scores(j):
        s_sc, t_sc = slot(j)
        s = lax.dot_general(q2_sc[...], key_tile(j), (((1,), (1,)), ((), ())), preferred_element_type=F32)
        s_sc[...] = s
        t_sc[...] = jnp.broadcast_to(jnp.max(s, axis=1, keepdims=True), t_sc.shape)

    def accumulate(j):
        s_sc, t_sc = slot(j)
        m_prev = m_sc[...]
        m_new = jnp.maximum(m_prev, t_sc[...])
        alpha = jnp.exp2(m_prev - m_new)
        e = jnp.exp2(s_sc[...] - _lane_tile(m_new, tk // LANES))
        acc_sc[...] = (_lane_tile(alpha, 2) * acc_sc[...]
                       + jnp.dot(e.astype(BF16), vx_sc[j * tk:(j + 1) * tk, :], preferred_element_type=F32))
        m_sc[...] = m_new

    def init():
        m_sc[...] = jnp.full(m_sc.shape, -jnp.inf, F32)
        acc_sc[...] = jnp.zeros(acc_sc.shape, F32)

    def finalize(i):
        o = (acc_sc[:tq, :V_DIM] / acc_sc[:tq, V_DIM:] - lam * (acc_sc[tq:, :V_DIM] / acc_sc[tq:, V_DIM:]))
        ms = jnp.mean(o * o, axis=-1, keepdims=True)
        y = (o * lax.rsqrt(ms + SUBLN_EPS) * g_ref[...]) * (1.0 - lam_init)
        o_ref[pl.ds(pl.multiple_of(i * tq, tq), tq), :] = y.astype(o_ref.dtype)

    def query_tile(i, prefetch):
        for j in range(1, nk):
            accumulate(j - 1)
            scores(j)
        if prefetch:
            load_q(jnp.minimum(i + 1, nq - 1))
        accumulate(nk - 1)
        if prefetch:
            scores(0)
        finalize(i)
        init()

    init()
    load_q(0)
    scores(0)
    if nq == 1:
        query_tile(0, False)
    else:
        def body(i, carry):
            query_tile(i, True)
            return carry
        lax.fori_loop(0, nq, body, 0)


def _attention(q, k, v, cache, lam_params, subln_g, lam_init, tq, tk):
    B, Lq, D = q.shape
    Lk = k.shape[1]
    kv = pl.BlockSpec((None, Lk, LANES), lambda b, h: (b, 0, h))
    qo = pl.BlockSpec((None, Lq, LANES), lambda b, h: (b, 0, h))
    in_specs = [pl.BlockSpec((4, HEAD_DIM), lambda b, h: (0, 0)),
                pl.BlockSpec((1, V_DIM), lambda b, h: (0, 0)), qo]
    args = [lam_params, subln_g, q]
    past = 0
    if cache is not None:
        ck, cv, layer = cache
        past = ck.shape[2]
        cspec = pl.BlockSpec((None, None, past, LANES), lambda b, h: (b, layer, 0, h))
        in_specs += [cspec, cspec]
        args += [ck, cv]
    in_specs += [kv, kv]
    args += [k, v]
    return pl.pallas_call(
        functools.partial(_attn_kernel, tq=tq, tk=tk, lam_init=lam_init, cached=cache is not None),
        grid=(B, N_HEADS),
        in_specs=in_specs,
        out_specs=qo,
        out_shape=jax.ShapeDtypeStruct((B, Lq, D), BF16),
        scratch_shapes=[pltpu.VMEM((2 * tq, LANES), F32),
                        pltpu.VMEM((2 * tq, 2 * V_DIM), F32),
                        pltpu.VMEM((past + Lk, 2 * V_DIM), BF16),
                        pltpu.VMEM((2 * tq, LANES), BF16)]
                       + [pltpu.VMEM((2 * tq, tk), F32)] * 3 + [pltpu.VMEM((2 * tq, LANES), F32)] * 3,
        compiler_params=_params("parallel", "parallel"),
        name="diff_attention",
    )(*args)


def _proj_res_kernel(*refs, bias):
    if bias:
        a_ref, w_ref, b_ref, x_ref, gate_ref, o_ref = refs
    else:
        a_ref, w_ref, x_ref, gate_ref, o_ref = refs
    y = jnp.dot(a_ref[...], w_ref[...], preferred_element_type=F32)
    if bias:
        y = y + b_ref[...]
    o_ref[...] = x_ref[...] + gate_ref[...] * y


def _proj_res(a, w, b, x, gate, rowmap, tm):
    B, L, D = x.shape
    row = pl.BlockSpec((None, tm, D), lambda b_, i: (b_, i, 0))
    in_specs = [row, pl.BlockSpec((D, D), lambda b_, i: (0, 0))]
    args = [a, w]
    if b is not None:
        in_specs.append(pl.BlockSpec((1, D), lambda b_, i: (0, 0)))
        args.append(b)
    in_specs += [row, pl.BlockSpec((None, 1, D), lambda b_, i: (rowmap(b_), 0, 0))]
    args += [x, gate]
    return pl.pallas_call(
        functools.partial(_proj_res_kernel, bias=b is not None),
        grid=(B, L // tm),
        in_specs=in_specs,
        out_specs=row,
        out_shape=jax.ShapeDtypeStruct((B, L, D), F32),
        compiler_params=_params("parallel", "parallel"),
        name="out_proj",
    )(*args)


def _ffn_kernel(*refs, fc, final):
    if final:
        x_ref, g_ref, sh_ref, sc_ref, gate_ref, wgu_ref, wd_ref, fg_ref, o_ref = refs
    else:
        x_ref, g_ref, sh_ref, sc_ref, gate_ref, wgu_ref, wd_ref, o_ref = refs
    x = x_ref[...]
    h = _norm_mod(x, g_ref[...], sh_ref[...], sc_ref[...]).astype(BF16)
    acc = jnp.zeros(x.shape, F32)
    for f in range(D_FF // fc):
        gg = jnp.dot(h, wgu_ref[:, f * fc:(f + 1) * fc], preferred_element_type=F32)
        uu = jnp.dot(h, wgu_ref[:, D_FF + f * fc:D_FF + (f + 1) * fc], preferred_element_type=F32)
        a = (gg * jax.nn.sigmoid(gg)) * uu
        acc = acc + jnp.dot(a.astype(BF16), wd_ref[f * fc:(f + 1) * fc, :], preferred_element_type=F32)
    y = x + gate_ref[...] * acc
    if final:
        ms = jnp.mean(y * y, axis=-1, keepdims=True)
        y = y * lax.rsqrt(ms + EPS) * fg_ref[...]
    o_ref[...] = y


def _ffn(x, g, shift, scale, gate, w_gu, w_down, final_g, rowmap, tm):
    B, L, D = x.shape
    final = final_g is not None
    row = pl.BlockSpec((None, tm, D), lambda b, i: (b, i, 0))
    vec = pl.BlockSpec((None, 1, D), lambda b, i: (rowmap(b), 0, 0))
    const = pl.BlockSpec((1, D), lambda b, i: (0, 0))
    in_specs = [row, const, vec, vec, vec,
                pl.BlockSpec((D, 2 * D_FF), lambda b, i: (0, 0)),
                pl.BlockSpec((D_FF, D), lambda b, i: (0, 0))]
    args = [x, g, shift, scale, gate, w_gu, w_down]
    if final:
        in_specs.append(const)
        args.append(final_g)
    return pl.pallas_call(
        functools.partial(_ffn_kernel, fc=256, final=final),
        grid=(B, L // tm),
        in_specs=in_specs,
        out_specs=row,
        out_shape=jax.ShapeDtypeStruct((B, L, D), F32),
        compiler_params=_params("parallel", "parallel"),
        name="ffn",
    )(*args)


def _conv3(u, w, b):
    L = u.shape[0]
    row = lax.broadcasted_iota(jnp.int32, u.shape, 0)
    prev = jnp.where(row == 0, 0.0, pltpu.roll(u, 1, 0))
    nxt = jnp.where(row == L - 1, 0.0, pltpu.roll(u, L - 1, 0))
    return prev * w[0:1] + u * w[1:2] + nxt * w[2:3] + b


def _short_conv_kernel(u0_ref, u1_ref, u2_ref, w0_ref, w1_ref, w2_ref, b0_ref, b1_ref, b2_ref,
                       vx_ref, x0_ref):
    x0_ref[...] = _conv3(u0_ref[...], w0_ref[...], b0_ref[...])
    vx = _conv3(u2_ref[...], w2_ref[...], b2_ref[...]) * _conv3(u1_ref[...], w1_ref[...], b1_ref[...])
    vx_ref[...] = vx.astype(vx_ref.dtype)


def _short_conv(u, conv_w, conv_b, tc):
    B, L, D3 = u.shape
    D = D3 // 3
    nb = D // tc
    specs = []
    for arr_block in ((None, L, tc), (3, tc), (1, tc)):
        for part in range(3):
            if len(arr_block) == 3:
                specs.append(pl.BlockSpec(arr_block, lambda b, c, part=part: (b, 0, part * nb + c)))
            else:
                specs.append(pl.BlockSpec(arr_block, lambda b, c, part=part: (0, part * nb + c)))
    out = pl.BlockSpec((None, L, tc), lambda b, c: (b, 0, c))
    return pl.pallas_call(
        _short_conv_kernel,
        grid=(B, nb),
        in_specs=specs,
        out_specs=[out, out],
        out_shape=[jax.ShapeDtypeStruct((B, L, D), BF16),
                   jax.ShapeDtypeStruct((B, L, D), F32)],
        compiler_params=_params("parallel", "parallel"),
        name="hyena_short_conv",
    )(u, u, u, conv_w, conv_w, conv_w, conv_b, conv_b, conv_b)


def _filter_kernel(bands_ref, w1t_ref, w1c_ref, w1s_ref, b1_ref, w2_ref, b2_ref, fr_ref,
                   w3f_ref, w3b_ref, b3f_ref, b3b_ref, delta_ref, h_ref, hid_sc):
    L = hid_sc.shape[0]
    pos = lax.broadcasted_iota(jnp.int32, (L, LANES), 0).astype(F32)
    t = pos / max(L - 1, 1)

    @pl.when(pl.program_id(0) == 0)
    def _():
        ang = (2.0 * math.pi * pos / L) * bands_ref[...]
        fr = fr_ref[...]
        pre = (t * w1t_ref[...]
               + jnp.dot(jnp.cos(ang), w1c_ref[...], preferred_element_type=F32, precision=HI)
               - jnp.dot(jnp.sin(ang), w1s_ref[...], preferred_element_type=F32, precision=HI)
               + b1_ref[...])
        hid = jnp.sin(fr * pre)
        hid_sc[...] = jnp.sin(fr * (jnp.dot(hid, w2_ref[...], preferred_element_type=F32, precision=HI)
                                    + b2_ref[...]))

    hid = hid_sc[...]
    window = jnp.exp(-t[:, 0:1] * delta_ref[...])
    hf = (jnp.dot(hid, w3f_ref[...], preferred_element_type=F32, precision=HI) + b3f_ref[...]) * window
    hb = (jnp.dot(hid, w3b_ref[...], preferred_element_type=F32, precision=HI) + b3b_ref[...]) * window
    norm = (jnp.sum(jnp.abs(hf), axis=0, keepdims=True) + jnp.sum(jnp.abs(hb), axis=0, keepdims=True)
            + FILT_EPS)
    row = lax.broadcasted_iota(jnp.int32, hb.shape, 0)
    h_ref[0] = (hf / norm).astype(h_ref.dtype)
    h_ref[1] = jnp.where(row == 0, 0.0, hb / norm).astype(h_ref.dtype)


def _pad2(a, rows, cols):
    return jnp.pad(a, ((0, rows - a.shape[0]), (0, cols - a.shape[1])))


def _implicit_filter(L, w1, b1, w2, b2, w3, b3, freq, tc):
    D = D_MODEL
    bands = _pad2(jnp.linspace(1e-4, FILT_BANDS - 1, FILT_BANDS, dtype=F32)[None], 1, LANES)
    deltas = jnp.abs(jnp.linspace(math.log(FILT_TARGET) / FILT_SLOW_PCT,
                                  math.log(FILT_TARGET) / FILT_FAST_PCT, D, dtype=F32))[None]
    w1 = w1.astype(F32)
    args = [bands,
            _pad2(w1[0:1], 1, LANES),
            _pad2(w1[1:1 + FILT_BANDS], LANES, LANES),
            _pad2(w1[1 + FILT_BANDS:], LANES, LANES),
            _pad2(b1[None], 1, LANES), _pad2(w2, LANES, LANES), _pad2(b2[None], 1, LANES),
            _pad2(freq[None], 1, LANES),
            _pad2(w3, LANES, 2 * D), _pad2(w3, LANES, 2 * D), b3[None], b3[None], deltas]
    small = lambda shape: pl.BlockSpec(shape, lambda c: (0, 0))
    nb = D // tc
    in_specs = [small((1, LANES)), small((1, LANES)), small((LANES, LANES)), small((LANES, LANES)),
                small((1, LANES)), small((LANES, LANES)), small((1, LANES)), small((1, LANES)),
                pl.BlockSpec((LANES, tc), lambda c: (0, c)),
                pl.BlockSpec((LANES, tc), lambda c: (0, nb + c)),
                pl.BlockSpec((1, tc), lambda c: (0, c)),
                pl.BlockSpec((1, tc), lambda c: (0, nb + c)),
                pl.BlockSpec((1, tc), lambda c: (0, c))]
    return pl.pallas_call(
        _filter_kernel,
        grid=(nb,),
        in_specs=in_specs,
        out_specs=pl.BlockSpec((2, L, tc), lambda c: (0, 0, c)),
        out_shape=jax.ShapeDtypeStruct((2, L, D), BF16),
        scratch_shapes=[pltpu.VMEM((L, LANES), F32)],
        compiler_params=_params("arbitrary"),
        name="hyena_filter",
    )(*args)


def _dft_gen_kernel(fre_ref, fim_ref, gim_ref, cb_sc, sb_sc, *, L):
    n = 2 * L
    tr = fre_ref.shape[0]
    step = 2.0 * math.pi / n
    i = pl.program_id(0)
    row = lax.broadcasted_iota(jnp.int32, (tr, L), 0)
    col = lax.broadcasted_iota(jnp.int32, (tr, L), 1)

    @pl.when(i == 0)
    def _():
        ang = ((row * col) & (n - 1)).astype(F32) * step
        cb_sc[...] = jnp.cos(ang)
        sb_sc[...] = jnp.sin(ang)

    col1 = lax.broadcasted_iota(jnp.int32, (1, L), 1)
    a = (((i * tr) * col1) & (n - 1)).astype(F32) * step
    ca, sa = jnp.cos(a), jnp.sin(a)
    cb, sb = cb_sc[...], sb_sc[...]
    c = cb * ca - sb * sa
    s = -(sb * ca + cb * sa)
    grow = row + i * tr
    fre_ref[...] = c.astype(fre_ref.dtype)
    fim_ref[...] = jnp.where(grow == 0, jnp.where((col & 1) == 0, 1.0, -1.0), s).astype(fim_ref.dtype)
    gim_ref[...] = jnp.where(col == 0, jnp.where((grow & 1) == 0, 1.0, -1.0), s).astype(gim_ref.dtype)


def _dft_tables(L, tr):
    spec = pl.BlockSpec((tr, L), lambda i: (i, 0))
    shape = jax.ShapeDtypeStruct((L, L), BF16)
    return pl.pallas_call(
        functools.partial(_dft_gen_kernel, L=L),
        grid=(L // tr,),
        out_specs=[spec, spec, spec],
        out_shape=[shape, shape, shape],
        scratch_shapes=[pltpu.VMEM((tr, L), F32), pltpu.VMEM((tr, L), F32)],
        compiler_params=_params("arbitrary"),
        name="dft_tables",
    )()


def _fwd_dft_kernel(*refs, with_filter):
    if with_filter:
        fre_ref, fim_ref, v_ref, kr_ref, ki_ref, y_ref = refs
    else:
        fre_ref, fim_ref, v_ref, y_ref = refs
    v = v_ref[...]
    vr = jnp.dot(fre_ref[...], v, preferred_element_type=F32)
    vi = jnp.dot(fim_ref[...], v, preferred_element_type=F32)
    if with_filter:
        kr, ki = kr_ref[...], ki_ref[...]
        tm = vr.shape[0]
        first = (lax.broadcasted_iota(jnp.int32, vr.shape, 0) + pl.program_id(1) * tm) == 0
        yr = vr * kr - jnp.where(first, 0.0, vi * ki)
        yi = jnp.where(first, vi * ki, vr * ki + vi * kr)
        y_ref[0] = yr.astype(y_ref.dtype)
        y_ref[1] = yi.astype(y_ref.dtype)
    else:
        y_ref[0] = vr.astype(y_ref.dtype)
        y_ref[1] = vi.astype(y_ref.dtype)


def _fwd_dft(fre, fim, v, kf, tm, out_dtype):
    B, L, D = v.shape
    ftile = pl.BlockSpec((tm, L), lambda b, m: (m, 0))
    in_specs = [ftile, ftile, pl.BlockSpec((None, L, D), lambda b, m: (b, 0, 0))]
    args = [fre, fim, v]
    if kf is not None:
        in_specs += [pl.BlockSpec((None, tm, D), lambda b, m: (0, m, 0)),
                     pl.BlockSpec((None, tm, D), lambda b, m: (1, m, 0))]
        args += [kf, kf]
    return pl.pallas_call(
        functools.partial(_fwd_dft_kernel, with_filter=kf is not None),
        grid=(B, L // tm),
        in_specs=in_specs,
        out_specs=pl.BlockSpec((None, 2, tm, D), lambda b, m: (b, 0, m, 0)),
        out_shape=jax.ShapeDtypeStruct((B, 2, L, D), out_dtype),
        compiler_params=_params("parallel", "parallel"),
        name="hyena_fwd_dft",
    )(*args)


def _filter_spectrum_kernel(hf_ref, hb_ref, k_ref, *, n):
    hfr, hfi = hf_ref[0], hf_ref[1]
    hbr, hbi = hb_ref[0], hb_ref[1]
    first = (lax.broadcasted_iota(jnp.int32, hfr.shape, 0) + pl.program_id(0) * hfr.shape[0]) == 0
    wgt = jnp.where(first, 1.0 / n, 2.0 / n)
    k_ref[0] = (hfr + hbr) * wgt
    k_ref[1] = jnp.where(first, hfi + hbi, hfi - hbi) * wgt


def _filter_spectrum(hspec, tm):
    _, _, L, D = hspec.shape
    return pl.pallas_call(
        functools.partial(_filter_spectrum_kernel, n=2 * L),
        grid=(L // tm,),
        in_specs=[pl.BlockSpec((None, 2, tm, D), lambda m: (0, 0, m, 0)),
                  pl.BlockSpec((None, 2, tm, D), lambda m: (1, 0, m, 0))],
        out_specs=pl.BlockSpec((2, tm, D), lambda m: (0, m, 0)),
        out_shape=jax.ShapeDtypeStruct((2, L, D), F32),
        compiler_params=_params("parallel"),
        name="hyena_filter_spectrum",
    )(hspec, hspec)


def _inv_dft_kernel(gre_ref, gim_ref, yr_ref, yi_ref, vx_ref, x0_ref, skip_ref, w_ref, b_ref,
                    x_ref, gate_ref, o_ref, acc_sc):
    kk = pl.program_id(2)

    @pl.when(kk == 0)
    def _():
        acc_sc[...] = jnp.zeros(acc_sc.shape, F32)

    acc_sc[...] += (jnp.dot(gre_ref[...], yr_ref[...], preferred_element_type=F32)
                    + jnp.dot(gim_ref[...], yi_ref[...], preferred_element_type=F32))

    @pl.when(kk == pl.num_programs(2) - 1)
    def _():
        z = (acc_sc[...] + vx_ref[...].astype(F32) * skip_ref[...]) * x0_ref[...]
        out = jnp.dot(z.astype(BF16), w_ref[...], preferred_element_type=F32) + b_ref[...]
        o_ref[...] = x_ref[...] + gate_ref[...] * out


def _inv_dft(gre, gim, y, vx, x0, skip, w_out, b_out, x, gate, rowmap, tm, tk):
    B, L, D = x.shape
    gtile = pl.BlockSpec((tm, tk), lambda b, i, k: (i, k))
    row = pl.BlockSpec((None, tm, D), lambda b, i, k: (b, i, 0))
    const = pl.BlockSpec((1, D), lambda b, i, k: (0, 0))
    return pl.pallas_call(
        _inv_dft_kernel,
        grid=(B, L // tm, L // tk),
        in_specs=[gtile, gtile,
                  pl.BlockSpec((None, None, tk, D), lambda b, i, k: (b, 0, k, 0)),
                  pl.BlockSpec((None, None, tk, D), lambda b, i, k: (b, 1, k, 0)),
                  row, row, const,
                  pl.BlockSpec((D, D), lambda b, i, k: (0, 0)), const,
                  row, pl.BlockSpec((None, 1, D), lambda b, i, k: (rowmap(b), 0, 0))],
        out_specs=row,
        out_shape=jax.ShapeDtypeStruct((B, L, D), F32),
        scratch_shapes=[pltpu.VMEM((tm, D), F32)],
        compiler_params=_params("parallel", "parallel", "arbitrary"),
        name="hyena_inv_dft",
    )(gre, gim, y, y, vx, x0, skip, w_out, b_out, x, gate)


FFT_N1 = 128
FFT_SLOT_GROUP = 8
FFT_ROWS = 16


def _cos_units(idx, n):
    return jnp.cos((idx & (n - 1)).astype(F32) * (2.0 * math.pi / n))


def _fft_step1_tables_kernel(fb_ref, gb_ref, *, n1, n2):
    n = n1 * n2
    h1, quarter, R = n1 // 2, n // 4, FFT_ROWS
    lr, lh = R.bit_length() - 1, h1.bit_length() - 1
    g = pl.program_id(0)

    def entry(part, slot, t1):
        idx = jnp.where((part == 1) & (slot == 0), (n // 2) * t1, n2 * t1 * slot + quarter * part)
        return _cos_units(idx, n)

    rows, cols = fb_ref.shape
    r = lax.broadcasted_iota(jnp.int32, (rows, cols), 0) + g * rows
    c = lax.broadcasted_iota(jnp.int32, (rows, cols), 1)
    val = entry(r >> (lr + lh), (r >> lr) & (h1 - 1), c >> lr)
    fb_ref[...] = jnp.where((r & (R - 1)) == (c & (R - 1)), val, 0.0).astype(fb_ref.dtype)
    rows, cols = gb_ref.shape
    r = lax.broadcasted_iota(jnp.int32, (rows, cols), 0) + g * rows
    c = lax.broadcasted_iota(jnp.int32, (rows, cols), 1)
    slot = (c >> lr) & (h1 - 1)
    val = jnp.where(slot == 0, 1.0, 2.0) * entry(c >> (lr + lh), slot, r >> lr)
    gb_ref[...] = jnp.where((r & (R - 1)) == (c & (R - 1)), val, 0.0).astype(gb_ref.dtype)


def _fft_step2_tables_kernel(m3_ref, m3i_ref, *, n1, n2):
    n = n1 * n2
    quarter, half = n // 4, n2 // 2
    g = pl.program_id(0)
    r = lax.broadcasted_iota(jnp.int32, (2 * n2, 2 * n2), 0)
    c = lax.broadcasted_iota(jnp.int32, (2 * n2, 2 * n2), 1)
    rj, rim = r & (n2 - 1), (r >= n2).astype(jnp.int32)
    cj, cim = c & (n2 - 1), (c >= n2).astype(jnp.int32)
    for si in range(m3_ref.shape[0]):
        s = g * m3_ref.shape[0] + si
        fwd = _cos_units(cj * (s + n1 * rj) + quarter * (rim - cim), n)
        inv = _cos_units(rj * (s + n1 * cj) + quarter * (cim - rim), n)
        if si == 0:
            def slot0(j, t2, t_im, out_im):
                lo = (j < half) & (t_im == 0)
                hi = (j >= half) & (t_im == 1)
                idx0 = jnp.where((out_im == 1) & (j == 0), (n // 2) * t2, t2 * n1 * j + quarter * out_im)
                idxh = t2 * (n1 // 2 + n1 * (j - half)) + quarter * out_im
                return lo, hi, jnp.where(lo, idx0, idxh)
            lo, hi, idx = slot0(rj, cj, cim, rim)
            fwd0 = jnp.where(lo | hi, _cos_units(idx, n), 0.0)
            lo, hi, idx = slot0(cj, rj, rim, cim)
            amp = jnp.where(lo & (cj == 0), 1.0, 2.0)
            inv0 = jnp.where(lo | hi, amp * _cos_units(idx, n), 0.0)
            fwd = jnp.where(g == 0, fwd0, fwd)
            inv = jnp.where(g == 0, inv0, inv)
        m3_ref[si] = fwd.astype(m3_ref.dtype)
        m3i_ref[si] = inv.astype(m3i_ref.dtype)


def _fft_tables(n1, n2):
    h1, sg, R = n1 // 2, FFT_SLOT_GROUP, FFT_ROWS
    steps = 8
    fb, gb = pl.pallas_call(
        functools.partial(_fft_step1_tables_kernel, n1=n1, n2=n2),
        grid=(steps,),
        out_specs=[pl.BlockSpec((2 * h1 * R // steps, h1 * R), lambda g: (g, 0)),
                   pl.BlockSpec((h1 * R // steps, 2 * h1 * R), lambda g: (g, 0))],
        out_shape=[jax.ShapeDtypeStruct((2 * h1 * R, h1 * R), BF16),
                   jax.ShapeDtypeStruct((h1 * R, 2 * h1 * R), BF16)],
        compiler_params=_params("parallel"),
        name="fft_step1_tables",
    )()
    mspec = pl.BlockSpec((sg, 2 * n2, 2 * n2), lambda g: (g, 0, 0))
    m3, m3i = pl.pallas_call(
        functools.partial(_fft_step2_tables_kernel, n1=n1, n2=n2),
        grid=(h1 // sg,),
        out_specs=[mspec, mspec],
        out_shape=[jax.ShapeDtypeStruct((h1, 2 * n2, 2 * n2), BF16)] * 2,
        compiler_params=_params("parallel"),
        name="fft_step2_tables",
    )()
    return fb, gb, m3, m3i


def _fft_step1_kernel(fb_ref, x_ref, a_ref):
    h1, R, D = x_ref.shape
    a = jnp.dot(fb_ref[...], x_ref[...].reshape(h1 * R, D), preferred_element_type=F32)
    a_ref[...] = a.reshape(2, h1, R, D).astype(a_ref.dtype)


def _fft_step1(fb, x):
    B, L, D = x.shape
    R = FFT_ROWS
    h1 = fb.shape[1] // R
    n2 = L // h1
    return pl.pallas_call(
        _fft_step1_kernel,
        grid=(B, n2 // R),
        in_specs=[pl.BlockSpec(fb.shape, lambda b, c: (0, 0)),
                  pl.BlockSpec((None, h1, R, D), lambda b, c: (b, 0, c, 0))],
        out_specs=pl.BlockSpec((None, 2, h1, R, D), lambda b, c: (b, 0, 0, c, 0)),
        out_shape=jax.ShapeDtypeStruct((B, 2, h1, n2, D), BF16),
        compiler_params=_params("parallel", "parallel"),
        name="fft_step1",
    )(fb, x.reshape(B, h1, n2, D))


def _first_bin(shape, g):
    return (lax.broadcasted_iota(jnp.int32, shape, 0) == 0) & (g == 0)


def _fft_filter_spectrum_kernel(m3_ref, a_ref, k_ref, *, n):
    n2 = a_ref.shape[-2]
    g = pl.program_id(0)
    for si in range(m3_ref.shape[0]):
        hf = jnp.dot(m3_ref[si], jnp.concatenate([a_ref[0, 0, si], a_ref[0, 1, si]], axis=0),
                     preferred_element_type=F32)
        hb = jnp.dot(m3_ref[si], jnp.concatenate([a_ref[1, 0, si], a_ref[1, 1, si]], axis=0),
                     preferred_element_type=F32)
        k_ref[0, si] = (hf[:n2] + hb[:n2]) * (1.0 / n)
        ki = hf[n2:] - hb[n2:]
        if si == 0:
            ki = jnp.where(_first_bin(ki.shape, g), hf[n2:] + hb[n2:], ki)
        k_ref[1, si] = ki * (1.0 / n)


def _fft_filter_spectrum(m3, a):
    h1, n2 = m3.shape[0], m3.shape[1] // 2
    D = a.shape[-1]
    sg = FFT_SLOT_GROUP
    return pl.pallas_call(
        functools.partial(_fft_filter_spectrum_kernel, n=2 * h1 * n2),
        grid=(h1 // sg,),
        in_specs=[pl.BlockSpec((sg, 2 * n2, 2 * n2), lambda g: (g, 0, 0)),
                  pl.BlockSpec((2, 2, sg, n2, D), lambda g: (0, 0, g, 0, 0))],
        out_specs=pl.BlockSpec((2, sg, n2, D), lambda g: (0, g, 0, 0)),
        out_shape=jax.ShapeDtypeStruct((2, h1, n2, D), F32),
        compiler_params=_params("parallel"),
        name="fft_filter_spectrum",
    )(m3, a)


def _fft_step2_kernel(m3_ref, m3i_ref, kr_ref, ki_ref, a_ref, b_ref):
    n2 = a_ref.shape[-2]
    g = pl.program_id(0)
    for si in range(m3_ref.shape[0]):
        xh = jnp.dot(m3_ref[si], jnp.concatenate([a_ref[0, si], a_ref[1, si]], axis=0),
                     preferred_element_type=F32)
        xr, xi = xh[:n2], xh[n2:]
        kr, ki = kr_ref[si], ki_ref[si]
        yr = xr * kr - xi * ki
        yi = xr * ki + xi * kr
        if si == 0:
            first = _first_bin(yr.shape, g)
            yr = jnp.where(first, xr * kr, yr)
            yi = jnp.where(first, xi * ki, yi)
        bb = jnp.dot(m3i_ref[si], jnp.concatenate([yr, yi], axis=0).astype(BF16),
                     preferred_element_type=F32)
        b_ref[0, si] = bb[:n2].astype(b_ref.dtype)
        b_ref[1, si] = bb[n2:].astype(b_ref.dtype)


def _fft_step2(m3, m3i, kf, a):
    B, _, h1, n2, D = a.shape
    sg = FFT_SLOT_GROUP
    mspec = pl.BlockSpec((sg, 2 * n2, 2 * n2), lambda g, b: (g, 0, 0))
    blk = pl.BlockSpec((None, 2, sg, n2, D), lambda g, b: (b, 0, g, 0, 0))
    return pl.pallas_call(
        _fft_step2_kernel,
        grid=(h1 // sg, B),
        in_specs=[mspec, mspec,
                  pl.BlockSpec((None, sg, n2, D), lambda g, b: (0, g, 0, 0)),
                  pl.BlockSpec((None, sg, n2, D), lambda g, b: (1, g, 0, 0)),
                  blk],
        out_specs=blk,
        out_shape=jax.ShapeDtypeStruct((B, 2, h1, n2, D), BF16),
        compiler_params=_params("parallel", "parallel"),
        name="fft_step2",
    )(m3, m3i, kf, kf, a)


def _fft_inv_step1_kernel(gb_ref, b_ref, vx_ref, x0_ref, skip_ref, z_ref):
    _, h1, R, D = b_ref.shape
    y = jnp.dot(gb_ref[...], b_ref[...].reshape(2 * h1 * R, D), preferred_element_type=F32)
    z = (y.reshape(h1, R, D) + vx_ref[...].astype(F32) * skip_ref[...]) * x0_ref[...]
    z_ref[...] = z.astype(z_ref.dtype)


def _fft_inv_step1(gb, bm, vx, x0, skip):
    B, L, D = vx.shape
    R = FFT_ROWS
    h1 = gb.shape[0] // R
    n2 = L // h1
    blk = pl.BlockSpec((None, h1, R, D), lambda b, c: (b, 0, c, 0))
    z = pl.pallas_call(
        _fft_inv_step1_kernel,
        grid=(B, n2 // R),
        in_specs=[pl.BlockSpec(gb.shape, lambda b, c: (0, 0)),
                  pl.BlockSpec((None, 2, h1, R, D), lambda b, c: (b, 0, 0, c, 0)),
                  blk, blk, pl.BlockSpec((1, D), lambda b, c: (0, 0))],
        out_specs=blk,
        out_shape=jax.ShapeDtypeStruct((B, h1, n2, D), BF16),
        compiler_params=_params("parallel", "parallel"),
        name="fft_inv_step1",
    )(gb, bm, vx.reshape(B, h1, n2, D), x0.reshape(B, h1, n2, D), skip)
    return z.reshape(B, L, D)


def _rope_tables(L):
    pos = jnp.arange(L, dtype=jnp.int32)
    rows = (pos // GRID_W).astype(F32)
    cols = (pos % GRID_W).astype(F32)
    inv = ROPE_BASE ** (-jnp.arange(ROPE_PAIRS, dtype=F32) / ROPE_PAIRS)
    lane = jnp.arange(LANES)
    within = lane % HEAD_DIM
    axis = within // (2 * ROPE_PAIRS)
    half = (within % (2 * ROPE_PAIRS)) // ROPE_PAIRS
    ang = jnp.where(axis[None, :] == 0, rows[:, None], cols[:, None]) * inv[within % ROPE_PAIRS][None, :]
    cos, sin = jnp.cos(ang), jnp.sin(ang)
    sin_a = jnp.where(half[None, :] == 0, -sin, 0.0)
    sin_b = jnp.where(half[None, :] == 1, sin, 0.0)
    return cos, sin_a, sin_b


def _tile(L, pref):
    return min(L, pref)


def _four_step(L):
    return (2 * L) // FFT_N1 >= 32


def _conv_tables(L):
    return _fft_tables(FFT_N1, 2 * L // FFT_N1) if _four_step(L) else _dft_tables(L, min(L, 256))


def _run_group(x, rowmap, mods, p, cache, rope_tabs, dft, kv_dtype):
    B, L, D = x.shape
    tm = _tile(L, 512)
    ctx_k, ctx_v = [], []
    for i in range(DEPTH):
        j = i // 2
        sh1, sc1, g1, sh2, sc2, g2 = (mods[i, :, s * D:(s + 1) * D].reshape(MOD_ROWS, 1, D) for s in range(6))
        n1 = p["norm1_g"][i][None]
        if i % 2 == 0:
            lam_init = 0.8 - 0.6 * math.exp(-0.3 * i)
            q, k, v = _qkv(x, n1, sh1, sc1, p["attn_w_qkv"][j], rowmap, rope_tabs, kv_dtype, tm)
            ctx_k.append(k)
            ctx_v.append(v)
            o = _attention(q, k, v, None if cache is None else cache + (j,),
                           p["attn_lambda"][j].astype(F32), p["attn_subln_g"][j][None],
                           lam_init, _tile(L, 512), _tile(L, 512))
            x = _proj_res(o, p["attn_w_o"][j], None, x, g1, rowmap, tm)
        else:
            u = _in_proj(x, n1, sh1, sc1, p["hy_w_in"][j], p["hy_b_in"][j][None], rowmap, tm)
            vx, x0 = _short_conv(u, p["hy_conv_w"][j], p["hy_conv_b"][j][None], LANES)
            filt = _implicit_filter(L, p["filt_w1"][j], p["filt_b1"][j], p["filt_w2"][j], p["filt_b2"][j],
                                    p["filt_w3"][j], p["filt_b3"][j], p["filt_freq"][j], 256)
            skip, w_out, b_out = p["hy_skip"][j][None], p["hy_w_out"][j], p["hy_b_out"][j][None]
            if _four_step(L):
                fb, gb, m3, m3i = dft
                kf = _fft_filter_spectrum(m3, _fft_step1(fb, filt))
                bm = _fft_step2(m3, m3i, kf, _fft_step1(fb, vx))
                z = _fft_inv_step1(gb, bm, vx, x0, skip)
                x = _proj_res(z, w_out, b_out, x, g1, rowmap, tm)
            else:
                fre, fim, gim = dft
                tmf = _tile(L, 256)
                kf = _filter_spectrum(_fwd_dft(fre, fim, filt, None, tmf, F32), tmf)
                y = _fwd_dft(fre, fim, vx, kf, tmf, BF16)
                x = _inv_dft(fre, gim, y, vx, x0, skip, w_out, b_out, x, g1, rowmap, tm, _tile(L, 512))
        final_g = p["final_g"][None] if i == DEPTH - 1 else None
        x = _ffn(x, p["norm2_g"][i][None], sh2, sc2, g2, p["ffn_w_gu"][i], p["ffn_w_down"][i],
                 final_g, rowmap, tm)
    return x, ctx_k, ctx_v


def kernel(x_prompt, x_sample, cache_k, cache_v, c, c_ctx, ada_w, ada_b, norm1_g, norm2_g, attn_w_qkv, attn_lambda, attn_subln_g, attn_w_o, hy_w_in, hy_b_in, hy_conv_w, hy_conv_b, filt_w1, filt_b1, filt_w2, filt_b2, filt_w3, filt_b3, filt_freq, hy_skip, hy_w_out, hy_b_out, ffn_w_gu, ffn_w_down, final_g):
    D = D_MODEL
    nb = c.shape[0]
    cmat = jnp.concatenate([c, c_ctx[None], jnp.zeros((MOD_ROWS - nb - 1, D), F32)], axis=0)
    mods = _adaln_all(cmat, ada_w, ada_b)

    p = dict(norm1_g=norm1_g, norm2_g=norm2_g, attn_lambda=attn_lambda, attn_subln_g=attn_subln_g,
             hy_b_in=hy_b_in, hy_conv_w=hy_conv_w, hy_conv_b=hy_conv_b,
             filt_w1=filt_w1, filt_b1=filt_b1, filt_w2=filt_w2, filt_b2=filt_b2, filt_w3=filt_w3,
             filt_b3=filt_b3, filt_freq=filt_freq, hy_skip=hy_skip, hy_b_out=hy_b_out, final_g=final_g,
             attn_w_qkv=attn_w_qkv.astype(BF16), attn_w_o=attn_w_o.astype(BF16),
             hy_w_in=hy_w_in.astype(BF16), hy_w_out=hy_w_out.astype(BF16),
             ffn_w_gu=ffn_w_gu.astype(BF16), ffn_w_down=ffn_w_down.astype(BF16))

    Lp, Ls = x_prompt.shape[1], x_sample.shape[1]
    y_prompt, ctx_k, ctx_v = _run_group(x_prompt, lambda b: CTX_ROW, mods, p, None, None,
                                        _conv_tables(Lp), F32)
    cache = tuple(a.reshape(a.shape[0], a.shape[1], a.shape[2], D) for a in (cache_k, cache_v))
    y_sample, _, _ = _run_group(x_sample, lambda b: b, mods, p, cache, _rope_tables(Ls),
                                _conv_tables(Ls), BF16)

    Bp = x_prompt.shape[0]
    new_k = jnp.stack(ctx_k, axis=1).reshape(Bp, len(ctx_k), Lp, N_HEADS, 2, HEAD_DIM)
    new_v = jnp.stack(ctx_v, axis=1).reshape(Bp, len(ctx_v), Lp, N_HEADS, V_DIM)
    return y_prompt, y_sample, new_k, new_v
```

```python
import functools
import math

import jax
import jax.numpy as jnp
from jax import lax
from jax.experimental import pallas as pl
from jax.experimental.pallas import tpu as pltpu

D_MODEL = 1024
DEPTH = 4
GRID_W = 64
N_HEADS = 8
HEAD_DIM = 64
V_DIM = 2 * HEAD_DIM
ROPE_PAIRS = HEAD_DIM // 4
ROPE_BASE = 10000.0
FILT_BANDS = 16
FILT_ORDER = 64
FILT_TARGET = 1e-2
FILT_FAST_PCT = 0.3
FILT_SLOW_PCT = 1.5
FILT_EPS = 1e-6
D_FF = 2816
EPS = 1e-6
SUBLN_EPS = 1e-5

LANES = 128
MOD_ROWS = 16
CTX_ROW = 8
VMEM_LIMIT = 56 * 1024 * 1024

F32 = jnp.float32
BF16 = jnp.bfloat16
HI = lax.Precision.HIGHEST


def _params(*sem):
    return pltpu.CompilerParams(dimension_semantics=sem, vmem_limit_bytes=VMEM_LIMIT)


def _lane_tile(x, reps):
    return x if reps == 1 else jnp.concatenate([x] * reps, axis=1)


def _norm_mod(x, g, shift, scale):
    ms = jnp.mean(x * x, axis=-1, keepdims=True)
    return (x * lax.rsqrt(ms + EPS) * g) * (1.0 + scale) + shift


def _mod_kernel(c_ref, w_ref, b_ref, o_ref):
    c = c_ref[...]
    a = c * jax.nn.sigmoid(c)
    o_ref[...] = jnp.dot(a, w_ref[...], preferred_element_type=F32, precision=HI) + b_ref[...]


def _adaln_all(cmat, ada_w, ada_b):
    n6 = 6 * D_MODEL
    tn = 1536
    return pl.pallas_call(
        _mod_kernel,
        grid=(DEPTH, n6 // tn),
        in_specs=[
            pl.BlockSpec((MOD_ROWS, D_MODEL), lambda i, n: (0, 0)),
            pl.BlockSpec((None, D_MODEL, tn), lambda i, n: (i, 0, n)),
            pl.BlockSpec((None, 1, tn), lambda i, n: (i, 0, n)),
        ],
        out_specs=pl.BlockSpec((None, MOD_ROWS, tn), lambda i, n: (i, 0, n)),
        out_shape=jax.ShapeDtypeStruct((DEPTH, MOD_ROWS, n6), F32),
        compiler_params=_params("parallel", "parallel"),
        name="adaln",
    )(cmat, ada_w, ada_b.reshape(DEPTH, 1, n6))


def _rope_chunk(x, cos, sin_a, sin_b):
    return x * cos + pltpu.roll(x, LANES - ROPE_PAIRS, 1) * sin_a + pltpu.roll(x, ROPE_PAIRS, 1) * sin_b


def _qkv_kernel(*refs, rope, tn):
    if rope:
        x_ref, g_ref, sh_ref, sc_ref, w_ref, cos_ref, sa_ref, sb_ref, q_ref, k_ref, v_ref = refs
    else:
        x_ref, g_ref, sh_ref, sc_ref, w_ref, q_ref, k_ref, v_ref = refs
    h = _norm_mod(x_ref[...], g_ref[...], sh_ref[...], sc_ref[...]).astype(BF16)
    outs = (q_ref, k_ref, v_ref)
    qk_scale = HEAD_DIM ** -0.5 * math.log2(math.e)
    for part in range(3):
        for n in range(D_MODEL // tn):
            col = part * D_MODEL + n * tn
            y = jnp.dot(h, w_ref[:, col:col + tn], preferred_element_type=F32)
            for j in range(tn // LANES):
                yj = y[:, j * LANES:(j + 1) * LANES]
                if rope and part < 2:
                    yj = _rope_chunk(yj, cos_ref[...], sa_ref[...], sb_ref[...])
                if part == 0:
                    yj = yj * qk_scale
                lo = n * tn + j * LANES
                outs[part][:, lo:lo + LANES] = yj.astype(outs[part].dtype)


def _qkv(x, g, shift, scale, w, rowmap, rope_tabs, kv_dtype, tm):
    B, L, D = x.shape
    rope = rope_tabs is not None
    row = pl.BlockSpec((None, tm, D), lambda b, i: (b, i, 0))
    vec = pl.BlockSpec((None, 1, D), lambda b, i: (rowmap(b), 0, 0))
    in_specs = [row, pl.BlockSpec((1, D), lambda b, i: (0, 0)), vec, vec,
                pl.BlockSpec((D, 3 * D), lambda b, i: (0, 0))]
    args = [x, g, shift, scale, w]
    if rope:
        tab = pl.BlockSpec((tm, LANES), lambda b, i: (i, 0))
        in_specs += [tab, tab, tab]
        args += list(rope_tabs)
    return pl.pallas_call(
        functools.partial(_qkv_kernel, rope=rope, tn=512),
        grid=(B, L // tm),
        in_specs=in_specs,
        out_specs=[row, row, row],
        out_shape=[jax.ShapeDtypeStruct((B, L, D), BF16),
                   jax.ShapeDtypeStruct((B, L, D), kv_dtype),
                   jax.ShapeDtypeStruct((B, L, D), kv_dtype)],
        compiler_params=_params("parallel", "parallel"),
        name="qkv",
    )(*args)


def _conv3(u, w, b):
    rows = u.shape[0] - 16
    prev = pltpu.roll(u, 1, 0)[8:8 + rows]
    nxt = pltpu.roll(u, u.shape[0] - 1, 0)[8:8 + rows]
    return prev * w[0:1] + u[8:8 + rows] * w[1:2] + nxt * w[2:3] + b


def _hyena_in_kernel(xp_ref, x_ref, xn_ref, g_ref, sh_ref, sc_ref, w_ref, b_ref, cw_ref, cb_ref,
                     vx_ref, x0_ref, *, tn):
    tm, D = x_ref.shape
    i = pl.program_id(1)
    xa = jnp.concatenate([xp_ref[...], x_ref[...], xn_ref[...]], axis=0)
    h = _norm_mod(xa, g_ref[...], sh_ref[...], sc_ref[...]).astype(BF16)
    keep_top = (i > 0).astype(F32)
    keep_bot = (i < pl.num_programs(1) - 1).astype(F32)

    def part(p, n):
        sl = slice(p * D + n * tn, p * D + (n + 1) * tn)
        u = jnp.dot(h, w_ref[:, sl], preferred_element_type=F32) + b_ref[:, sl]
        u = jnp.concatenate([u[:8] * keep_top, u[8:8 + tm], u[8 + tm:] * keep_bot], axis=0)
        return _conv3(u, cw_ref[:, sl], cb_ref[:, sl])

    for n in range(D // tn):
        out = slice(n * tn, (n + 1) * tn)
        x0_ref[:, out] = part(0, n)
        vx_ref[:, out] = (part(2, n) * part(1, n)).astype(vx_ref.dtype)


def _hyena_in(x, g, shift, scale, w, b, conv_w, conv_b, rowmap, tm):
    B, L, D = x.shape
    N = w.shape[1]
    hb = tm // 8
    vec = pl.BlockSpec((None, 1, D), lambda b_, i: (rowmap(b_), 0, 0))
    row = pl.BlockSpec((None, tm, D), lambda b_, i: (b_, i, 0))
    const = lambda r: pl.BlockSpec((r, N), lambda b_, i: (0, 0))
    return pl.pallas_call(
        functools.partial(_hyena_in_kernel, tn=512),
        grid=(B, L // tm),
        in_specs=[pl.BlockSpec((None, 8, D), lambda b_, i: (b_, jnp.maximum(i * hb - 1, 0), 0)),
                  row,
                  pl.BlockSpec((None, 8, D), lambda b_, i: (b_, jnp.minimum((i + 1) * hb, L // 8 - 1), 0)),
                  pl.BlockSpec((1, D), lambda b_, i: (0, 0)), vec, vec,
                  pl.BlockSpec((D, N), lambda b_, i: (0, 0)), const(1), const(3), const(1)],
        out_specs=[row, row],
        out_shape=[jax.ShapeDtypeStruct((B, L, D), BF16), jax.ShapeDtypeStruct((B, L, D), F32)],
        compiler_params=_params("parallel", "parallel"),
        name="hyena_in",
    )(x, x, x, g, shift, scale, w, b, conv_w, conv_b)


def _attn_kernel(*refs, tq, tk, lam_init, cached):
    if cached:
        lam_ref, g_ref, q_ref, ck_ref, cv_ref, k_ref, v_ref, o_ref = refs[:8]
    else:
        lam_ref, g_ref, q_ref, k_ref, v_ref, o_ref = refs[:6]
    m_sc, acc_sc, vx_sc, q2_sc, s0_sc, s1_sc, s2_sc, t0_sc, t1_sc, t2_sc = refs[-10:]
    nc = ck_ref.shape[0] // tk if cached else 0
    nk = nc + k_ref.shape[0] // tk
    nq = q_ref.shape[0] // tq
    if cached:
        vx_sc[:nc * tk, :V_DIM] = cv_ref[...].astype(BF16)
    vx_sc[nc * tk:, :V_DIM] = v_ref[...].astype(BF16)
    vx_sc[:, V_DIM:] = jnp.ones((vx_sc.shape[0], V_DIM), BF16)
    lp = lam_ref[...]
    lam = (jnp.exp(jnp.sum(lp[0:1] * lp[1:2], axis=1, keepdims=True))
           - jnp.exp(jnp.sum(lp[2:3] * lp[3:4], axis=1, keepdims=True)) + lam_init)
    lane = lax.broadcasted_iota(jnp.int32, (tq, LANES), 1)

    def slot(j):
        if j == 0:
            return s0_sc, t0_sc
        return (s1_sc, t1_sc) if j % 2 == 1 else (s2_sc, t2_sc)

    def key_tile(j):
        if j < nc:
            return ck_ref[j * tk:(j + 1) * tk, :].astype(BF16)
        return k_ref[(j - nc) * tk:(j - nc + 1) * tk, :].astype(BF16)

    def load_q(i):
        q = q_ref[pl.ds(pl.multiple_of(i * tq, tq), tq), :]
        zero = jnp.zeros_like(q)
        q2_sc[:tq] = jnp.where(lane < HEAD_DIM, q, zero)
        q2_sc[tq:] = jnp.where(lane >= HEAD_DIM, q, zero)

    def scores(j):
        s_sc, t_sc = slot(j)
        s = lax.dot_general(q2_sc[...], key_tile(j), (((1,), (1,)), ((), ())), preferred_element_type=F32)
        s_sc[...] = s
        t_sc[...] = jnp.broadcast_to(jnp.max(s, axis=1, keepdims=True), t_sc.shape)

    def accumulate(j):
        s_sc, t_sc = slot(j)
        m_prev = m_sc[...]
        m_new = jnp.maximum(m_prev, t_sc[...])
        alpha = jnp.exp2(m_prev - m_new)
        e = jnp.exp2(s_sc[...] - _lane_tile(m_new, tk // LANES))
        acc_sc[...] = (_lane_tile(alpha, 2) * acc_sc[...]
                       + jnp.dot(e.astype(BF16), vx_sc[j * tk:(j + 1) * tk, :], preferred_element_type=F32))
        m_sc[...] = m_new

    def init():
        m_sc[...] = jnp.full(m_sc.shape, -jnp.inf, F32)
        acc_sc[...] = jnp.zeros(acc_sc.shape, F32)

    def finalize(i):
        o = (acc_sc[:tq, :V_DIM] / acc_sc[:tq, V_DIM:] - lam * (acc_sc[tq:, :V_DIM] / acc_sc[tq:, V_DIM:]))
        ms = jnp.mean(o * o, axis=-1, keepdims=True)
        y = (o * lax.rsqrt(ms + SUBLN_EPS) * g_ref[...]) * (1.0 - lam_init)
        o_ref[pl.ds(pl.multiple_of(i * tq, tq), tq), :] = y.astype(o_ref.dtype)

    def query_tile(i, prefetch):
        for j in range(1, nk):
            accumulate(j - 1)
            scores(j)
        if prefetch:
            load_q(jnp.minimum(i + 1, nq - 1))
        accumulate(nk - 1)
        if prefetch:
            scores(0)
        finalize(i)
        init()

    init()
    load_q(0)
    scores(0)
    if nq == 1:
        query_tile(0, False)
    else:
        def body(i, carry):
            query_tile(i, True)
            return carry
        lax.fori_loop(0, nq, body, 0)


def _attention(q, k, v, cache, lam_params, subln_g, lam_init, tq, tk):
    B, Lq, D = q.shape
    Lk = k.shape[1]
    kv = pl.BlockSpec((None, Lk, LANES), lambda b, h: (b, 0, h))
    qo = pl.BlockSpec((None, Lq, LANES), lambda b, h: (b, 0, h))
    in_specs = [pl.BlockSpec((4, HEAD_DIM), lambda b, h: (0, 0)),
                pl.BlockSpec((1, V_DIM), lambda b, h: (0, 0)), qo]
    args = [lam_params, subln_g, q]
    past = 0
    if cache is not None:
        ck, cv, layer = cache
        past = ck.shape[2]
        cspec = pl.BlockSpec((None, None, past, LANES), lambda b, h: (b, layer, 0, h))
        in_specs += [cspec, cspec]
        args += [ck, cv]
    in_specs += [kv, kv]
    args += [k, v]
    return pl.pallas_call(
        functools.partial(_attn_kernel, tq=tq, tk=tk, lam_init=lam_init, cached=cache is not None),
        grid=(B, N_HEADS),
        in_specs=in_specs,
        out_specs=qo,
        out_shape=jax.ShapeDtypeStruct((B, Lq, D), BF16),
        scratch_shapes=[pltpu.VMEM((2 * tq, LANES), F32),
                        pltpu.VMEM((2 * tq, 2 * V_DIM), F32),
                        pltpu.VMEM((past + Lk, 2 * V_DIM), BF16),
                        pltpu.VMEM((2 * tq, LANES), BF16)]
                       + [pltpu.VMEM((2 * tq, tk), F32)] * 3 + [pltpu.VMEM((2 * tq, LANES), F32)] * 3,
        compiler_params=_params("parallel", "parallel"),
        name="diff_attention",
    )(*args)


def _resident(shape):
    return pl.BlockSpec(shape, lambda b, i: (0,) * len(shape), pipeline_mode=pl.Buffered(1))


def _out_ffn_kernel(*refs, fc, final, bias):
    refs = list(refs)
    o_ref = refs.pop()
    fg_ref = refs.pop() if final else None
    a_ref, wo_ref = refs[:2]
    bo_ref = refs[2] if bias else None
    x_ref, gate1_ref, g_ref, sh_ref, sc_ref, gate2_ref, wgu_ref, wd_ref = refs[2 + bias:]
    y = jnp.dot(a_ref[...], wo_ref[...], preferred_element_type=F32)
    if bias:
        y = y + bo_ref[...]
    x = x_ref[...] + gate1_ref[...] * y
    h = _norm_mod(x, g_ref[...], sh_ref[...], sc_ref[...]).astype(BF16)
    acc = jnp.zeros(x.shape, F32)
    for f in range(D_FF // fc):
        gg = jnp.dot(h, wgu_ref[:, f * fc:(f + 1) * fc], preferred_element_type=F32)
        uu = jnp.dot(h, wgu_ref[:, D_FF + f * fc:D_FF + (f + 1) * fc], preferred_element_type=F32)
        act = (gg * jax.nn.sigmoid(gg)) * uu
        acc = acc + jnp.dot(act.astype(BF16), wd_ref[f * fc:(f + 1) * fc, :], preferred_element_type=F32)
    out = x + gate2_ref[...] * acc
    if final:
        ms = jnp.mean(out * out, axis=-1, keepdims=True)
        out = out * lax.rsqrt(ms + EPS) * fg_ref[...]
    o_ref[...] = out


def _out_ffn(a, w_o, b_o, x, gate1, g, shift, scale, gate2, w_gu, w_down, final_g, rowmap, tm):
    B, L, D = x.shape
    final, bias = final_g is not None, b_o is not None
    row = pl.BlockSpec((None, tm, D), lambda b, i: (b, i, 0))
    vec = pl.BlockSpec((None, 1, D), lambda b, i: (rowmap(b), 0, 0))
    const = pl.BlockSpec((1, D), lambda b, i: (0, 0))
    in_specs = [row, _resident((D, D))] + ([const] if bias else [])
    args = [a, w_o] + ([b_o] if bias else [])
    in_specs += [row, vec, const, vec, vec, vec, _resident((D, 2 * D_FF)), _resident((D_FF, D))]
    args += [x, gate1, g, shift, scale, gate2, w_gu, w_down]
    if final:
        in_specs.append(const)
        args.append(final_g)
    return pl.pallas_call(
        functools.partial(_out_ffn_kernel, fc=256, final=final, bias=bias),
        grid=(B, L // tm),
        in_specs=in_specs,
        out_specs=row,
        out_shape=jax.ShapeDtypeStruct((B, L, D), F32),
        compiler_params=_params("parallel", "parallel"),
        name="out_ffn",
    )(*args)


def _filter_kernel(bands_ref, w1t_ref, w1c_ref, w1s_ref, b1_ref, w2_ref, b2_ref, fr_ref,
                   w3f_ref, w3b_ref, b3f_ref, b3b_ref, delta_ref, h_ref, hid_sc):
    L = hid_sc.shape[0]
    pos = lax.broadcasted_iota(jnp.int32, (L, LANES), 0).astype(F32)
    t = pos / max(L - 1, 1)

    @pl.when(pl.program_id(0) == 0)
    def _():
        ang = (2.0 * math.pi * pos / L) * bands_ref[...]
        fr = fr_ref[...]
        pre = (t * w1t_ref[...]
               + jnp.dot(jnp.cos(ang), w1c_ref[...], preferred_element_type=F32, precision=HI)
               - jnp.dot(jnp.sin(ang), w1s_ref[...], preferred_element_type=F32, precision=HI)
               + b1_ref[...])
        hid = jnp.sin(fr * pre)
        hid_sc[...] = jnp.sin(fr * (jnp.dot(hid, w2_ref[...], preferred_element_type=F32, precision=HI)
                                    + b2_ref[...]))

    hid = hid_sc[...]
    window = jnp.exp(-t[:, 0:1] * delta_ref[...])
    hf = (jnp.dot(hid, w3f_ref[...], preferred_element_type=F32, precision=HI) + b3f_ref[...]) * window
    hb = (jnp.dot(hid, w3b_ref[...], preferred_element_type=F32, precision=HI) + b3b_ref[...]) * window
    norm = (jnp.sum(jnp.abs(hf), axis=0, keepdims=True) + jnp.sum(jnp.abs(hb), axis=0, keepdims=True)
            + FILT_EPS)
    row = lax.broadcasted_iota(jnp.int32, hb.shape, 0)
    h_ref[0] = (hf / norm).astype(h_ref.dtype)
    h_ref[1] = jnp.where(row == 0, 0.0, hb / norm).astype(h_ref.dtype)


def _pad2(a, rows, cols):
    return jnp.pad(a, ((0, rows - a.shape[0]), (0, cols - a.shape[1])))


def _implicit_filter(L, w1, b1, w2, b2, w3, b3, freq, tc):
    D = D_MODEL
    bands = _pad2(jnp.linspace(1e-4, FILT_BANDS - 1, FILT_BANDS, dtype=F32)[None], 1, LANES)
    deltas = jnp.abs(jnp.linspace(math.log(FILT_TARGET) / FILT_SLOW_PCT,
                                  math.log(FILT_TARGET) / FILT_FAST_PCT, D, dtype=F32))[None]
    w1 = w1.astype(F32)
    args = [bands,
            _pad2(w1[0:1], 1, LANES),
            _pad2(w1[1:1 + FILT_BANDS], LANES, LANES),
            _pad2(w1[1 + FILT_BANDS:], LANES, LANES),
            _pad2(b1[None], 1, LANES), _pad2(w2, LANES, LANES), _pad2(b2[None], 1, LANES),
            _pad2(freq[None], 1, LANES),
            _pad2(w3, LANES, 2 * D), _pad2(w3, LANES, 2 * D), b3[None], b3[None], deltas]
    small = lambda shape: pl.BlockSpec(shape, lambda c: (0, 0))
    nb = D // tc
    in_specs = [small((1, LANES)), small((1, LANES)), small((LANES, LANES)), small((LANES, LANES)),
                small((1, LANES)), small((LANES, LANES)), small((1, LANES)), small((1, LANES)),
                pl.BlockSpec((LANES, tc), lambda c: (0, c)),
                pl.BlockSpec((LANES, tc), lambda c: (0, nb + c)),
                pl.BlockSpec((1, tc), lambda c: (0, c)),
                pl.BlockSpec((1, tc), lambda c: (0, nb + c)),
                pl.BlockSpec((1, tc), lambda c: (0, c))]
    return pl.pallas_call(
        _filter_kernel,
        grid=(nb,),
        in_specs=in_specs,
        out_specs=pl.BlockSpec((2, L, tc), lambda c: (0, 0, c)),
        out_shape=jax.ShapeDtypeStruct((2, L, D), BF16),
        scratch_shapes=[pltpu.VMEM((L, LANES), F32)],
        compiler_params=_params("arbitrary"),
        name="hyena_filter",
    )(*args)


def _dft_gen_kernel(fre_ref, fim_ref, gim_ref, cb_sc, sb_sc, *, L):
    n = 2 * L
    tr = fre_ref.shape[0]
    step = 2.0 * math.pi / n
    i = pl.program_id(0)
    row = lax.broadcasted_iota(jnp.int32, (tr, L), 0)
    col = lax.broadcasted_iota(jnp.int32, (tr, L), 1)

    @pl.when(i == 0)
    def _():
        ang = ((row * col) & (n - 1)).astype(F32) * step
        cb_sc[...] = jnp.cos(ang)
        sb_sc[...] = jnp.sin(ang)

    col1 = lax.broadcasted_iota(jnp.int32, (1, L), 1)
    a = (((i * tr) * col1) & (n - 1)).astype(F32) * step
    ca, sa = jnp.cos(a), jnp.sin(a)
    cb, sb = cb_sc[...], sb_sc[...]
    c = cb * ca - sb * sa
    s = -(sb * ca + cb * sa)
    grow = row + i * tr
    fre_ref[...] = c.astype(fre_ref.dtype)
    fim_ref[...] = jnp.where(grow == 0, jnp.where((col & 1) == 0, 1.0, -1.0), s).astype(fim_ref.dtype)
    gim_ref[...] = jnp.where(col == 0, jnp.where((grow & 1) == 0, 1.0, -1.0), s).astype(gim_ref.dtype)


def _dft_tables(L, tr):
    spec = pl.BlockSpec((tr, L), lambda i: (i, 0))
    shape = jax.ShapeDtypeStruct((L, L), BF16)
    return pl.pallas_call(
        functools.partial(_dft_gen_kernel, L=L),
        grid=(L // tr,),
        out_specs=[spec, spec, spec],
        out_shape=[shape, shape, shape],
        scratch_shapes=[pltpu.VMEM((tr, L), F32), pltpu.VMEM((tr, L), F32)],
        compiler_params=_params("arbitrary"),
        name="dft_tables",
    )()


def _fwd_dft_kernel(*refs, with_filter):
    if with_filter:
        fre_ref, fim_ref, v_ref, kr_ref, ki_ref, y_ref = refs
    else:
        fre_ref, fim_ref, v_ref, y_ref = refs
    v = v_ref[...]
    vr = jnp.dot(fre_ref[...], v, preferred_element_type=F32)
    vi = jnp.dot(fim_ref[...], v, preferred_element_type=F32)
    if with_filter:
        kr, ki = kr_ref[...], ki_ref[...]
        tm = vr.shape[0]
        first = (lax.broadcasted_iota(jnp.int32, vr.shape, 0) + pl.program_id(1) * tm) == 0
        yr = vr * kr - jnp.where(first, 0.0, vi * ki)
        yi = jnp.where(first, vi * ki, vr * ki + vi * kr)
        y_ref[0] = yr.astype(y_ref.dtype)
        y_ref[1] = yi.astype(y_ref.dtype)
    else:
        y_ref[0] = vr.astype(y_ref.dtype)
        y_ref[1] = vi.astype(y_ref.dtype)


def _fwd_dft(fre, fim, v, kf, tm, out_dtype):
    B, L, D = v.shape
    ftile = pl.BlockSpec((tm, L), lambda b, m: (m, 0))
    in_specs = [ftile, ftile, pl.BlockSpec((None, L, D), lambda b, m: (b, 0, 0))]
    args = [fre, fim, v]
    if kf is not None:
        in_specs += [pl.BlockSpec((None, tm, D), lambda b, m: (0, m, 0)),
                     pl.BlockSpec((None, tm, D), lambda b, m: (1, m, 0))]
        args += [kf, kf]
    return pl.pallas_call(
        functools.partial(_fwd_dft_kernel, with_filter=kf is not None),
        grid=(B, L // tm),
        in_specs=in_specs,
        out_specs=pl.BlockSpec((None, 2, tm, D), lambda b, m: (b, 0, m, 0)),
        out_shape=jax.ShapeDtypeStruct((B, 2, L, D), out_dtype),
        compiler_params=_params("parallel", "parallel"),
        name="hyena_fwd_dft",
    )(*args)


def _filter_spectrum_kernel(hf_ref, hb_ref, k_ref, *, n):
    hfr, hfi = hf_ref[0], hf_ref[1]
    hbr, hbi = hb_ref[0], hb_ref[1]
    first = (lax.broadcasted_iota(jnp.int32, hfr.shape, 0) + pl.program_id(0) * hfr.shape[0]) == 0
    wgt = jnp.where(first, 1.0 / n, 2.0 / n)
    k_ref[0] = (hfr + hbr) * wgt
    k_ref[1] = jnp.where(first, hfi + hbi, hfi - hbi) * wgt


def _filter_spectrum(hspec, tm):
    _, _, L, D = hspec.shape
    return pl.pallas_call(
        functools.partial(_filter_spectrum_kernel, n=2 * L),
        grid=(L // tm,),
        in_specs=[pl.BlockSpec((None, 2, tm, D), lambda m: (0, 0, m, 0)),
                  pl.BlockSpec((None, 2, tm, D), lambda m: (1, 0, m, 0))],
        out_specs=pl.BlockSpec((2, tm, D), lambda m: (0, m, 0)),
        out_shape=jax.ShapeDtypeStruct((2, L, D), F32),
        compiler_params=_params("parallel"),
        name="hyena_filter_spectrum",
    )(hspec, hspec)


def _inv_dft_kernel(gre_ref, gim_ref, yr_ref, yi_ref, vx_ref, x0_ref, skip_ref, z_ref, acc_sc):
    kk = pl.program_id(2)

    @pl.when(kk == 0)
    def _():
        acc_sc[...] = jnp.zeros(acc_sc.shape, F32)

    acc_sc[...] += (jnp.dot(gre_ref[...], yr_ref[...], preferred_element_type=F32)
                    + jnp.dot(gim_ref[...], yi_ref[...], preferred_element_type=F32))

    @pl.when(kk == pl.num_programs(2) - 1)
    def _():
        z = (acc_sc[...] + vx_ref[...].astype(F32) * skip_ref[...]) * x0_ref[...]
        z_ref[...] = z.astype(z_ref.dtype)


def _inv_dft(gre, gim, y, vx, x0, skip, tm, tk):
    B, L, D = vx.shape
    gtile = pl.BlockSpec((tm, tk), lambda b, i, k: (i, k))
    row = pl.BlockSpec((None, tm, D), lambda b, i, k: (b, i, 0))
    return pl.pallas_call(
        _inv_dft_kernel,
        grid=(B, L // tm, L // tk),
        in_specs=[gtile, gtile,
                  pl.BlockSpec((None, None, tk, D), lambda b, i, k: (b, 0, k, 0)),
                  pl.BlockSpec((None, None, tk, D), lambda b, i, k: (b, 1, k, 0)),
                  row, row, pl.BlockSpec((1, D), lambda b, i, k: (0, 0))],
        out_specs=row,
        out_shape=jax.ShapeDtypeStruct((B, L, D), BF16),
        scratch_shapes=[pltpu.VMEM((tm, D), F32)],
        compiler_params=_params("parallel", "parallel", "arbitrary"),
        name="hyena_inv_dft",
    )(gre, gim, y, y, vx, x0, skip)


FFT_N1 = 128
FFT_SLOT_GROUP = 8
FFT_ROWS = 16


def _cos_units(idx, n):
    return jnp.cos((idx & (n - 1)).astype(F32) * (2.0 * math.pi / n))


def _fft_step1_tables_kernel(fb_ref, gb_ref, *, n1, n2):
    n = n1 * n2
    h1, quarter, R = n1 // 2, n // 4, FFT_ROWS
    lr, lh = R.bit_length() - 1, h1.bit_length() - 1
    g = pl.program_id(0)

    def entry(part, slot, t1):
        idx = jnp.where((part == 1) & (slot == 0), (n // 2) * t1, n2 * t1 * slot + quarter * part)
        return _cos_units(idx, n)

    rows, cols = fb_ref.shape
    r = lax.broadcasted_iota(jnp.int32, (rows, cols), 0) + g * rows
    c = lax.broadcasted_iota(jnp.int32, (rows, cols), 1)
    val = entry(r >> (lr + lh), (r >> lr) & (h1 - 1), c >> lr)
    fb_ref[...] = jnp.where((r & (R - 1)) == (c & (R - 1)), val, 0.0).astype(fb_ref.dtype)
    rows, cols = gb_ref.shape
    r = lax.broadcasted_iota(jnp.int32, (rows, cols), 0) + g * rows
    c = lax.broadcasted_iota(jnp.int32, (rows, cols), 1)
    slot = (c >> lr) & (h1 - 1)
    val = jnp.where(slot == 0, 1.0, 2.0) * entry(c >> (lr + lh), slot, r >> lr)
    gb_ref[...] = jnp.where((r & (R - 1)) == (c & (R - 1)), val, 0.0).astype(gb_ref.dtype)


def _fft_step2_tables_kernel(m3_ref, m3i_ref, *, n1, n2):
    n = n1 * n2
    quarter, half = n // 4, n2 // 2
    g = pl.program_id(0)
    r = lax.broadcasted_iota(jnp.int32, (2 * n2, 2 * n2), 0)
    c = lax.broadcasted_iota(jnp.int32, (2 * n2, 2 * n2), 1)
    rj, rim = r & (n2 - 1), (r >= n2).astype(jnp.int32)
    cj, cim = c & (n2 - 1), (c >= n2).astype(jnp.int32)
    for si in range(m3_ref.shape[0]):
        s = g * m3_ref.shape[0] + si
        fwd = _cos_units(cj * (s + n1 * rj) + quarter * (rim - cim), n)
        inv = _cos_units(rj * (s + n1 * cj) + quarter * (cim - rim), n)
        if si == 0:
            def slot0(j, t2, t_im, out_im):
                lo = (j < half) & (t_im == 0)
                hi = (j >= half) & (t_im == 1)
                idx0 = jnp.where((out_im == 1) & (j == 0), (n // 2) * t2, t2 * n1 * j + quarter * out_im)
                idxh = t2 * (n1 // 2 + n1 * (j - half)) + quarter * out_im
                return lo, hi, jnp.where(lo, idx0, idxh)
            lo, hi, idx = slot0(rj, cj, cim, rim)
            fwd0 = jnp.where(lo | hi, _cos_units(idx, n), 0.0)
            lo, hi, idx = slot0(cj, rj, rim, cim)
            amp = jnp.where(lo & (cj == 0), 1.0, 2.0)
            inv0 = jnp.where(lo | hi, amp * _cos_units(idx, n), 0.0)
            fwd = jnp.where(g == 0, fwd0, fwd)
            inv = jnp.where(g == 0, inv0, inv)
        m3_ref[si] = fwd.astype(m3_ref.dtype)
        m3i_ref[si] = inv.astype(m3i_ref.dtype)


def _fft_tables(n1, n2):
    h1, sg, R = n1 // 2, FFT_SLOT_GROUP, FFT_ROWS
    steps = 8
    fb, gb = pl.pallas_call(
        functools.partial(_fft_step1_tables_kernel, n1=n1, n2=n2),
        grid=(steps,),
        out_specs=[pl.BlockSpec((2 * h1 * R // steps, h1 * R), lambda g: (g, 0)),
                   pl.BlockSpec((h1 * R // steps, 2 * h1 * R), lambda g: (g, 0))],
        out_shape=[jax.ShapeDtypeStruct((2 * h1 * R, h1 * R), BF16),
                   jax.ShapeDtypeStruct((h1 * R, 2 * h1 * R), BF16)],
        compiler_params=_params("parallel"),
        name="fft_step1_tables",
    )()
    mspec = pl.BlockSpec((sg, 2 * n2, 2 * n2), lambda g: (g, 0, 0))
    m3, m3i = pl.pallas_call(
        functools.partial(_fft_step2_tables_kernel, n1=n1, n2=n2),
        grid=(h1 // sg,),
        out_specs=[mspec, mspec],
        out_shape=[jax.ShapeDtypeStruct((h1, 2 * n2, 2 * n2), BF16)] * 2,
        compiler_params=_params("parallel"),
        name="fft_step2_tables",
    )()
    return fb, gb, m3, m3i


def _fft_step1_kernel(fb_ref, x_ref, a_ref):
    h1, R, D = x_ref.shape
    a = jnp.dot(fb_ref[...], x_ref[...].reshape(h1 * R, D), preferred_element_type=F32)
    a_ref[...] = a.reshape(2, h1, R, D).astype(a_ref.dtype)


def _fft_step1(fb, x):
    B, L, D = x.shape
    R = FFT_ROWS
    h1 = fb.shape[1] // R
    n2 = L // h1
    return pl.pallas_call(
        _fft_step1_kernel,
        grid=(B, n2 // R),
        in_specs=[pl.BlockSpec(fb.shape, lambda b, c: (0, 0)),
                  pl.BlockSpec((None, h1, R, D), lambda b, c: (b, 0, c, 0))],
        out_specs=pl.BlockSpec((None, 2, h1, R, D), lambda b, c: (b, 0, 0, c, 0)),
        out_shape=jax.ShapeDtypeStruct((B, 2, h1, n2, D), BF16),
        compiler_params=_params("parallel", "parallel"),
        name="fft_step1",
    )(fb, x.reshape(B, h1, n2, D))


def _first_bin(shape, g):
    return (lax.broadcasted_iota(jnp.int32, shape, 0) == 0) & (g == 0)


def _fft_filter_spectrum_kernel(m3_ref, a_ref, k_ref, *, n):
    n2 = a_ref.shape[-2]
    g = pl.program_id(0)
    for si in range(m3_ref.shape[0]):
        hf = jnp.dot(m3_ref[si], jnp.concatenate([a_ref[0, 0, si], a_ref[0, 1, si]], axis=0),
                     preferred_element_type=F32)
        hb = jnp.dot(m3_ref[si], jnp.concatenate([a_ref[1, 0, si], a_ref[1, 1, si]], axis=0),
                     preferred_element_type=F32)
        k_ref[0, si] = (hf[:n2] + hb[:n2]) * (1.0 / n)
        ki = hf[n2:] - hb[n2:]
        if si == 0:
            ki = jnp.where(_first_bin(ki.shape, g), hf[n2:] + hb[n2:], ki)
        k_ref[1, si] = ki * (1.0 / n)


def _fft_filter_spectrum(m3, a):
    h1, n2 = m3.shape[0], m3.shape[1] // 2
    D = a.shape[-1]
    sg = FFT_SLOT_GROUP
    return pl.pallas_call(
        functools.partial(_fft_filter_spectrum_kernel, n=2 * h1 * n2),
        grid=(h1 // sg,),
        in_specs=[pl.BlockSpec((sg, 2 * n2, 2 * n2), lambda g: (g, 0, 0)),
                  pl.BlockSpec((2, 2, sg, n2, D), lambda g: (0, 0, g, 0, 0))],
        out_specs=pl.BlockSpec((2, sg, n2, D), lambda g: (0, g, 0, 0)),
        out_shape=jax.ShapeDtypeStruct((2, h1, n2, D), F32),
        compiler_params=_params("parallel"),
        name="fft_filter_spectrum",
    )(m3, a)


def _fft_step2_kernel(m3_ref, m3i_ref, kr_ref, ki_ref, a_ref, b_ref):
    n2 = a_ref.shape[-2]
    g = pl.program_id(0)
    for si in range(m3_ref.shape[0]):
        xh = jnp.dot(m3_ref[si], jnp.concatenate([a_ref[0, si], a_ref[1, si]], axis=0),
                     preferred_element_type=F32)
        xr, xi = xh[:n2], xh[n2:]
        kr, ki = kr_ref[si], ki_ref[si]
        yr = xr * kr - xi * ki
        yi = xr * ki + xi * kr
        if si == 0:
            first = _first_bin(yr.shape, g)
            yr = jnp.where(first, xr * kr, yr)
            yi = jnp.where(first, xi * ki, yi)
        bb = jnp.dot(m3i_ref[si], jnp.concatenate([yr, yi], axis=0).astype(BF16),
                     preferred_element_type=F32)
        b_ref[0, si] = bb[:n2].astype(b_ref.dtype)
        b_ref[1, si] = bb[n2:].astype(b_ref.dtype)


def _fft_step2(m3, m3i, kf, a):
    B, _, h1, n2, D = a.shape
    sg = FFT_SLOT_GROUP
    mspec = pl.BlockSpec((sg, 2 * n2, 2 * n2), lambda g, b: (g, 0, 0))
    blk = pl.BlockSpec((None, 2, sg, n2, D), lambda g, b: (b, 0, g, 0, 0))
    return pl.pallas_call(
        _fft_step2_kernel,
        grid=(h1 // sg, B),
        in_specs=[mspec, mspec,
                  pl.BlockSpec((None, sg, n2, D), lambda g, b: (0, g, 0, 0)),
                  pl.BlockSpec((None, sg, n2, D), lambda g, b: (1, g, 0, 0)),
                  blk],
        out_specs=blk,
        out_shape=jax.ShapeDtypeStruct((B, 2, h1, n2, D), BF16),
        compiler_params=_params("parallel", "parallel"),
        name="fft_step2",
    )(m3, m3i, kf, kf, a)


def _fft_inv_step1_kernel(gb_ref, b_ref, vx_ref, x0_ref, skip_ref, z_ref):
    _, h1, R, D = b_ref.shape
    y = jnp.dot(gb_ref[...], b_ref[...].reshape(2 * h1 * R, D), preferred_element_type=F32)
    z = (y.reshape(h1, R, D) + vx_ref[...].astype(F32) * skip_ref[...]) * x0_ref[...]
    z_ref[...] = z.astype(z_ref.dtype)


def _fft_inv_step1(gb, bm, vx, x0, skip):
    B, L, D = vx.shape
    R = FFT_ROWS
    h1 = gb.shape[0] // R
    n2 = L // h1
    blk = pl.BlockSpec((None, h1, R, D), lambda b, c: (b, 0, c, 0))
    z = pl.pallas_call(
        _fft_inv_step1_kernel,
        grid=(B, n2 // R),
        in_specs=[pl.BlockSpec(gb.shape, lambda b, c: (0, 0)),
                  pl.BlockSpec((None, 2, h1, R, D), lambda b, c: (b, 0, 0, c, 0)),
                  blk, blk, pl.BlockSpec((1, D), lambda b, c: (0, 0))],
        out_specs=blk,
        out_shape=jax.ShapeDtypeStruct((B, h1, n2, D), BF16),
        compiler_params=_params("parallel", "parallel"),
        name="fft_inv_step1",
    )(gb, bm, vx.reshape(B, h1, n2, D), x0.reshape(B, h1, n2, D), skip)
    return z.reshape(B, L, D)


def _rope_tables(L):
    pos = jnp.arange(L, dtype=jnp.int32)
    rows = (pos // GRID_W).astype(F32)
    cols = (pos % GRID_W).astype(F32)
    inv = ROPE_BASE ** (-jnp.arange(ROPE_PAIRS, dtype=F32) / ROPE_PAIRS)
    lane = jnp.arange(LANES)
    within = lane % HEAD_DIM
    axis = within // (2 * ROPE_PAIRS)
    half = (within % (2 * ROPE_PAIRS)) // ROPE_PAIRS
    ang = jnp.where(axis[None, :] == 0, rows[:, None], cols[:, None]) * inv[within % ROPE_PAIRS][None, :]
    cos, sin = jnp.cos(ang), jnp.sin(ang)
    sin_a = jnp.where(half[None, :] == 0, -sin, 0.0)
    sin_b = jnp.where(half[None, :] == 1, sin, 0.0)
    return cos, sin_a, sin_b


def _tile(L, pref):
    return min(L, pref)


def _four_step(L):
    return (2 * L) // FFT_N1 >= 32


def _conv_tables(L):
    return _fft_tables(FFT_N1, 2 * L // FFT_N1) if _four_step(L) else _dft_tables(L, min(L, 256))


def _run_group(x, rowmap, mods, p, cache, rope_tabs, dft, kv_dtype):
    B, L, D = x.shape
    tm = _tile(L, 512)
    ctx_k, ctx_v = [], []
    for i in range(DEPTH):
        j = i // 2
        sh1, sc1, g1, sh2, sc2, g2 = (mods[i, :, s * D:(s + 1) * D].reshape(MOD_ROWS, 1, D) for s in range(6))
        n1 = p["norm1_g"][i][None]
        if i % 2 == 0:
            lam_init = 0.8 - 0.6 * math.exp(-0.3 * i)
            q, k, v = _qkv(x, n1, sh1, sc1, p["attn_w_qkv"][j], rowmap, rope_tabs, kv_dtype, tm)
            ctx_k.append(k)
            ctx_v.append(v)
            mix = _attention(q, k, v, None if cache is None else cache + (j,),
                             p["attn_lambda"][j].astype(F32), p["attn_subln_g"][j][None],
                             lam_init, _tile(L, 512), _tile(L, 512))
            w_o, b_o = p["attn_w_o"][j], None
        else:
            vx, x0 = _hyena_in(x, n1, sh1, sc1, p["hy_w_in"][j], p["hy_b_in"][j][None],
                               p["hy_conv_w"][j], p["hy_conv_b"][j][None], rowmap, tm)
            filt = _implicit_filter(L, p["filt_w1"][j], p["filt_b1"][j], p["filt_w2"][j], p["filt_b2"][j],
                                    p["filt_w3"][j], p["filt_b3"][j], p["filt_freq"][j], 256)
            skip = p["hy_skip"][j][None]
            if _four_step(L):
                fb, gb, m3, m3i = dft
                kf = _fft_filter_spectrum(m3, _fft_step1(fb, filt))
                bm = _fft_step2(m3, m3i, kf, _fft_step1(fb, vx))
                mix = _fft_inv_step1(gb, bm, vx, x0, skip)
            else:
                fre, fim, gim = dft
                tmf = _tile(L, 256)
                kf = _filter_spectrum(_fwd_dft(fre, fim, filt, None, tmf, F32), tmf)
                y = _fwd_dft(fre, fim, vx, kf, tmf, BF16)
                mix = _inv_dft(fre, gim, y, vx, x0, skip, tm, _tile(L, 512))
            w_o, b_o = p["hy_w_out"][j], p["hy_b_out"][j][None]
        final_g = p["final_g"][None] if i == DEPTH - 1 else None
        x = _out_ffn(mix, w_o, b_o, x, g1, p["norm2_g"][i][None], sh2, sc2, g2,
                     p["ffn_w_gu"][i], p["ffn_w_down"][i], final_g, rowmap, tm)
    return x, ctx_k, ctx_v


def kernel(x_prompt, x_sample, cache_k, cache_v, c, c_ctx, ada_w, ada_b, norm1_g, norm2_g, attn_w_qkv, attn_lambda, attn_subln_g, attn_w_o, hy_w_in, hy_b_in, hy_conv_w, hy_conv_b, filt_w1, filt_b1, filt_w2, filt_b2, filt_w3, filt_b3, filt_freq, hy_skip, hy_w_out, hy_b_out, ffn_w_gu, ffn_w_down, final_g):
    D = D_MODEL
    nb = c.shape[0]
    cmat = jnp.concatenate([c, c_ctx[None], jnp.zeros((MOD_ROWS - nb - 1, D), F32)], axis=0)
    mods = _adaln_all(cmat, ada_w, ada_b)

    p = dict(norm1_g=norm1_g, norm2_g=norm2_g, attn_lambda=attn_lambda, attn_subln_g=attn_subln_g,
             hy_b_in=hy_b_in, hy_conv_w=hy_conv_w, hy_conv_b=hy_conv_b,
             filt_w1=filt_w1, filt_b1=filt_b1, filt_w2=filt_w2, filt_b2=filt_b2, filt_w3=filt_w3,
             filt_b3=filt_b3, filt_freq=filt_freq, hy_skip=hy_skip, hy_b_out=hy_b_out, final_g=final_g,
             attn_w_qkv=attn_w_qkv.astype(BF16), attn_w_o=attn_w_o.astype(BF16),
             hy_w_in=hy_w_in.astype(BF16), hy_w_out=hy_w_out.astype(BF16),
             ffn_w_gu=ffn_w_gu.astype(BF16), ffn_w_down=ffn_w_down.astype(BF16))

    Lp, Ls = x_prompt.shape[1], x_sample.shape[1]
    y_prompt, ctx_k, ctx_v = _run_group(x_prompt, lambda b: CTX_ROW, mods, p, None, None,
                                        _conv_tables(Lp), F32)
    cache = tuple(a.reshape(a.shape[0], a.shape[1], a.shape[2], D) for a in (cache_k, cache_v))
    y_sample, _, _ = _run_group(x_sample, lambda b: b, mods, p, cache, _rope_tables(Ls),
                                _conv_tables(Ls), BF16)

    Bp = x_prompt.shape[0]
    new_k = jnp.stack(ctx_k, axis=1).reshape(Bp, len(ctx_k), Lp, N_HEADS, 2, HEAD_DIM)
    new_v = jnp.stack(ctx_v, axis=1).reshape(Bp, len(ctx_v), Lp, N_HEADS, V_DIM)
    return y_prompt, y_sample, new_k, new_v
```

```python
import functools
import math

import jax
import jax.numpy as jnp
from jax import lax
from jax.experimental import pallas as pl
from jax.experimental.pallas import tpu as pltpu

D_MODEL = 1024
DEPTH = 4
GRID_W = 64
N_HEADS = 8
HEAD_DIM = 64
V_DIM = 2 * HEAD_DIM
ROPE_PAIRS = HEAD_DIM // 4
ROPE_BASE = 10000.0
FILT_BANDS = 16
FILT_ORDER = 64
FILT_TARGET = 1e-2
FILT_FAST_PCT = 0.3
FILT_SLOW_PCT = 1.5
FILT_EPS = 1e-6
D_FF = 2816
EPS = 1e-6
SUBLN_EPS = 1e-5

LANES = 128
MOD_ROWS = 16
CTX_ROW = 8
VMEM_LIMIT = 56 * 1024 * 1024

F32 = jnp.float32
BF16 = jnp.bfloat16
HI = lax.Precision.HIGHEST


def _params(*sem):
    return pltpu.CompilerParams(dimension_semantics=sem, vmem_limit_bytes=VMEM_LIMIT)


def _lane_tile(x, reps):
    return x if reps == 1 else jnp.concatenate([x] * reps, axis=1)


def _norm_mod(x, g, shift, scale):
    ms = jnp.mean(x * x, axis=-1, keepdims=True)
    return (x * lax.rsqrt(ms + EPS) * g) * (1.0 + scale) + shift


def _mod_kernel(c_ref, w_ref, b_ref, o_ref):
    c = c_ref[...]
    a = c * jax.nn.sigmoid(c)
    o_ref[...] = jnp.dot(a, w_ref[...], preferred_element_type=F32, precision=HI) + b_ref[...]


def _adaln_all(cmat, ada_w, ada_b):
    n6 = 6 * D_MODEL
    tn = 1536
    return pl.pallas_call(
        _mod_kernel,
        grid=(DEPTH, n6 // tn),
        in_specs=[
            pl.BlockSpec((MOD_ROWS, D_MODEL), lambda i, n: (0, 0)),
            pl.BlockSpec((None, D_MODEL, tn), lambda i, n: (i, 0, n)),
            pl.BlockSpec((None, 1, tn), lambda i, n: (i, 0, n)),
        ],
        out_specs=pl.BlockSpec((None, MOD_ROWS, tn), lambda i, n: (i, 0, n)),
        out_shape=jax.ShapeDtypeStruct((DEPTH, MOD_ROWS, n6), F32),
        compiler_params=_params("parallel", "parallel"),
        name="adaln",
    )(cmat, ada_w, ada_b.reshape(DEPTH, 1, n6))


def _rope_chunk(x, cos, sin_a, sin_b):
    return x * cos + pltpu.roll(x, LANES - ROPE_PAIRS, 1) * sin_a + pltpu.roll(x, ROPE_PAIRS, 1) * sin_b


def _qkv_kernel(*refs, rope, tn):
    if rope:
        x_ref, g_ref, sh_ref, sc_ref, w_ref, cos_ref, sa_ref, sb_ref, q_ref, k_ref, v_ref = refs
    else:
        x_ref, g_ref, sh_ref, sc_ref, w_ref, q_ref, k_ref, v_ref = refs
    h = _norm_mod(x_ref[...], g_ref[...], sh_ref[...], sc_ref[...]).astype(BF16)
    outs = (q_ref, k_ref, v_ref)
    qk_scale = HEAD_DIM ** -0.5 * math.log2(math.e)
    for part in range(3):
        for n in range(D_MODEL // tn):
            col = part * D_MODEL + n * tn
            y = jnp.dot(h, w_ref[:, col:col + tn], preferred_element_type=F32)
            for j in range(tn // LANES):
                yj = y[:, j * LANES:(j + 1) * LANES]
                if rope and part < 2:
                    yj = _rope_chunk(yj, cos_ref[...], sa_ref[...], sb_ref[...])
                if part == 0:
                    yj = yj * qk_scale
                lo = n * tn + j * LANES
                outs[part][:, lo:lo + LANES] = yj.astype(outs[part].dtype)


def _qkv(x, g, shift, scale, w, rowmap, rope_tabs, kv_dtype, tm):
    B, L, D = x.shape
    rope = rope_tabs is not None
    row = pl.BlockSpec((None, tm, D), lambda b, i: (b, i, 0))
    vec = pl.BlockSpec((None, 1, D), lambda b, i: (rowmap(b), 0, 0))
    in_specs = [row, pl.BlockSpec((1, D), lambda b, i: (0, 0)), vec, vec,
                pl.BlockSpec((D, 3 * D), lambda b, i: (0, 0))]
    args = [x, g, shift, scale, w]
    if rope:
        tab = pl.BlockSpec((tm, LANES), lambda b, i: (i, 0))
        in_specs += [tab, tab, tab]
        args += list(rope_tabs)
    return pl.pallas_call(
        functools.partial(_qkv_kernel, rope=rope, tn=512),
        grid=(B, L // tm),
        in_specs=in_specs,
        out_specs=[row, row, row],
        out_shape=[jax.ShapeDtypeStruct((B, L, D), BF16),
                   jax.ShapeDtypeStruct((B, L, D), kv_dtype),
                   jax.ShapeDtypeStruct((B, L, D), kv_dtype)],
        compiler_params=_params("parallel", "parallel"),
        name="qkv",
    )(*args)


def _hyena_in_kernel(xp_ref, x_ref, xn_ref, g_ref, sh_ref, sc_ref, w_ref, b_ref, cw_ref, cb_ref,
                     vx_ref, x0_ref, u_sc, *, tn):
    tm, D = x_ref.shape
    i = pl.program_id(1)
    xa = jnp.concatenate([xp_ref[...], x_ref[...], xn_ref[...]], axis=0)
    h = _norm_mod(xa, g_ref[...], sh_ref[...], sc_ref[...]).astype(BF16)
    keep_top = (i > 0).astype(F32)
    keep_bot = (i < pl.num_programs(1) - 1).astype(F32)

    def part(p, n, slot):
        sl = slice(p * D + n * tn, p * D + (n + 1) * tn)
        u = jnp.dot(h, w_ref[:, sl], preferred_element_type=F32) + b_ref[:, sl]
        u_sc[slot, 0:8] = u[:8] * keep_top
        u_sc[slot, 8:8 + tm] = u[8:8 + tm]
        u_sc[slot, 8 + tm:] = u[8 + tm:] * keep_bot
        w = cw_ref[:, sl]
        return (u_sc[slot, 7:7 + tm] * w[0:1] + u_sc[slot, 8:8 + tm] * w[1:2] + u_sc[slot, 9:9 + tm] * w[2:3]
                + cb_ref[:, sl])

    for n in range(D // tn):
        out = slice(n * tn, (n + 1) * tn)
        x0_ref[:, out] = part(0, n, 0)
        vx_ref[:, out] = (part(2, n, 1) * part(1, n, 2)).astype(vx_ref.dtype)


def _hyena_in(x, g, shift, scale, w, b, conv_w, conv_b, rowmap, tm):
    B, L, D = x.shape
    N = w.shape[1]
    hb = tm // 8
    tn = 512
    vec = pl.BlockSpec((None, 1, D), lambda b_, i: (rowmap(b_), 0, 0))
    row = pl.BlockSpec((None, tm, D), lambda b_, i: (b_, i, 0))
    const = lambda r: pl.BlockSpec((r, N), lambda b_, i: (0, 0))
    return pl.pallas_call(
        functools.partial(_hyena_in_kernel, tn=tn),
        grid=(B, L // tm),
        in_specs=[pl.BlockSpec((None, 8, D), lambda b_, i: (b_, jnp.maximum(i * hb - 1, 0), 0)),
                  row,
                  pl.BlockSpec((None, 8, D), lambda b_, i: (b_, jnp.minimum((i + 1) * hb, L // 8 - 1), 0)),
                  pl.BlockSpec((1, D), lambda b_, i: (0, 0)), vec, vec,
                  pl.BlockSpec((D, N), lambda b_, i: (0, 0)), const(1), const(3), const(1)],
        out_specs=[row, row],
        out_shape=[jax.ShapeDtypeStruct((B, L, D), BF16), jax.ShapeDtypeStruct((B, L, D), F32)],
        scratch_shapes=[pltpu.VMEM((3, tm + 16, tn), F32)],
        compiler_params=_params("parallel", "parallel"),
        name="hyena_in",
    )(x, x, x, g, shift, scale, w, b, conv_w, conv_b)


def _attn_kernel(*refs, tq, tk, lam_init, cached):
    if cached:
        lam_ref, g_ref, q_ref, ck_ref, cv_ref, k_ref, v_ref, o_ref = refs[:8]
    else:
        lam_ref, g_ref, q_ref, k_ref, v_ref, o_ref = refs[:6]
    m_sc, acc_sc, vx_sc, q2_sc, s0_sc, s1_sc, s2_sc = refs[-7:]
    nc = ck_ref.shape[0] // tk if cached else 0
    nk = nc + k_ref.shape[0] // tk
    nq = q_ref.shape[0] // tq
    if cached:
        vx_sc[:nc * tk, :V_DIM] = cv_ref[...].astype(BF16)
    vx_sc[nc * tk:, :V_DIM] = v_ref[...].astype(BF16)
    vx_sc[:, V_DIM:] = jnp.ones((vx_sc.shape[0], V_DIM), BF16)
    lp = lam_ref[...]
    lam = (jnp.exp(jnp.sum(lp[0:1] * lp[1:2], axis=1, keepdims=True))
           - jnp.exp(jnp.sum(lp[2:3] * lp[3:4], axis=1, keepdims=True)) + lam_init)
    lane = lax.broadcasted_iota(jnp.int32, (tq, LANES), 1)

    def slot(j):
        if j == 0:
            return s0_sc
        return s1_sc if j % 2 == 1 else s2_sc

    def key_tile(j):
        if j < nc:
            return ck_ref[j * tk:(j + 1) * tk, :].astype(BF16)
        return k_ref[(j - nc) * tk:(j - nc + 1) * tk, :].astype(BF16)

    def load_q(i):
        q = q_ref[pl.ds(pl.multiple_of(i * tq, tq), tq), :]
        zero = jnp.zeros_like(q)
        q2_sc[:tq] = jnp.where(lane < HEAD_DIM, q, zero)
        q2_sc[tq:] = jnp.where(lane >= HEAD_DIM, q, zero)

    def scores(j):
        s_sc = slot(j)
        s_sc[...] = lax.dot_general(q2_sc[...], key_tile(j), (((1,), (1,)), ((), ())),
                                    preferred_element_type=F32)

    def accumulate(j):
        s_sc = slot(j)
        t = jnp.broadcast_to(jnp.max(s_sc[...], axis=1, keepdims=True), m_sc.shape)
        if j == 0:
            m_new = t
        else:
            m_prev = m_sc[...]
            m_new = jnp.maximum(m_prev, t)
            alpha = jnp.exp2(m_prev - m_new)
        e = jnp.exp2(s_sc[...] - _lane_tile(m_new, tk // LANES))
        pv = jnp.dot(e.astype(BF16), vx_sc[j * tk:(j + 1) * tk, :], preferred_element_type=F32)
        acc_sc[...] = pv if j == 0 else _lane_tile(alpha, 2) * acc_sc[...] + pv
        m_sc[...] = m_new

    def finalize(i):
        o = (acc_sc[:tq, :V_DIM] / acc_sc[:tq, V_DIM:] - lam * (acc_sc[tq:, :V_DIM] / acc_sc[tq:, V_DIM:]))
        ms = jnp.mean(o * o, axis=-1, keepdims=True)
        y = (o * lax.rsqrt(ms + SUBLN_EPS) * g_ref[...]) * (1.0 - lam_init)
        o_ref[pl.ds(pl.multiple_of(i * tq, tq), tq), :] = y.astype(o_ref.dtype)

    def query_tile(i, prefetch):
        for j in range(1, nk):
            accumulate(j - 1)
            scores(j)
        if prefetch:
            load_q(i + 1)
        accumulate(nk - 1)
        if prefetch:
            scores(0)
        finalize(i)

    load_q(0)
    scores(0)

    def body(i, carry):
        query_tile(i, True)
        return carry
    lax.fori_loop(0, nq - 1, body, 0)
    query_tile(nq - 1, False)


def _attention(q, k, v, cache, lam_params, subln_g, lam_init, tq, tk):
    B, Lq, D = q.shape
    Lk = k.shape[1]
    kv = pl.BlockSpec((None, Lk, LANES), lambda b, h: (b, 0, h))
    qo = pl.BlockSpec((None, Lq, LANES), lambda b, h: (b, 0, h))
    in_specs = [pl.BlockSpec((4, HEAD_DIM), lambda b, h: (0, 0)),
                pl.BlockSpec((1, V_DIM), lambda b, h: (0, 0)), qo]
    args = [lam_params, subln_g, q]
    past = 0
    if cache is not None:
        ck, cv, layer = cache
        past = ck.shape[2]
        cspec = pl.BlockSpec((None, None, past, LANES), lambda b, h: (b, layer, 0, h))
        in_specs += [cspec, cspec]
        args += [ck, cv]
    in_specs += [kv, kv]
    args += [k, v]
    return pl.pallas_call(
        functools.partial(_attn_kernel, tq=tq, tk=tk, lam_init=lam_init, cached=cache is not None),
        grid=(B, N_HEADS),
        in_specs=in_specs,
        out_specs=qo,
        out_shape=jax.ShapeDtypeStruct((B, Lq, D), BF16),
        scratch_shapes=[pltpu.VMEM((2 * tq, LANES), F32),
                        pltpu.VMEM((2 * tq, 2 * V_DIM), F32),
                        pltpu.VMEM((past + Lk, 2 * V_DIM), BF16),
                        pltpu.VMEM((2 * tq, LANES), BF16)]
                       + [pltpu.VMEM((2 * tq, tk), F32)] * 3,
        compiler_params=_params("parallel", "parallel"),
        name="diff_attention",
    )(*args)


def _resident(shape):
    return pl.BlockSpec(shape, lambda b, i: (0,) * len(shape), pipeline_mode=pl.Buffered(1))


def _out_ffn_kernel(*refs, fc, final, bias):
    refs = list(refs)
    o_ref = refs.pop()
    fg_ref = refs.pop() if final else None
    a_ref, wo_ref = refs[:2]
    bo_ref = refs[2] if bias else None
    x_ref, gate1_ref, g_ref, sh_ref, sc_ref, gate2_ref, wgu_ref, wd_ref = refs[2 + bias:]
    y = jnp.dot(a_ref[...], wo_ref[...], preferred_element_type=F32)
    if bias:
        y = y + bo_ref[...]
    x = x_ref[...] + gate1_ref[...] * y
    h = _norm_mod(x, g_ref[...], sh_ref[...], sc_ref[...]).astype(BF16)
    acc = jnp.zeros(x.shape, F32)
    for f in range(D_FF // fc):
        gg = jnp.dot(h, wgu_ref[:, f * fc:(f + 1) * fc], preferred_element_type=F32)
        uu = jnp.dot(h, wgu_ref[:, D_FF + f * fc:D_FF + (f + 1) * fc], preferred_element_type=F32)
        act = (gg * jax.nn.sigmoid(gg)) * uu
        acc = acc + jnp.dot(act.astype(BF16), wd_ref[f * fc:(f + 1) * fc, :], preferred_element_type=F32)
    out = x + gate2_ref[...] * acc
    if final:
        ms = jnp.mean(out * out, axis=-1, keepdims=True)
        out = out * lax.rsqrt(ms + EPS) * fg_ref[...]
    o_ref[...] = out


def _out_ffn(a, w_o, b_o, x, gate1, g, shift, scale, gate2, w_gu, w_down, final_g, rowmap, tm):
    B, L, D = x.shape
    final, bias = final_g is not None, b_o is not None
    row = pl.BlockSpec((None, tm, D), lambda b, i: (b, i, 0))
    vec = pl.BlockSpec((None, 1, D), lambda b, i: (rowmap(b), 0, 0))
    const = pl.BlockSpec((1, D), lambda b, i: (0, 0))
    in_specs = [row, _resident((D, D))] + ([const] if bias else [])
    args = [a, w_o] + ([b_o] if bias else [])
    in_specs += [row, vec, const, vec, vec, vec, _resident((D, 2 * D_FF)), _resident((D_FF, D))]
    args += [x, gate1, g, shift, scale, gate2, w_gu, w_down]
    if final:
        in_specs.append(const)
        args.append(final_g)
    return pl.pallas_call(
        functools.partial(_out_ffn_kernel, fc=256, final=final, bias=bias),
        grid=(B, L // tm),
        in_specs=in_specs,
        out_specs=row,
        out_shape=jax.ShapeDtypeStruct((B, L, D), F32),
        compiler_params=_params("parallel", "parallel"),
        name="out_ffn",
    )(*args)


def _filter_kernel(bands_ref, w1t_ref, w1c_ref, w1s_ref, b1_ref, w2_ref, b2_ref, fr_ref,
                   w3f_ref, w3b_ref, b3f_ref, b3b_ref, delta_ref, h_ref, hid_sc):
    L = hid_sc.shape[0]
    pos = lax.broadcasted_iota(jnp.int32, (L, LANES), 0).astype(F32)
    t = pos / max(L - 1, 1)

    @pl.when(pl.program_id(0) == 0)
    def _():
        ang = (2.0 * math.pi * pos / L) * bands_ref[...]
        fr = fr_ref[...]
        pre = (t * w1t_ref[...]
               + jnp.dot(jnp.cos(ang), w1c_ref[...], preferred_element_type=F32, precision=HI)
               - jnp.dot(jnp.sin(ang), w1s_ref[...], preferred_element_type=F32, precision=HI)
               + b1_ref[...])
        hid = jnp.sin(fr * pre)
        hid_sc[...] = jnp.sin(fr * (jnp.dot(hid, w2_ref[...], preferred_element_type=F32, precision=HI)
                                    + b2_ref[...]))

    hid = hid_sc[...]
    window = jnp.exp(-t[:, 0:1] * delta_ref[...])
    hf = (jnp.dot(hid, w3f_ref[...], preferred_element_type=F32, precision=HI) + b3f_ref[...]) * window
    hb = (jnp.dot(hid, w3b_ref[...], preferred_element_type=F32, precision=HI) + b3b_ref[...]) * window
    norm = (jnp.sum(jnp.abs(hf), axis=0, keepdims=True) + jnp.sum(jnp.abs(hb), axis=0, keepdims=True)
            + FILT_EPS)
    row = lax.broadcasted_iota(jnp.int32, hb.shape, 0)
    h_ref[0] = (hf / norm).astype(h_ref.dtype)
    h_ref[1] = jnp.where(row == 0, 0.0, hb / norm).astype(h_ref.dtype)


def _pad2(a, rows, cols):
    return jnp.pad(a, ((0, rows - a.shape[0]), (0, cols - a.shape[1])))


def _implicit_filter(L, w1, b1, w2, b2, w3, b3, freq, tc):
    D = D_MODEL
    bands = _pad2(jnp.linspace(1e-4, FILT_BANDS - 1, FILT_BANDS, dtype=F32)[None], 1, LANES)
    deltas = jnp.abs(jnp.linspace(math.log(FILT_TARGET) / FILT_SLOW_PCT,
                                  math.log(FILT_TARGET) / FILT_FAST_PCT, D, dtype=F32))[None]
    w1 = w1.astype(F32)
    args = [bands,
            _pad2(w1[0:1], 1, LANES),
            _pad2(w1[1:1 + FILT_BANDS], LANES, LANES),
            _pad2(w1[1 + FILT_BANDS:], LANES, LANES),
            _pad2(b1[None], 1, LANES), _pad2(w2, LANES, LANES), _pad2(b2[None], 1, LANES),
            _pad2(freq[None], 1, LANES),
            _pad2(w3, LANES, 2 * D), _pad2(w3, LANES, 2 * D), b3[None], b3[None], deltas]
    small = lambda shape: pl.BlockSpec(shape, lambda c: (0, 0))
    nb = D // tc
    in_specs = [small((1, LANES)), small((1, LANES)), small((LANES, LANES)), small((LANES, LANES)),
                small((1, LANES)), small((LANES, LANES)), small((1, LANES)), small((1, LANES)),
                pl.BlockSpec((LANES, tc), lambda c: (0, c)),
                pl.BlockSpec((LANES, tc), lambda c: (0, nb + c)),
                pl.BlockSpec((1, tc), lambda c: (0, c)),
                pl.BlockSpec((1, tc), lambda c: (0, nb + c)),
                pl.BlockSpec((1, tc), lambda c: (0, c))]
    return pl.pallas_call(
        _filter_kernel,
        grid=(nb,),
        in_specs=in_specs,
        out_specs=pl.BlockSpec((2, L, tc), lambda c: (0, 0, c)),
        out_shape=jax.ShapeDtypeStruct((2, L, D), BF16),
        scratch_shapes=[pltpu.VMEM((L, LANES), F32)],
        compiler_params=_params("arbitrary"),
        name="hyena_filter",
    )(*args)


def _dft_gen_kernel(fre_ref, fim_ref, gim_ref, cb_sc, sb_sc, *, L):
    n = 2 * L
    tr = fre_ref.shape[0]
    step = 2.0 * math.pi / n
    i = pl.program_id(0)
    row = lax.broadcasted_iota(jnp.int32, (tr, L), 0)
    col = lax.broadcasted_iota(jnp.int32, (tr, L), 1)

    @pl.when(i == 0)
    def _():
        ang = ((row * col) & (n - 1)).astype(F32) * step
        cb_sc[...] = jnp.cos(ang)
        sb_sc[...] = jnp.sin(ang)

    col1 = lax.broadcasted_iota(jnp.int32, (1, L), 1)
    a = (((i * tr) * col1) & (n - 1)).astype(F32) * step
    ca, sa = jnp.cos(a), jnp.sin(a)
    cb, sb = cb_sc[...], sb_sc[...]
    c = cb * ca - sb * sa
    s = -(sb * ca + cb * sa)
    grow = row + i * tr
    fre_ref[...] = c.astype(fre_ref.dtype)
    fim_ref[...] = jnp.where(grow == 0, jnp.where((col & 1) == 0, 1.0, -1.0), s).astype(fim_ref.dtype)
    gim_ref[...] = jnp.where(col == 0, jnp.where((grow & 1) == 0, 1.0, -1.0), s).astype(gim_ref.dtype)


def _dft_tables(L, tr):
    spec = pl.BlockSpec((tr, L), lambda i: (i, 0))
    shape = jax.ShapeDtypeStruct((L, L), BF16)
    return pl.pallas_call(
        functools.partial(_dft_gen_kernel, L=L),
        grid=(L // tr,),
        out_specs=[spec, spec, spec],
        out_shape=[shape, shape, shape],
        scratch_shapes=[pltpu.VMEM((tr, L), F32), pltpu.VMEM((tr, L), F32)],
        compiler_params=_params("arbitrary"),
        name="dft_tables",
    )()


def _fwd_dft_kernel(*refs, with_filter):
    if with_filter:
        fre_ref, fim_ref, v_ref, kr_ref, ki_ref, y_ref = refs
    else:
        fre_ref, fim_ref, v_ref, y_ref = refs
    v = v_ref[...]
    vr = jnp.dot(fre_ref[...], v, preferred_element_type=F32)
    vi = jnp.dot(fim_ref[...], v, preferred_element_type=F32)
    if with_filter:
        kr, ki = kr_ref[...], ki_ref[...]
        tm = vr.shape[0]
        first = (lax.broadcasted_iota(jnp.int32, vr.shape, 0) + pl.program_id(1) * tm) == 0
        yr = vr * kr - jnp.where(first, 0.0, vi * ki)
        yi = jnp.where(first, vi * ki, vr * ki + vi * kr)
        y_ref[0] = yr.astype(y_ref.dtype)
        y_ref[1] = yi.astype(y_ref.dtype)
    else:
        y_ref[0] = vr.astype(y_ref.dtype)
        y_ref[1] = vi.astype(y_ref.dtype)


def _fwd_dft(fre, fim, v, kf, tm, out_dtype):
    B, L, D = v.shape
    ftile = pl.BlockSpec((tm, L), lambda b, m: (m, 0))
    in_specs = [ftile, ftile, pl.BlockSpec((None, L, D), lambda b, m: (b, 0, 0))]
    args = [fre, fim, v]
    if kf is not None:
        in_specs += [pl.BlockSpec((None, tm, D), lambda b, m: (0, m, 0)),
                     pl.BlockSpec((None, tm, D), lambda b, m: (1, m, 0))]
        args += [kf, kf]
    return pl.pallas_call(
        functools.partial(_fwd_dft_kernel, with_filter=kf is not None),
        grid=(B, L // tm),
        in_specs=in_specs,
        out_specs=pl.BlockSpec((None, 2, tm, D), lambda b, m: (b, 0, m, 0)),
        out_shape=jax.ShapeDtypeStruct((B, 2, L, D), out_dtype),
        compiler_params=_params("parallel", "parallel"),
        name="hyena_fwd_dft",
    )(*args)


def _filter_spectrum_kernel(hf_ref, hb_ref, k_ref, *, n):
    hfr, hfi = hf_ref[0], hf_ref[1]
    hbr, hbi = hb_ref[0], hb_ref[1]
    first = (lax.broadcasted_iota(jnp.int32, hfr.shape, 0) + pl.program_id(0) * hfr.shape[0]) == 0
    wgt = jnp.where(first, 1.0 / n, 2.0 / n)
    k_ref[0] = (hfr + hbr) * wgt
    k_ref[1] = jnp.where(first, hfi + hbi, hfi - hbi) * wgt


def _filter_spectrum(hspec, tm):
    _, _, L, D = hspec.shape
    return pl.pallas_call(
        functools.partial(_filter_spectrum_kernel, n=2 * L),
        grid=(L // tm,),
        in_specs=[pl.BlockSpec((None, 2, tm, D), lambda m: (0, 0, m, 0)),
                  pl.BlockSpec((None, 2, tm, D), lambda m: (1, 0, m, 0))],
        out_specs=pl.BlockSpec((2, tm, D), lambda m: (0, m, 0)),
        out_shape=jax.ShapeDtypeStruct((2, L, D), F32),
        compiler_params=_params("parallel"),
        name="hyena_filter_spectrum",
    )(hspec, hspec)


def _inv_dft_kernel(gre_ref, gim_ref, yr_ref, yi_ref, vx_ref, x0_ref, skip_ref, z_ref, acc_sc):
    kk = pl.program_id(2)

    @pl.when(kk == 0)
    def _():
        acc_sc[...] = jnp.zeros(acc_sc.shape, F32)

    acc_sc[...] += (jnp.dot(gre_ref[...], yr_ref[...], preferred_element_type=F32)
                    + jnp.dot(gim_ref[...], yi_ref[...], preferred_element_type=F32))

    @pl.when(kk == pl.num_programs(2) - 1)
    def _():
        z = (acc_sc[...] + vx_ref[...].astype(F32) * skip_ref[...]) * x0_ref[...]
        z_ref[...] = z.astype(z_ref.dtype)


def _inv_dft(gre, gim, y, vx, x0, skip, tm, tk):
    B, L, D = vx.shape
    gtile = pl.BlockSpec((tm, tk), lambda b, i, k: (i, k))
    row = pl.BlockSpec((None, tm, D), lambda b, i, k: (b, i, 0))
    return pl.pallas_call(
        _inv_dft_kernel,
        grid=(B, L // tm, L // tk),
        in_specs=[gtile, gtile,
                  pl.BlockSpec((None, None, tk, D), lambda b, i, k: (b, 0, k, 0)),
                  pl.BlockSpec((None, None, tk, D), lambda b, i, k: (b, 1, k, 0)),
                  row, row, pl.BlockSpec((1, D), lambda b, i, k: (0, 0))],
        out_specs=row,
        out_shape=jax.ShapeDtypeStruct((B, L, D), BF16),
        scratch_shapes=[pltpu.VMEM((tm, D), F32)],
        compiler_params=_params("parallel", "parallel", "arbitrary"),
        name="hyena_inv_dft",
    )(gre, gim, y, y, vx, x0, skip)


FFT_N1 = 128
FFT_SLOT_GROUP = 8
FFT_ROWS = 16


def _cos_units(idx, n):
    return jnp.cos((idx & (n - 1)).astype(F32) * (2.0 * math.pi / n))


def _fft_step1_tables_kernel(fb_ref, gb_ref, *, n1, n2):
    n = n1 * n2
    h1, quarter, R = n1 // 2, n // 4, FFT_ROWS
    lr, lh = R.bit_length() - 1, h1.bit_length() - 1
    g = pl.program_id(0)

    def entry(part, slot, t1):
        idx = jnp.where((part == 1) & (slot == 0), (n // 2) * t1, n2 * t1 * slot + quarter * part)
        return _cos_units(idx, n)

    rows, cols = fb_ref.shape
    r = lax.broadcasted_iota(jnp.int32, (rows, cols), 0) + g * rows
    c = lax.broadcasted_iota(jnp.int32, (rows, cols), 1)
    val = entry(r >> (lr + lh), (r >> lr) & (h1 - 1), c >> lr)
    fb_ref[...] = jnp.where((r & (R - 1)) == (c & (R - 1)), val, 0.0).astype(fb_ref.dtype)
    rows, cols = gb_ref.shape
    r = lax.broadcasted_iota(jnp.int32, (rows, cols), 0) + g * rows
    c = lax.broadcasted_iota(jnp.int32, (rows, cols), 1)
    slot = (c >> lr) & (h1 - 1)
    val = jnp.where(slot == 0, 1.0, 2.0) * entry(c >> (lr + lh), slot, r >> lr)
    gb_ref[...] = jnp.where((r & (R - 1)) == (c & (R - 1)), val, 0.0).astype(gb_ref.dtype)


def _fft_step2_tables_kernel(m3_ref, m3i_ref, *, n1, n2):
    n = n1 * n2
    quarter, half = n // 4, n2 // 2
    g = pl.program_id(0)
    r = lax.broadcasted_iota(jnp.int32, (2 * n2, 2 * n2), 0)
    c = lax.broadcasted_iota(jnp.int32, (2 * n2, 2 * n2), 1)
    rj, rim = r & (n2 - 1), (r >= n2).astype(jnp.int32)
    cj, cim = c & (n2 - 1), (c >= n2).astype(jnp.int32)
    for si in range(m3_ref.shape[0]):
        s = g * m3_ref.shape[0] + si
        fwd = _cos_units(cj * (s + n1 * rj) + quarter * (rim - cim), n)
        inv = _cos_units(rj * (s + n1 * cj) + quarter * (cim - rim), n)
        if si == 0:
            def slot0(j, t2, t_im, out_im):
                lo = (j < half) & (t_im == 0)
                hi = (j >= half) & (t_im == 1)
                idx0 = jnp.where((out_im == 1) & (j == 0), (n // 2) * t2, t2 * n1 * j + quarter * out_im)
                idxh = t2 * (n1 // 2 + n1 * (j - half)) + quarter * out_im
                return lo, hi, jnp.where(lo, idx0, idxh)
            lo, hi, idx = slot0(rj, cj, cim, rim)
            fwd0 = jnp.where(lo | hi, _cos_units(idx, n), 0.0)
            lo, hi, idx = slot0(cj, rj, rim, cim)
            amp = jnp.where(lo & (cj == 0), 1.0, 2.0)
            inv0 = jnp.where(lo | hi, amp * _cos_units(idx, n), 0.0)
            fwd = jnp.where(g == 0, fwd0, fwd)
            inv = jnp.where(g == 0, inv0, inv)
        m3_ref[si] = fwd.astype(m3_ref.dtype)
        m3i_ref[si] = inv.astype(m3i_ref.dtype)


def _fft_tables(n1, n2):
    h1, sg, R = n1 // 2, FFT_SLOT_GROUP, FFT_ROWS
    steps = 8
    fb, gb = pl.pallas_call(
        functools.partial(_fft_step1_tables_kernel, n1=n1, n2=n2),
        grid=(steps,),
        out_specs=[pl.BlockSpec((2 * h1 * R // steps, h1 * R), lambda g: (g, 0)),
                   pl.BlockSpec((h1 * R // steps, 2 * h1 * R), lambda g: (g, 0))],
        out_shape=[jax.ShapeDtypeStruct((2 * h1 * R, h1 * R), BF16),
                   jax.ShapeDtypeStruct((h1 * R, 2 * h1 * R), BF16)],
        compiler_params=_params("parallel"),
        name="fft_step1_tables",
    )()
    mspec = pl.BlockSpec((sg, 2 * n2, 2 * n2), lambda g: (g, 0, 0))
    m3, m3i = pl.pallas_call(
        functools.partial(_fft_step2_tables_kernel, n1=n1, n2=n2),
        grid=(h1 // sg,),
        out_specs=[mspec, mspec],
        out_shape=[jax.ShapeDtypeStruct((h1, 2 * n2, 2 * n2), BF16)] * 2,
        compiler_params=_params("parallel"),
        name="fft_step2_tables",
    )()
    return fb, gb, m3, m3i


def _fft_step1_kernel(fb_ref, x_ref, a_ref):
    h1, R, D = x_ref.shape
    a = jnp.dot(fb_ref[...], x_ref[...].reshape(h1 * R, D), preferred_element_type=F32)
    a_ref[...] = a.reshape(2, h1, R, D).astype(a_ref.dtype)


def _fft_step1(fb, x):
    B, L, D = x.shape
    R = FFT_ROWS
    h1 = fb.shape[1] // R
    n2 = L // h1
    return pl.pallas_call(
        _fft_step1_kernel,
        grid=(B, n2 // R),
        in_specs=[pl.BlockSpec(fb.shape, lambda b, c: (0, 0)),
                  pl.BlockSpec((None, h1, R, D), lambda b, c: (b, 0, c, 0))],
        out_specs=pl.BlockSpec((None, 2, h1, R, D), lambda b, c: (b, 0, 0, c, 0)),
        out_shape=jax.ShapeDtypeStruct((B, 2, h1, n2, D), BF16),
        compiler_params=_params("parallel", "parallel"),
        name="fft_step1",
    )(fb, x.reshape(B, h1, n2, D))


def _first_bin(shape, g):
    return (lax.broadcasted_iota(jnp.int32, shape, 0) == 0) & (g == 0)


def _fft_filter_spectrum_kernel(m3_ref, a_ref, k_ref, *, n):
    n2 = a_ref.shape[-2]
    g = pl.program_id(0)
    for si in range(m3_ref.shape[0]):
        hf = jnp.dot(m3_ref[si], jnp.concatenate([a_ref[0, 0, si], a_ref[0, 1, si]], axis=0),
                     preferred_element_type=F32)
        hb = jnp.dot(m3_ref[si], jnp.concatenate([a_ref[1, 0, si], a_ref[1, 1, si]], axis=0),
                     preferred_element_type=F32)
        k_ref[0, si] = (hf[:n2] + hb[:n2]) * (1.0 / n)
        ki = hf[n2:] - hb[n2:]
        if si == 0:
            ki = jnp.where(_first_bin(ki.shape, g), hf[n2:] + hb[n2:], ki)
        k_ref[1, si] = ki * (1.0 / n)


def _fft_filter_spectrum(m3, a):
    h1, n2 = m3.shape[0], m3.shape[1] // 2
    D = a.shape[-1]
    sg = FFT_SLOT_GROUP
    return pl.pallas_call(
        functools.partial(_fft_filter_spectrum_kernel, n=2 * h1 * n2),
        grid=(h1 // sg,),
        in_specs=[pl.BlockSpec((sg, 2 * n2, 2 * n2), lambda g: (g, 0, 0)),
                  pl.BlockSpec((2, 2, sg, n2, D), lambda g: (0, 0, g, 0, 0))],
        out_specs=pl.BlockSpec((2, sg, n2, D), lambda g: (0, g, 0, 0)),
        out_shape=jax.ShapeDtypeStruct((2, h1, n2, D), F32),
        compiler_params=_params("parallel"),
        name="fft_filter_spectrum",
    )(m3, a)


def _fft_step2_kernel(m3_ref, m3i_ref, kr_ref, ki_ref, a_ref, b_ref):
    n2 = a_ref.shape[-2]
    g = pl.program_id(0)
    for si in range(m3_ref.shape[0]):
        xh = jnp.dot(m3_ref[si], jnp.concatenate([a_ref[0, si], a_ref[1, si]], axis=0),
                     preferred_element_type=F32)
        xr, xi = xh[:n2], xh[n2:]
        kr, ki = kr_ref[si], ki_ref[si]
        yr = xr * kr - xi * ki
        yi = xr * ki + xi * kr
        if si == 0:
            first = _first_bin(yr.shape, g)
            yr = jnp.where(first, xr * kr, yr)
            yi = jnp.where(first, xi * ki, yi)
        bb = jnp.dot(m3i_ref[si], jnp.concatenate([yr, yi], axis=0).astype(BF16),
                     preferred_element_type=F32)
        b_ref[0, si] = bb[:n2].astype(b_ref.dtype)
        b_ref[1, si] = bb[n2:].astype(b_ref.dtype)


def _fft_step2(m3, m3i, kf, a):
    B, _, h1, n2, D = a.shape
    sg = FFT_SLOT_GROUP
    mspec = pl.BlockSpec((sg, 2 * n2, 2 * n2), lambda g, b: (g, 0, 0))
    blk = pl.BlockSpec((None, 2, sg, n2, D), lambda g, b: (b, 0, g, 0, 0))
    return pl.pallas_call(
        _fft_step2_kernel,
        grid=(h1 // sg, B),
        in_specs=[mspec, mspec,
                  pl.BlockSpec((None, sg, n2, D), lambda g, b: (0, g, 0, 0)),
                  pl.BlockSpec((None, sg, n2, D), lambda g, b: (1, g, 0, 0)),
                  blk],
        out_specs=blk,
        out_shape=jax.ShapeDtypeStruct((B, 2, h1, n2, D), BF16),
        compiler_params=_params("parallel", "parallel"),
        name="fft_step2",
    )(m3, m3i, kf, kf, a)


def _fft_inv_step1_kernel(gb_ref, b_ref, vx_ref, x0_ref, skip_ref, z_ref):
    _, h1, R, D = b_ref.shape
    y = jnp.dot(gb_ref[...], b_ref[...].reshape(2 * h1 * R, D), preferred_element_type=F32)
    z = (y.reshape(h1, R, D) + vx_ref[...].astype(F32) * skip_ref[...]) * x0_ref[...]
    z_ref[...] = z.astype(z_ref.dtype)


def _fft_inv_step1(gb, bm, vx, x0, skip):
    B, L, D = vx.shape
    R = FFT_ROWS
    h1 = gb.shape[0] // R
    n2 = L // h1
    blk = pl.BlockSpec((None, h1, R, D), lambda b, c: (b, 0, c, 0))
    z = pl.pallas_call(
        _fft_inv_step1_kernel,
        grid=(B, n2 // R),
        in_specs=[pl.BlockSpec(gb.shape, lambda b, c: (0, 0)),
                  pl.BlockSpec((None, 2, h1, R, D), lambda b, c: (b, 0, 0, c, 0)),
                  blk, blk, pl.BlockSpec((1, D), lambda b, c: (0, 0))],
        out_specs=blk,
        out_shape=jax.ShapeDtypeStruct((B, h1, n2, D), BF16),
        compiler_params=_params("parallel", "parallel"),
        name="fft_inv_step1",
    )(gb, bm, vx.reshape(B, h1, n2, D), x0.reshape(B, h1, n2, D), skip)
    return z.reshape(B, L, D)


def _rope_tables(L):
    pos = jnp.arange(L, dtype=jnp.int32)
    rows = (pos // GRID_W).astype(F32)
    cols = (pos % GRID_W).astype(F32)
    inv = ROPE_BASE ** (-jnp.arange(ROPE_PAIRS, dtype=F32) / ROPE_PAIRS)
    lane = jnp.arange(LANES)
    within = lane % HEAD_DIM
    axis = within // (2 * ROPE_PAIRS)
    half = (within % (2 * ROPE_PAIRS)) // ROPE_PAIRS
    ang = jnp.where(axis[None, :] == 0, rows[:, None], cols[:, None]) * inv[within % ROPE_PAIRS][None, :]
    cos, sin = jnp.cos(ang), jnp.sin(ang)
    sin_a = jnp.where(half[None, :] == 0, -sin, 0.0)
    sin_b = jnp.where(half[None, :] == 1, sin, 0.0)
    return cos, sin_a, sin_b


def _tile(L, pref):
    return min(L, pref)


def _four_step(L):
    return (2 * L) // FFT_N1 >= 32


def _conv_tables(L):
    return _fft_tables(FFT_N1, 2 * L // FFT_N1) if _four_step(L) else _dft_tables(L, min(L, 256))


def _run_group(x, rowmap, mods, p, cache, rope_tabs, dft, kv_dtype):
    B, L, D = x.shape
    tm = _tile(L, 512)
    ctx_k, ctx_v = [], []
    for i in range(DEPTH):
        j = i // 2
        sh1, sc1, g1, sh2, sc2, g2 = (mods[i, :, s * D:(s + 1) * D].reshape(MOD_ROWS, 1, D) for s in range(6))
        n1 = p["norm1_g"][i][None]
        if i % 2 == 0:
            lam_init = 0.8 - 0.6 * math.exp(-0.3 * i)
            q, k, v = _qkv(x, n1, sh1, sc1, p["attn_w_qkv"][j], rowmap, rope_tabs, kv_dtype, tm)
            ctx_k.append(k)
            ctx_v.append(v)
            mix = _attention(q, k, v, None if cache is None else cache + (j,),
                             p["attn_lambda"][j].astype(F32), p["attn_subln_g"][j][None],
                             lam_init, _tile(L, 512), _tile(L, 512))
            w_o, b_o = p["attn_w_o"][j], None
        else:
            vx, x0 = _hyena_in(x, n1, sh1, sc1, p["hy_w_in"][j], p["hy_b_in"][j][None],
                               p["hy_conv_w"][j], p["hy_conv_b"][j][None], rowmap, tm)
            filt = _implicit_filter(L, p["filt_w1"][j], p["filt_b1"][j], p["filt_w2"][j], p["filt_b2"][j],
                                    p["filt_w3"][j], p["filt_b3"][j], p["filt_freq"][j], 256)
            skip = p["hy_skip"][j][None]
            if _four_step(L):
                fb, gb, m3, m3i = dft
                kf = _fft_filter_spectrum(m3, _fft_step1(fb, filt))
                bm = _fft_step2(m3, m3i, kf, _fft_step1(fb, vx))
                mix = _fft_inv_step1(gb, bm, vx, x0, skip)
            else:
                fre, fim, gim = dft
                tmf = _tile(L, 256)
                kf = _filter_spectrum(_fwd_dft(fre, fim, filt, None, tmf, F32), tmf)
                y = _fwd_dft(fre, fim, vx, kf, tmf, BF16)
                mix = _inv_dft(fre, gim, y, vx, x0, skip, tm, _tile(L, 512))
            w_o, b_o = p["hy_w_out"][j], p["hy_b_out"][j][None]
        final_g = p["final_g"][None] if i == DEPTH - 1 else None
        x = _out_ffn(mix, w_o, b_o, x, g1, p["norm2_g"][i][None], sh2, sc2, g2,
                     p["ffn_w_gu"][i], p["ffn_w_down"][i], final_g, rowmap, tm)
    return x, ctx_k, ctx_v


def kernel(x_prompt, x_sample, cache_k, cache_v, c, c_ctx, ada_w, ada_b, norm1_g, norm2_g, attn_w_qkv, attn_lambda, attn_subln_g, attn_w_o, hy_w_in, hy_b_in, hy_conv_w, hy_conv_b, filt_w1, filt_b1, filt_w2, filt_b2, filt_w3, filt_b3, filt_freq, hy_skip, hy_w_out, hy_b_out, ffn_w_gu, ffn_w_down, final_g):
    D = D_MODEL
    nb = c.shape[0]
    cmat = jnp.concatenate([c, c_ctx[None], jnp.zeros((MOD_ROWS - nb - 1, D), F32)], axis=0)
    mods = _adaln_all(cmat, ada_w, ada_b)

    p = dict(norm1_g=norm1_g, norm2_g=norm2_g, attn_lambda=attn_lambda, attn_subln_g=attn_subln_g,
             hy_b_in=hy_b_in, hy_conv_w=hy_conv_w, hy_conv_b=hy_conv_b,
             filt_w1=filt_w1, filt_b1=filt_b1, filt_w2=filt_w2, filt_b2=filt_b2, filt_w3=filt_w3,
             filt_b3=filt_b3, filt_freq=filt_freq, hy_skip=hy_skip, hy_b_out=hy_b_out, final_g=final_g,
             attn_w_qkv=attn_w_qkv.astype(BF16), attn_w_o=attn_w_o.astype(BF16),
             hy_w_in=hy_w_in.astype(BF16), hy_w_out=hy_w_out.astype(BF16),
             ffn_w_gu=ffn_w_gu.astype(BF16), ffn_w_down=ffn_w_down.astype(BF16))

    Lp, Ls = x_prompt.shape[1], x_sample.shape[1]
    y_prompt, ctx_k, ctx_v = _run_group(x_prompt, lambda b: CTX_ROW, mods, p, None, None,
                                        _conv_tables(Lp), F32)
    cache = tuple(a.reshape(a.shape[0], a.shape[1], a.shape[2], D) for a in (cache_k, cache_v))
    y_sample, _, _ = _run_group(x_sample, lambda b: b, mods, p, cache, _rope_tables(Ls),
                                _conv_tables(Ls), BF16)

    Bp = x_prompt.shape[0]
    new_k = jnp.stack(ctx_k, axis=1).reshape(Bp, len(ctx_k), Lp, N_HEADS, 2, HEAD_DIM)
    new_v = jnp.stack(ctx_v, axis=1).reshape(Bp, len(ctx_v), Lp, N_HEADS, V_DIM)
    return y_prompt, y_sample, new_k, new_v
```

```python
import functools
import math

import jax
import jax.numpy as jnp
from jax import lax
from jax.experimental import pallas as pl
from jax.experimental.pallas import tpu as pltpu

D_MODEL = 1024
DEPTH = 4
GRID_W = 64
N_HEADS = 8
HEAD_DIM = 64
V_DIM = 2 * HEAD_DIM
ROPE_PAIRS = HEAD_DIM // 4
ROPE_BASE = 10000.0
FILT_BANDS = 16
FILT_ORDER = 64
FILT_TARGET = 1e-2
FILT_FAST_PCT = 0.3
FILT_SLOW_PCT = 1.5
FILT_EPS = 1e-6
D_FF = 2816
EPS = 1e-6
SUBLN_EPS = 1e-5

LANES = 128
MOD_ROWS = 16
CTX_ROW = 8
VMEM_LIMIT = 56 * 1024 * 1024

F32 = jnp.float32
BF16 = jnp.bfloat16
HI = lax.Precision.HIGHEST


def _params(*sem):
    return pltpu.CompilerParams(dimension_semantics=sem, vmem_limit_bytes=VMEM_LIMIT)


def _lane_tile(x, reps):
    return x if reps == 1 else jnp.concatenate([x] * reps, axis=1)


def _norm_mod(x, g, shift, scale):
    ms = jnp.mean(x * x, axis=-1, keepdims=True)
    return (x * lax.rsqrt(ms + EPS) * g) * (1.0 + scale) + shift


def _mod_kernel(c_ref, w_ref, b_ref, o_ref):
    c = c_ref[...]
    a = c * jax.nn.sigmoid(c)
    o_ref[...] = jnp.dot(a, w_ref[...], preferred_element_type=F32, precision=HI) + b_ref[...]


def _adaln_all(cmat, ada_w, ada_b):
    n6 = 6 * D_MODEL
    tn = 1536
    return pl.pallas_call(
        _mod_kernel,
        grid=(DEPTH, n6 // tn),
        in_specs=[
            pl.BlockSpec((MOD_ROWS, D_MODEL), lambda i, n: (0, 0)),
            pl.BlockSpec((None, D_MODEL, tn), lambda i, n: (i, 0, n)),
            pl.BlockSpec((None, 1, tn), lambda i, n: (i, 0, n)),
        ],
        out_specs=pl.BlockSpec((None, MOD_ROWS, tn), lambda i, n: (i, 0, n)),
        out_shape=jax.ShapeDtypeStruct((DEPTH, MOD_ROWS, n6), F32),
        compiler_params=_params("parallel", "parallel"),
        name="adaln",
    )(cmat, ada_w, ada_b.reshape(DEPTH, 1, n6))


def _rope_chunk(x, cos, sin_a, sin_b):
    return x * cos + pltpu.roll(x, LANES - ROPE_PAIRS, 1) * sin_a + pltpu.roll(x, ROPE_PAIRS, 1) * sin_b


def _qkv_kernel(*refs, rope, tn):
    if rope:
        x_ref, g_ref, sh_ref, sc_ref, w_ref, cos_ref, sa_ref, sb_ref, q_ref, k_ref, v_ref = refs
    else:
        x_ref, g_ref, sh_ref, sc_ref, w_ref, q_ref, k_ref, v_ref = refs
    h = _norm_mod(x_ref[...], g_ref[...], sh_ref[...], sc_ref[...]).astype(BF16)
    outs = (q_ref, k_ref, v_ref)
    qk_scale = HEAD_DIM ** -0.5 * math.log2(math.e)
    for part in range(3):
        for n in range(D_MODEL // tn):
            col = part * D_MODEL + n * tn
            y = jnp.dot(h, w_ref[:, col:col + tn], preferred_element_type=F32)
            for j in range(tn // LANES):
                yj = y[:, j * LANES:(j + 1) * LANES]
                if rope and part < 2:
                    yj = _rope_chunk(yj, cos_ref[...], sa_ref[...], sb_ref[...])
                if part == 0:
                    yj = yj * qk_scale
                lo = n * tn + j * LANES
                outs[part][:, lo:lo + LANES] = yj.astype(outs[part].dtype)


def _qkv(x, g, shift, scale, w, rowmap, rope_tabs, kv_dtype, tm):
    B, L, D = x.shape
    rope = rope_tabs is not None
    row = pl.BlockSpec((None, tm, D), lambda b, i: (b, i, 0))
    vec = pl.BlockSpec((None, 1, D), lambda b, i: (rowmap(b), 0, 0))
    in_specs = [row, pl.BlockSpec((1, D), lambda b, i: (0, 0)), vec, vec,
                pl.BlockSpec((D, 3 * D), lambda b, i: (0, 0))]
    args = [x, g, shift, scale, w]
    if rope:
        tab = pl.BlockSpec((tm, LANES), lambda b, i: (i, 0))
        in_specs += [tab, tab, tab]
        args += list(rope_tabs)
    return pl.pallas_call(
        functools.partial(_qkv_kernel, rope=rope, tn=512),
        grid=(B, L // tm),
        in_specs=in_specs,
        out_specs=[row, row, row],
        out_shape=[jax.ShapeDtypeStruct((B, L, D), BF16),
                   jax.ShapeDtypeStruct((B, L, D), kv_dtype),
                   jax.ShapeDtypeStruct((B, L, D), kv_dtype)],
        compiler_params=_params("parallel", "parallel"),
        name="qkv",
    )(*args)


def _hyena_in_kernel(xp_ref, x_ref, xn_ref, g_ref, sh_ref, sc_ref, w_ref, b_ref, cw_ref, cb_ref,
                     vx_ref, x0_ref, u_sc, *, tn):
    tm, D = x_ref.shape
    i = pl.program_id(1)
    xa = jnp.concatenate([xp_ref[...], x_ref[...], xn_ref[...]], axis=0)
    h = _norm_mod(xa, g_ref[...], sh_ref[...], sc_ref[...]).astype(BF16)
    keep_top = (i > 0).astype(F32)
    keep_bot = (i < pl.num_programs(1) - 1).astype(F32)

    def part(p, n, slot):
        sl = slice(p * D + n * tn, p * D + (n + 1) * tn)
        u = jnp.dot(h, w_ref[:, sl], preferred_element_type=F32) + b_ref[:, sl]
        u_sc[slot, 0:8] = u[:8] * keep_top
        u_sc[slot, 8:8 + tm] = u[8:8 + tm]
        u_sc[slot, 8 + tm:] = u[8 + tm:] * keep_bot
        w = cw_ref[:, sl]
        return (u_sc[slot, 7:7 + tm] * w[0:1] + u_sc[slot, 8:8 + tm] * w[1:2] + u_sc[slot, 9:9 + tm] * w[2:3]
                + cb_ref[:, sl])

    for n in range(D // tn):
        out = slice(n * tn, (n + 1) * tn)
        x0_ref[:, out] = part(0, n, 0)
        vx_ref[:, out] = (part(2, n, 1) * part(1, n, 2)).astype(vx_ref.dtype)


def _hyena_in(x, g, shift, scale, w, b, conv_w, conv_b, rowmap, tm):
    B, L, D = x.shape
    N = w.shape[1]
    hb = tm // 8
    tn = 512
    vec = pl.BlockSpec((None, 1, D), lambda b_, i: (rowmap(b_), 0, 0))
    row = pl.BlockSpec((None, tm, D), lambda b_, i: (b_, i, 0))
    const = lambda r: pl.BlockSpec((r, N), lambda b_, i: (0, 0))
    return pl.pallas_call(
        functools.partial(_hyena_in_kernel, tn=tn),
        grid=(B, L // tm),
        in_specs=[pl.BlockSpec((None, 8, D), lambda b_, i: (b_, jnp.maximum(i * hb - 1, 0), 0)),
                  row,
                  pl.BlockSpec((None, 8, D), lambda b_, i: (b_, jnp.minimum((i + 1) * hb, L // 8 - 1), 0)),
                  pl.BlockSpec((1, D), lambda b_, i: (0, 0)), vec, vec,
                  pl.BlockSpec((D, N), lambda b_, i: (0, 0)), const(1), const(3), const(1)],
        out_specs=[row, row],
        out_shape=[jax.ShapeDtypeStruct((B, L, D), BF16), jax.ShapeDtypeStruct((B, L, D), F32)],
        scratch_shapes=[pltpu.VMEM((3, tm + 16, tn), F32)],
        compiler_params=_params("parallel", "parallel"),
        name="hyena_in",
    )(x, x, x, g, shift, scale, w, b, conv_w, conv_b)


def _attn_kernel(*refs, tq, tk, lam_init, cached):
    if cached:
        lam_ref, g_ref, q_ref, ck_ref, cv_ref, k_ref, v_ref, o_ref = refs[:8]
    else:
        lam_ref, g_ref, q_ref, k_ref, v_ref, o_ref = refs[:6]
    m_sc, acc_sc, vx_sc, q2_sc, s0_sc, s1_sc, s2_sc = refs[-7:]
    nc = ck_ref.shape[0] // tk if cached else 0
    nk = nc + k_ref.shape[0] // tk
    nq = q_ref.shape[0] // tq
    if cached:
        vx_sc[:nc * tk, :V_DIM] = cv_ref[...].astype(BF16)
    vx_sc[nc * tk:, :V_DIM] = v_ref[...].astype(BF16)
    vx_sc[:, V_DIM:] = jnp.ones((vx_sc.shape[0], V_DIM), BF16)
    lp = lam_ref[...]
    lam = (jnp.exp(jnp.sum(lp[0:1] * lp[1:2], axis=1, keepdims=True))
           - jnp.exp(jnp.sum(lp[2:3] * lp[3:4], axis=1, keepdims=True)) + lam_init)
    lane = lax.broadcasted_iota(jnp.int32, (tq, LANES), 1)

    def slot(j):
        if j == 0:
            return s0_sc
        return s1_sc if j % 2 == 1 else s2_sc

    def key_tile(j):
        if j < nc:
            return ck_ref[j * tk:(j + 1) * tk, :].astype(BF16)
        return k_ref[(j - nc) * tk:(j - nc + 1) * tk, :].astype(BF16)

    def load_q(i):
        q = q_ref[pl.ds(pl.multiple_of(i * tq, tq), tq), :]
        zero = jnp.zeros_like(q)
        q2_sc[:tq] = jnp.where(lane < HEAD_DIM, q, zero)
        q2_sc[tq:] = jnp.where(lane >= HEAD_DIM, q, zero)

    def scores(j):
        s_sc = slot(j)
        s_sc[...] = lax.dot_general(q2_sc[...], key_tile(j), (((1,), (1,)), ((), ())),
                                    preferred_element_type=F32)

    def accumulate(j):
        s_sc = slot(j)
        t = jnp.broadcast_to(jnp.max(s_sc[...], axis=1, keepdims=True), m_sc.shape)
        if j == 0:
            m_new = t
        else:
            m_prev = m_sc[...]
            m_new = jnp.maximum(m_prev, t)
            alpha = jnp.exp2(m_prev - m_new)
        e = jnp.exp2(s_sc[...] - _lane_tile(m_new, tk // LANES))
        pv = jnp.dot(e.astype(BF16), vx_sc[j * tk:(j + 1) * tk, :], preferred_element_type=F32)
        acc_sc[...] = pv if j == 0 else _lane_tile(alpha, 2) * acc_sc[...] + pv
        m_sc[...] = m_new

    def finalize(i):
        o = (acc_sc[:tq, :V_DIM] / acc_sc[:tq, V_DIM:] - lam * (acc_sc[tq:, :V_DIM] / acc_sc[tq:, V_DIM:]))
        ms = jnp.mean(o * o, axis=-1, keepdims=True)
        y = (o * lax.rsqrt(ms + SUBLN_EPS) * g_ref[...]) * (1.0 - lam_init)
        o_ref[pl.ds(pl.multiple_of(i * tq, tq), tq), :] = y.astype(o_ref.dtype)

    def query_tile(i, prefetch):
        for j in range(1, nk):
            accumulate(j - 1)
            scores(j)
        if prefetch:
            load_q(i + 1)
        accumulate(nk - 1)
        if prefetch:
            scores(0)
        finalize(i)

    load_q(0)
    scores(0)

    def body(i, carry):
        query_tile(i, True)
        return carry
    lax.fori_loop(0, nq - 1, body, 0)
    query_tile(nq - 1, False)


def _attention(q, k, v, cache, lam_params, subln_g, lam_init, tq, tk):
    B, Lq, D = q.shape
    Lk = k.shape[1]
    kv = pl.BlockSpec((None, Lk, LANES), lambda b, h: (b, 0, h))
    qo = pl.BlockSpec((None, Lq, LANES), lambda b, h: (b, 0, h))
    in_specs = [pl.BlockSpec((4, HEAD_DIM), lambda b, h: (0, 0)),
                pl.BlockSpec((1, V_DIM), lambda b, h: (0, 0)), qo]
    args = [lam_params, subln_g, q]
    past = 0
    if cache is not None:
        ck, cv, layer = cache
        past = ck.shape[2]
        cspec = pl.BlockSpec((None, None, past, LANES), lambda b, h: (b, layer, 0, h))
        in_specs += [cspec, cspec]
        args += [ck, cv]
    in_specs += [kv, kv]
    args += [k, v]
    return pl.pallas_call(
        functools.partial(_attn_kernel, tq=tq, tk=tk, lam_init=lam_init, cached=cache is not None),
        grid=(B, N_HEADS),
        in_specs=in_specs,
        out_specs=qo,
        out_shape=jax.ShapeDtypeStruct((B, Lq, D), BF16),
        scratch_shapes=[pltpu.VMEM((2 * tq, LANES), F32),
                        pltpu.VMEM((2 * tq, 2 * V_DIM), F32),
                        pltpu.VMEM((past + Lk, 2 * V_DIM), BF16),
                        pltpu.VMEM((2 * tq, LANES), BF16)]
                       + [pltpu.VMEM((2 * tq, tk), F32)] * 3,
        compiler_params=_params("parallel", "parallel"),
        name="diff_attention",
    )(*args)


def _resident(shape):
    return pl.BlockSpec(shape, lambda b, i: (0,) * len(shape), pipeline_mode=pl.Buffered(1))


def _out_ffn_kernel(*refs, fc, final, bias):
    refs = list(refs)
    o_ref = refs.pop()
    fg_ref = refs.pop() if final else None
    a_ref, wo_ref = refs[:2]
    bo_ref = refs[2] if bias else None
    x_ref, gate1_ref, g_ref, sh_ref, sc_ref, gate2_ref, wgu_ref, wd_ref = refs[2 + bias:]
    y = jnp.dot(a_ref[...], wo_ref[...], preferred_element_type=F32)
    if bias:
        y = y + bo_ref[...]
    x = x_ref[...] + gate1_ref[...] * y
    h = _norm_mod(x, g_ref[...], sh_ref[...], sc_ref[...]).astype(BF16)
    acc = jnp.zeros(x.shape, F32)
    for f in range(D_FF // fc):
        gg = jnp.dot(h, wgu_ref[:, f * fc:(f + 1) * fc], preferred_element_type=F32)
        uu = jnp.dot(h, wgu_ref[:, D_FF + f * fc:D_FF + (f + 1) * fc], preferred_element_type=F32)
        act = (gg * jax.nn.sigmoid(gg)) * uu
        acc = acc + jnp.dot(act.astype(BF16), wd_ref[f * fc:(f + 1) * fc, :], preferred_element_type=F32)
    out = x + gate2_ref[...] * acc
    if final:
        ms = jnp.mean(out * out, axis=-1, keepdims=True)
        out = out * lax.rsqrt(ms + EPS) * fg_ref[...]
    o_ref[...] = out


def _out_ffn(a, w_o, b_o, x, gate1, g, shift, scale, gate2, w_gu, w_down, final_g, rowmap, tm):
    B, L, D = x.shape
    final, bias = final_g is not None, b_o is not None
    row = pl.BlockSpec((None, tm, D), lambda b, i: (b, i, 0))
    vec = pl.BlockSpec((None, 1, D), lambda b, i: (rowmap(b), 0, 0))
    const = pl.BlockSpec((1, D), lambda b, i: (0, 0))
    in_specs = [row, _resident((D, D))] + ([const] if bias else [])
    args = [a, w_o] + ([b_o] if bias else [])
    in_specs += [row, vec, const, vec, vec, vec, _resident((D, 2 * D_FF)), _resident((D_FF, D))]
    args += [x, gate1, g, shift, scale, gate2, w_gu, w_down]
    if final:
        in_specs.append(const)
        args.append(final_g)
    return pl.pallas_call(
        functools.partial(_out_ffn_kernel, fc=256, final=final, bias=bias),
        grid=(B, L // tm),
        in_specs=in_specs,
        out_specs=row,
        out_shape=jax.ShapeDtypeStruct((B, L, D), F32),
        compiler_params=_params("parallel", "parallel"),
        name="out_ffn",
    )(*args)


def _filter_kernel(bands_ref, w1t_ref, w1c_ref, w1s_ref, b1_ref, w2_ref, b2_ref, fr_ref,
                   w3f_ref, w3b_ref, b3f_ref, b3b_ref, delta_ref, h_ref, hid_sc):
    L = hid_sc.shape[0]
    pos = lax.broadcasted_iota(jnp.int32, (L, LANES), 0).astype(F32)
    t = pos / max(L - 1, 1)

    @pl.when(pl.program_id(0) == 0)
    def _():
        ang = (2.0 * math.pi * pos / L) * bands_ref[...]
        fr = fr_ref[...]
        pre = (t * w1t_ref[...]
               + jnp.dot(jnp.cos(ang), w1c_ref[...], preferred_element_type=F32, precision=HI)
               - jnp.dot(jnp.sin(ang), w1s_ref[...], preferred_element_type=F32, precision=HI)
               + b1_ref[...])
        hid = jnp.sin(fr * pre)
        hid_sc[...] = jnp.sin(fr * (jnp.dot(hid, w2_ref[...], preferred_element_type=F32, precision=HI)
                                    + b2_ref[...]))

    hid = hid_sc[...]
    window = jnp.exp(-t[:, 0:1] * delta_ref[...])
    hf = (jnp.dot(hid, w3f_ref[...], preferred_element_type=F32, precision=HI) + b3f_ref[...]) * window
    hb = (jnp.dot(hid, w3b_ref[...], preferred_element_type=F32, precision=HI) + b3b_ref[...]) * window
    norm = (jnp.sum(jnp.abs(hf), axis=0, keepdims=True) + jnp.sum(jnp.abs(hb), axis=0, keepdims=True)
            + FILT_EPS)
    row = lax.broadcasted_iota(jnp.int32, hb.shape, 0)
    h_ref[0] = (hf / norm).astype(h_ref.dtype)
    h_ref[1] = jnp.where(row == 0, 0.0, hb / norm).astype(h_ref.dtype)


def _pad2(a, rows, cols):
    return jnp.pad(a, ((0, rows - a.shape[0]), (0, cols - a.shape[1])))


def _implicit_filter(L, w1, b1, w2, b2, w3, b3, freq, tc):
    D = D_MODEL
    bands = _pad2(jnp.linspace(1e-4, FILT_BANDS - 1, FILT_BANDS, dtype=F32)[None], 1, LANES)
    deltas = jnp.abs(jnp.linspace(math.log(FILT_TARGET) / FILT_SLOW_PCT,
                                  math.log(FILT_TARGET) / FILT_FAST_PCT, D, dtype=F32))[None]
    w1 = w1.astype(F32)
    args = [bands,
            _pad2(w1[0:1], 1, LANES),
            _pad2(w1[1:1 + FILT_BANDS], LANES, LANES),
            _pad2(w1[1 + FILT_BANDS:], LANES, LANES),
            _pad2(b1[None], 1, LANES), _pad2(w2, LANES, LANES), _pad2(b2[None], 1, LANES),
            _pad2(freq[None], 1, LANES),
            _pad2(w3, LANES, 2 * D), _pad2(w3, LANES, 2 * D), b3[None], b3[None], deltas]
    small = lambda shape: pl.BlockSpec(shape, lambda c: (0, 0))
    nb = D // tc
    in_specs = [small((1, LANES)), small((1, LANES)), small((LANES, LANES)), small((LANES, LANES)),
                small((1, LANES)), small((LANES, LANES)), small((1, LANES)), small((1, LANES)),
                pl.BlockSpec((LANES, tc), lambda c: (0, c)),
                pl.BlockSpec((LANES, tc), lambda c: (0, nb + c)),
                pl.BlockSpec((1, tc), lambda c: (0, c)),
                pl.BlockSpec((1, tc), lambda c: (0, nb + c)),
                pl.BlockSpec((1, tc), lambda c: (0, c))]
    return pl.pallas_call(
        _filter_kernel,
        grid=(nb,),
        in_specs=in_specs,
        out_specs=pl.BlockSpec((2, L, tc), lambda c: (0, 0, c)),
        out_shape=jax.ShapeDtypeStruct((2, L, D), BF16),
        scratch_shapes=[pltpu.VMEM((L, LANES), F32)],
        compiler_params=_params("arbitrary"),
        name="hyena_filter",
    )(*args)


def _dft_gen_kernel(fre_ref, fim_ref, gim_ref, cb_sc, sb_sc, *, L):
    n = 2 * L
    tr = fre_ref.shape[0]
    step = 2.0 * math.pi / n
    i = pl.program_id(0)
    row = lax.broadcasted_iota(jnp.int32, (tr, L), 0)
    col = lax.broadcasted_iota(jnp.int32, (tr, L), 1)

    @pl.when(i == 0)
    def _():
        ang = ((row * col) & (n - 1)).astype(F32) * step
        cb_sc[...] = jnp.cos(ang)
        sb_sc[...] = jnp.sin(ang)

    col1 = lax.broadcasted_iota(jnp.int32, (1, L), 1)
    a = (((i * tr) * col1) & (n - 1)).astype(F32) * step
    ca, sa = jnp.cos(a), jnp.sin(a)
    cb, sb = cb_sc[...], sb_sc[...]
    c = cb * ca - sb * sa
    s = -(sb * ca + cb * sa)
    grow = row + i * tr
    fre_ref[...] = c.astype(fre_ref.dtype)
    fim_ref[...] = jnp.where(grow == 0, jnp.where((col & 1) == 0, 1.0, -1.0), s).astype(fim_ref.dtype)
    gim_ref[...] = jnp.where(col == 0, jnp.where((grow & 1) == 0, 1.0, -1.0), s).astype(gim_ref.dtype)


def _dft_tables(L, tr):
    spec = pl.BlockSpec((tr, L), lambda i: (i, 0))
    shape = jax.ShapeDtypeStruct((L, L), BF16)
    return pl.pallas_call(
        functools.partial(_dft_gen_kernel, L=L),
        grid=(L // tr,),
        out_specs=[spec, spec, spec],
        out_shape=[shape, shape, shape],
        scratch_shapes=[pltpu.VMEM((tr, L), F32), pltpu.VMEM((tr, L), F32)],
        compiler_params=_params("arbitrary"),
        name="dft_tables",
    )()


def _fwd_dft_kernel(*refs, with_filter):
    if with_filter:
        fre_ref, fim_ref, v_ref, kr_ref, ki_ref, y_ref = refs
    else:
        fre_ref, fim_ref, v_ref, y_ref = refs
    v = v_ref[...]
    vr = jnp.dot(fre_ref[...], v, preferred_element_type=F32)
    vi = jnp.dot(fim_ref[...], v, preferred_element_type=F32)
    if with_filter:
        kr, ki = kr_ref[...], ki_ref[...]
        tm = vr.shape[0]
        first = (lax.broadcasted_iota(jnp.int32, vr.shape, 0) + pl.program_id(1) * tm) == 0
        yr = vr * kr - jnp.where(first, 0.0, vi * ki)
        yi = jnp.where(first, vi * ki, vr * ki + vi * kr)
        y_ref[0] = yr.astype(y_ref.dtype)
        y_ref[1] = yi.astype(y_ref.dtype)
    else:
        y_ref[0] = vr.astype(y_ref.dtype)
        y_ref[1] = vi.astype(y_ref.dtype)


def _fwd_dft(fre, fim, v, kf, tm, out_dtype):
    B, L, D = v.shape
    ftile = pl.BlockSpec((tm, L), lambda b, m: (m, 0))
    in_specs = [ftile, ftile, pl.BlockSpec((None, L, D), lambda b, m: (b, 0, 0))]
    args = [fre, fim, v]
    if kf is not None:
        in_specs += [pl.BlockSpec((None, tm, D), lambda b, m: (0, m, 0)),
                     pl.BlockSpec((None, tm, D), lambda b, m: (1, m, 0))]
        args += [kf, kf]
    return pl.pallas_call(
        functools.partial(_fwd_dft_kernel, with_filter=kf is not None),
        grid=(B, L // tm),
        in_specs=in_specs,
        out_specs=pl.BlockSpec((None, 2, tm, D), lambda b, m: (b, 0, m, 0)),
        out_shape=jax.ShapeDtypeStruct((B, 2, L, D), out_dtype),
        compiler_params=_params("parallel", "parallel"),
        name="hyena_fwd_dft",
    )(*args)


def _filter_spectrum_kernel(hf_ref, hb_ref, k_ref, *, n):
    hfr, hfi = hf_ref[0], hf_ref[1]
    hbr, hbi = hb_ref[0], hb_ref[1]
    first = (lax.broadcasted_iota(jnp.int32, hfr.shape, 0) + pl.program_id(0) * hfr.shape[0]) == 0
    wgt = jnp.where(first, 1.0 / n, 2.0 / n)
    k_ref[0] = (hfr + hbr) * wgt
    k_ref[1] = jnp.where(first, hfi + hbi, hfi - hbi) * wgt


def _filter_spectrum(hspec, tm):
    _, _, L, D = hspec.shape
    return pl.pallas_call(
        functools.partial(_filter_spectrum_kernel, n=2 * L),
        grid=(L // tm,),
        in_specs=[pl.BlockSpec((None, 2, tm, D), lambda m: (0, 0, m, 0)),
                  pl.BlockSpec((None, 2, tm, D), lambda m: (1, 0, m, 0))],
        out_specs=pl.BlockSpec((2, tm, D), lambda m: (0, m, 0)),
        out_shape=jax.ShapeDtypeStruct((2, L, D), F32),
        compiler_params=_params("parallel"),
        name="hyena_filter_spectrum",
    )(hspec, hspec)


def _inv_dft_kernel(gre_ref, gim_ref, yr_ref, yi_ref, vx_ref, x0_ref, skip_ref, z_ref, acc_sc):
    kk = pl.program_id(2)

    @pl.when(kk == 0)
    def _():
        acc_sc[...] = jnp.zeros(acc_sc.shape, F32)

    acc_sc[...] += (jnp.dot(gre_ref[...], yr_ref[...], preferred_element_type=F32)
                    + jnp.dot(gim_ref[...], yi_ref[...], preferred_element_type=F32))

    @pl.when(kk == pl.num_programs(2) - 1)
    def _():
        z = (acc_sc[...] + vx_ref[...].astype(F32) * skip_ref[...]) * x0_ref[...]
        z_ref[...] = z.astype(z_ref.dtype)


def _inv_dft(gre, gim, y, vx, x0, skip, tm, tk):
    B, L, D = vx.shape
    gtile = pl.BlockSpec((tm, tk), lambda b, i, k: (i, k))
    row = pl.BlockSpec((None, tm, D), lambda b, i, k: (b, i, 0))
    return pl.pallas_call(
        _inv_dft_kernel,
        grid=(B, L // tm, L // tk),
        in_specs=[gtile, gtile,
                  pl.BlockSpec((None, None, tk, D), lambda b, i, k: (b, 0, k, 0)),
                  pl.BlockSpec((None, None, tk, D), lambda b, i, k: (b, 1, k, 0)),
                  row, row, pl.BlockSpec((1, D), lambda b, i, k: (0, 0))],
        out_specs=row,
        out_shape=jax.ShapeDtypeStruct((B, L, D), BF16),
        scratch_shapes=[pltpu.VMEM((tm, D), F32)],
        compiler_params=_params("parallel", "parallel", "arbitrary"),
        name="hyena_inv_dft",
    )(gre, gim, y, y, vx, x0, skip)


FFT_N1 = 64
FFT_SLOT_GROUP = 8
FFT_ROWS = 16


def _cos_units(idx, n):
    return jnp.cos((idx & (n - 1)).astype(F32) * (2.0 * math.pi / n))


def _fft_step1_tables_kernel(fb_ref, gb_ref, *, n1, n2):
    n = n1 * n2
    h1, quarter, R = n1 // 2, n // 4, FFT_ROWS
    lr, lh = R.bit_length() - 1, h1.bit_length() - 1
    g = pl.program_id(0)

    def entry(part, slot, t1):
        idx = jnp.where((part == 1) & (slot == 0), (n // 2) * t1, n2 * t1 * slot + quarter * part)
        return _cos_units(idx, n)

    rows, cols = fb_ref.shape
    r = lax.broadcasted_iota(jnp.int32, (rows, cols), 0) + g * rows
    c = lax.broadcasted_iota(jnp.int32, (rows, cols), 1)
    val = entry(r >> (lr + lh), (r >> lr) & (h1 - 1), c >> lr)
    fb_ref[...] = jnp.where((r & (R - 1)) == (c & (R - 1)), val, 0.0).astype(fb_ref.dtype)
    rows, cols = gb_ref.shape
    r = lax.broadcasted_iota(jnp.int32, (rows, cols), 0) + g * rows
    c = lax.broadcasted_iota(jnp.int32, (rows, cols), 1)
    slot = (c >> lr) & (h1 - 1)
    val = jnp.where(slot == 0, 1.0, 2.0) * entry(c >> (lr + lh), slot, r >> lr)
    gb_ref[...] = jnp.where((r & (R - 1)) == (c & (R - 1)), val, 0.0).astype(gb_ref.dtype)


def _fft_step2_tables_kernel(m3_ref, m3i_ref, *, n1, n2):
    n = n1 * n2
    quarter, half = n // 4, n2 // 2
    g = pl.program_id(0)
    r = lax.broadcasted_iota(jnp.int32, (2 * n2, 2 * n2), 0)
    c = lax.broadcasted_iota(jnp.int32, (2 * n2, 2 * n2), 1)
    rj, rim = r & (n2 - 1), (r >= n2).astype(jnp.int32)
    cj, cim = c & (n2 - 1), (c >= n2).astype(jnp.int32)
    for si in range(m3_ref.shape[0]):
        s = g * m3_ref.shape[0] + si
        fwd = _cos_units(cj * (s + n1 * rj) + quarter * (rim - cim), n)
        inv = _cos_units(rj * (s + n1 * cj) + quarter * (cim - rim), n)
        if si == 0:
            def slot0(j, t2, t_im, out_im):
                lo = (j < half) & (t_im == 0)
                hi = (j >= half) & (t_im == 1)
                idx0 = jnp.where((out_im == 1) & (j == 0), (n // 2) * t2, t2 * n1 * j + quarter * out_im)
                idxh = t2 * (n1 // 2 + n1 * (j - half)) + quarter * out_im
                return lo, hi, jnp.where(lo, idx0, idxh)
            lo, hi, idx = slot0(rj, cj, cim, rim)
            fwd0 = jnp.where(lo | hi, _cos_units(idx, n), 0.0)
            lo, hi, idx = slot0(cj, rj, rim, cim)
            amp = jnp.where(lo & (cj == 0), 1.0, 2.0)
            inv0 = jnp.where(lo | hi, amp * _cos_units(idx, n), 0.0)
            fwd = jnp.where(g == 0, fwd0, fwd)
            inv = jnp.where(g == 0, inv0, inv)
        m3_ref[si] = fwd.astype(m3_ref.dtype)
        m3i_ref[si] = inv.astype(m3i_ref.dtype)


def _fft_tables(n1, n2):
    h1, sg, R = n1 // 2, FFT_SLOT_GROUP, FFT_ROWS
    steps = 8
    fb, gb = pl.pallas_call(
        functools.partial(_fft_step1_tables_kernel, n1=n1, n2=n2),
        grid=(steps,),
        out_specs=[pl.BlockSpec((2 * h1 * R // steps, h1 * R), lambda g: (g, 0)),
                   pl.BlockSpec((h1 * R // steps, 2 * h1 * R), lambda g: (g, 0))],
        out_shape=[jax.ShapeDtypeStruct((2 * h1 * R, h1 * R), BF16),
                   jax.ShapeDtypeStruct((h1 * R, 2 * h1 * R), BF16)],
        compiler_params=_params("parallel"),
        name="fft_step1_tables",
    )()
    mspec = pl.BlockSpec((sg, 2 * n2, 2 * n2), lambda g: (g, 0, 0))
    m3, m3i = pl.pallas_call(
        functools.partial(_fft_step2_tables_kernel, n1=n1, n2=n2),
        grid=(h1 // sg,),
        out_specs=[mspec, mspec],
        out_shape=[jax.ShapeDtypeStruct((h1, 2 * n2, 2 * n2), BF16)] * 2,
        compiler_params=_params("parallel"),
        name="fft_step2_tables",
    )()
    return fb, gb, m3, m3i


def _fft_step1_kernel(fb_ref, x_ref, a_ref):
    h1, R, D = x_ref.shape
    a = jnp.dot(fb_ref[...], x_ref[...].reshape(h1 * R, D), preferred_element_type=F32)
    a_ref[...] = a.reshape(2, h1, R, D).astype(a_ref.dtype)


def _fft_step1(fb, x):
    B, L, D = x.shape
    R = FFT_ROWS
    h1 = fb.shape[1] // R
    n2 = L // h1
    return pl.pallas_call(
        _fft_step1_kernel,
        grid=(B, n2 // R),
        in_specs=[pl.BlockSpec(fb.shape, lambda b, c: (0, 0)),
                  pl.BlockSpec((None, h1, R, D), lambda b, c: (b, 0, c, 0))],
        out_specs=pl.BlockSpec((None, 2, h1, R, D), lambda b, c: (b, 0, 0, c, 0)),
        out_shape=jax.ShapeDtypeStruct((B, 2, h1, n2, D), BF16),
        compiler_params=_params("parallel", "parallel"),
        name="fft_step1",
    )(fb, x.reshape(B, h1, n2, D))


def _first_bin(shape, g):
    return (lax.broadcasted_iota(jnp.int32, shape, 0) == 0) & (g == 0)


def _fft_filter_spectrum_kernel(m3_ref, a_ref, k_ref, *, n):
    n2 = a_ref.shape[-2]
    g = pl.program_id(0)
    for si in range(m3_ref.shape[0]):
        hf = jnp.dot(m3_ref[si], jnp.concatenate([a_ref[0, 0, si], a_ref[0, 1, si]], axis=0),
                     preferred_element_type=F32)
        hb = jnp.dot(m3_ref[si], jnp.concatenate([a_ref[1, 0, si], a_ref[1, 1, si]], axis=0),
                     preferred_element_type=F32)
        k_ref[0, si] = (hf[:n2] + hb[:n2]) * (1.0 / n)
        ki = hf[n2:] - hb[n2:]
        if si == 0:
            ki = jnp.where(_first_bin(ki.shape, g), hf[n2:] + hb[n2:], ki)
        k_ref[1, si] = ki * (1.0 / n)


def _fft_filter_spectrum(m3, a):
    h1, n2 = m3.shape[0], m3.shape[1] // 2
    D = a.shape[-1]
    sg = FFT_SLOT_GROUP
    return pl.pallas_call(
        functools.partial(_fft_filter_spectrum_kernel, n=2 * h1 * n2),
        grid=(h1 // sg,),
        in_specs=[pl.BlockSpec((sg, 2 * n2, 2 * n2), lambda g: (g, 0, 0)),
                  pl.BlockSpec((2, 2, sg, n2, D), lambda g: (0, 0, g, 0, 0))],
        out_specs=pl.BlockSpec((2, sg, n2, D), lambda g: (0, g, 0, 0)),
        out_shape=jax.ShapeDtypeStruct((2, h1, n2, D), F32),
        compiler_params=_params("parallel"),
        name="fft_filter_spectrum",
    )(m3, a)


def _fft_step2_kernel(m3_ref, m3i_ref, kr_ref, ki_ref, a_ref, b_ref, y_sc):
    n2 = a_ref.shape[-2]
    g = pl.program_id(0)
    sg = m3_ref.shape[0]
    for si in range(sg):
        xh = jnp.dot(m3_ref[si], a_ref[:, si].reshape(2 * n2, a_ref.shape[-1]), preferred_element_type=F32)
        xr, xi = xh[:n2], xh[n2:]
        kr, ki = kr_ref[si], ki_ref[si]
        yr = xr * kr - xi * ki
        yi = xr * ki + xi * kr
        if si == 0:
            first = _first_bin(yr.shape, g)
            yr = jnp.where(first, xr * kr, yr)
            yi = jnp.where(first, xi * ki, yi)
        y_sc[si, :n2] = yr.astype(BF16)
        y_sc[si, n2:] = yi.astype(BF16)
    for si in range(sg):
        bb = jnp.dot(m3i_ref[si], y_sc[si], preferred_element_type=F32)
        b_ref[0, si] = bb[:n2].astype(b_ref.dtype)
        b_ref[1, si] = bb[n2:].astype(b_ref.dtype)


def _fft_step2(m3, m3i, kf, a):
    B, _, h1, n2, D = a.shape
    sg = FFT_SLOT_GROUP
    mspec = pl.BlockSpec((sg, 2 * n2, 2 * n2), lambda g, b: (g, 0, 0))
    blk = pl.BlockSpec((None, 2, sg, n2, D), lambda g, b: (b, 0, g, 0, 0))
    return pl.pallas_call(
        _fft_step2_kernel,
        grid=(h1 // sg, B),
        in_specs=[mspec, mspec,
                  pl.BlockSpec((None, sg, n2, D), lambda g, b: (0, g, 0, 0)),
                  pl.BlockSpec((None, sg, n2, D), lambda g, b: (1, g, 0, 0)),
                  blk],
        out_specs=blk,
        out_shape=jax.ShapeDtypeStruct((B, 2, h1, n2, D), BF16),
        scratch_shapes=[pltpu.VMEM((sg, 2 * n2, D), BF16)],
        compiler_params=_params("parallel", "parallel"),
        name="fft_step2",
    )(m3, m3i, kf, kf, a)


def _fft_inv_step1_kernel(gb_ref, b_ref, vx_ref, x0_ref, skip_ref, z_ref):
    _, h1, R, D = b_ref.shape
    y = jnp.dot(gb_ref[...], b_ref[...].reshape(2 * h1 * R, D), preferred_element_type=F32)
    z = (y.reshape(h1, R, D) + vx_ref[...].astype(F32) * skip_ref[...]) * x0_ref[...]
    z_ref[...] = z.astype(z_ref.dtype)


def _fft_inv_step1(gb, bm, vx, x0, skip):
    B, L, D = vx.shape
    R = FFT_ROWS
    h1 = gb.shape[0] // R
    n2 = L // h1
    blk = pl.BlockSpec((None, h1, R, D), lambda b, c: (b, 0, c, 0))
    z = pl.pallas_call(
        _fft_inv_step1_kernel,
        grid=(B, n2 // R),
        in_specs=[pl.BlockSpec(gb.shape, lambda b, c: (0, 0)),
                  pl.BlockSpec((None, 2, h1, R, D), lambda b, c: (b, 0, 0, c, 0)),
                  blk, blk, pl.BlockSpec((1, D), lambda b, c: (0, 0))],
        out_specs=blk,
        out_shape=jax.ShapeDtypeStruct((B, h1, n2, D), BF16),
        compiler_params=_params("parallel", "parallel"),
        name="fft_inv_step1",
    )(gb, bm, vx.reshape(B, h1, n2, D), x0.reshape(B, h1, n2, D), skip)
    return z.reshape(B, L, D)


def _rope_tables(L):
    pos = jnp.arange(L, dtype=jnp.int32)
    rows = (pos // GRID_W).astype(F32)
    cols = (pos % GRID_W).astype(F32)
    inv = ROPE_BASE ** (-jnp.arange(ROPE_PAIRS, dtype=F32) / ROPE_PAIRS)
    lane = jnp.arange(LANES)
    within = lane % HEAD_DIM
    axis = within // (2 * ROPE_PAIRS)
    half = (within % (2 * ROPE_PAIRS)) // ROPE_PAIRS
    ang = jnp.where(axis[None, :] == 0, rows[:, None], cols[:, None]) * inv[within % ROPE_PAIRS][None, :]
    cos, sin = jnp.cos(ang), jnp.sin(ang)
    sin_a = jnp.where(half[None, :] == 0, -sin, 0.0)
    sin_b = jnp.where(half[None, :] == 1, sin, 0.0)
    return cos, sin_a, sin_b


def _tile(L, pref):
    return min(L, pref)


def _four_step(L):
    return (2 * L) // FFT_N1 >= 32


def _conv_tables(L):
    return _fft_tables(FFT_N1, 2 * L // FFT_N1) if _four_step(L) else _dft_tables(L, min(L, 256))


def _run_group(x, rowmap, shared_mod, mods, p, cache, rope_tabs, dft, kv_dtype):
    B, L, D = x.shape
    tm = _tile(L, 512)
    rows = (1, B * L, D) if shared_mod else (B, L, D)
    tmr = _tile(rows[1], 512)
    ctx_k, ctx_v = [], []
    for i in range(DEPTH):
        j = i // 2
        sh1, sc1, g1, sh2, sc2, g2 = (mods[i, :, s * D:(s + 1) * D].reshape(MOD_ROWS, 1, D) for s in range(6))
        n1 = p["norm1_g"][i][None]
        if i % 2 == 0:
            lam_init = 0.8 - 0.6 * math.exp(-0.3 * i)
            q, k, v = (a.reshape(B, L, D) for a in
                       _qkv(x.reshape(rows), n1, sh1, sc1, p["attn_w_qkv"][j], rowmap, rope_tabs, kv_dtype, tmr))
            ctx_k.append(k)
            ctx_v.append(v)
            mix = _attention(q, k, v, None if cache is None else cache + (j,),
                             p["attn_lambda"][j].astype(F32), p["attn_subln_g"][j][None],
                             lam_init, _tile(L, 512), _tile(L, 512))
            w_o, b_o = p["attn_w_o"][j], None
        else:
            vx, x0 = _hyena_in(x, n1, sh1, sc1, p["hy_w_in"][j], p["hy_b_in"][j][None],
                               p["hy_conv_w"][j], p["hy_conv_b"][j][None], rowmap, tm)
            filt = _implicit_filter(L, p["filt_w1"][j], p["filt_b1"][j], p["filt_w2"][j], p["filt_b2"][j],
                                    p["filt_w3"][j], p["filt_b3"][j], p["filt_freq"][j], 256)
            skip = p["hy_skip"][j][None]
            if _four_step(L):
                fb, gb, m3, m3i = dft
                kf = _fft_filter_spectrum(m3, _fft_step1(fb, filt))
                bm = _fft_step2(m3, m3i, kf, _fft_step1(fb, vx))
                mix = _fft_inv_step1(gb, bm, vx, x0, skip)
            else:
                fre, fim, gim = dft
                tmf = _tile(L, 256)
                kf = _filter_spectrum(_fwd_dft(fre, fim, filt, None, tmf, F32), tmf)
                y = _fwd_dft(fre, fim, vx, kf, tmf, BF16)
                mix = _inv_dft(fre, gim, y, vx, x0, skip, tm, _tile(L, 512))
            w_o, b_o = p["hy_w_out"][j], p["hy_b_out"][j][None]
        final_g = p["final_g"][None] if i == DEPTH - 1 else None
        x = _out_ffn(mix.reshape(rows), w_o, b_o, x.reshape(rows), g1, p["norm2_g"][i][None], sh2, sc2, g2,
                     p["ffn_w_gu"][i], p["ffn_w_down"][i], final_g, rowmap, tmr).reshape(B, L, D)
    return x, ctx_k, ctx_v


def kernel(x_prompt, x_sample, cache_k, cache_v, c, c_ctx, ada_w, ada_b, norm1_g, norm2_g, attn_w_qkv, attn_lambda, attn_subln_g, attn_w_o, hy_w_in, hy_b_in, hy_conv_w, hy_conv_b, filt_w1, filt_b1, filt_w2, filt_b2, filt_w3, filt_b3, filt_freq, hy_skip, hy_w_out, hy_b_out, ffn_w_gu, ffn_w_down, final_g):
    D = D_MODEL
    nb = c.shape[0]
    cmat = jnp.concatenate([c, c_ctx[None], jnp.zeros((MOD_ROWS - nb - 1, D), F32)], axis=0)
    mods = _adaln_all(cmat, ada_w, ada_b)

    p = dict(norm1_g=norm1_g, norm2_g=norm2_g, attn_lambda=attn_lambda, attn_subln_g=attn_subln_g,
             hy_b_in=hy_b_in, hy_conv_w=hy_conv_w, hy_conv_b=hy_conv_b,
             filt_w1=filt_w1, filt_b1=filt_b1, filt_w2=filt_w2, filt_b2=filt_b2, filt_w3=filt_w3,
             filt_b3=filt_b3, filt_freq=filt_freq, hy_skip=hy_skip, hy_b_out=hy_b_out, final_g=final_g,
             attn_w_qkv=attn_w_qkv.astype(BF16), attn_w_o=attn_w_o.astype(BF16),
             hy_w_in=hy_w_in.astype(BF16), hy_w_out=hy_w_out.astype(BF16),
             ffn_w_gu=ffn_w_gu.astype(BF16), ffn_w_down=ffn_w_down.astype(BF16))

    Lp, Ls = x_prompt.shape[1], x_sample.shape[1]
    y_prompt, ctx_k, ctx_v = _run_group(x_prompt, lambda b: CTX_ROW, True, mods, p, None, None,
                                        _conv_tables(Lp), F32)
    cache = tuple(a.reshape(a.shape[0], a.shape[1], a.shape[2], D) for a in (cache_k, cache_v))
    y_sample, _, _ = _run_group(x_sample, lambda b: b, False, mods, p, cache, _rope_tables(Ls),
                                _conv_tables(Ls), BF16)

    Bp = x_prompt.shape[0]
    new_k = jnp.stack(ctx_k, axis=1).reshape(Bp, len(ctx_k), Lp, N_HEADS, 2, HEAD_DIM)
    new_v = jnp.stack(ctx_v, axis=1).reshape(Bp, len(ctx_v), Lp, N_HEADS, V_DIM)
    return y_prompt, y_sample, new_k, new_v
```

```python
import functools
import math

import jax
import jax.numpy as jnp
from jax import lax
from jax.experimental import pallas as pl
from jax.experimental.pallas import tpu as pltpu

D_MODEL = 1024
DEPTH = 4
GRID_W = 64
N_HEADS = 8
HEAD_DIM = 64
V_DIM = 2 * HEAD_DIM
ROPE_PAIRS = HEAD_DIM // 4
ROPE_BASE = 10000.0
FILT_BANDS = 16
FILT_ORDER = 64
FILT_TARGET = 1e-2
FILT_FAST_PCT = 0.3
FILT_SLOW_PCT = 1.5
FILT_EPS = 1e-6
D_FF = 2816
EPS = 1e-6
SUBLN_EPS = 1e-5

LANES = 128
MOD_ROWS = 16
CTX_ROW = 8
VMEM_LIMIT = 56 * 1024 * 1024

ROW_TILE = 512
ATTN_TILE = 512
DFT_ROW_TILE = 256
FILT_COL_TILE = 256
COL_CHUNK = 512
FFN_CHUNK = 256

F32 = jnp.float32
BF16 = jnp.bfloat16
HI = lax.Precision.HIGHEST


def _params(*sem):
    return pltpu.CompilerParams(dimension_semantics=sem, vmem_limit_bytes=VMEM_LIMIT)


def _lane_tile(x, reps):
    return x if reps == 1 else jnp.concatenate([x] * reps, axis=1)


def _norm_mod(x, g, shift, scale):
    ms = jnp.mean(x * x, axis=-1, keepdims=True)
    return (x * lax.rsqrt(ms + EPS) * g) * (1.0 + scale) + shift


def _mod_kernel(c_ref, w_ref, b_ref, o_ref):
    c = c_ref[...]
    a = c * jax.nn.sigmoid(c)
    o_ref[...] = jnp.dot(a, w_ref[...], preferred_element_type=F32, precision=HI) + b_ref[...]


def _adaln_all(cmat, ada_w, ada_b):
    n6 = 6 * D_MODEL
    tn = 1536
    return pl.pallas_call(
        _mod_kernel,
        grid=(DEPTH, n6 // tn),
        in_specs=[
            pl.BlockSpec((MOD_ROWS, D_MODEL), lambda i, n: (0, 0)),
            pl.BlockSpec((None, D_MODEL, tn), lambda i, n: (i, 0, n)),
            pl.BlockSpec((None, 1, tn), lambda i, n: (i, 0, n)),
        ],
        out_specs=pl.BlockSpec((None, MOD_ROWS, tn), lambda i, n: (i, 0, n)),
        out_shape=jax.ShapeDtypeStruct((DEPTH, MOD_ROWS, n6), F32),
        compiler_params=_params("parallel", "parallel"),
        name="adaln",
    )(cmat, ada_w, ada_b.reshape(DEPTH, 1, n6))


def _rope_chunk(x, cos, sin_a, sin_b):
    return x * cos + pltpu.roll(x, LANES - ROPE_PAIRS, 1) * sin_a + pltpu.roll(x, ROPE_PAIRS, 1) * sin_b


def _qkv_kernel(*refs, rope, tn):
    if rope:
        x_ref, g_ref, sh_ref, sc_ref, w_ref, cos_ref, sa_ref, sb_ref, q_ref, k_ref, v_ref = refs
    else:
        x_ref, g_ref, sh_ref, sc_ref, w_ref, q_ref, k_ref, v_ref = refs
    h = _norm_mod(x_ref[...], g_ref[...], sh_ref[...], sc_ref[...]).astype(BF16)
    outs = (q_ref, k_ref, v_ref)
    qk_scale = HEAD_DIM ** -0.5 * math.log2(math.e)
    for part in range(3):
        for n in range(D_MODEL // tn):
            col = part * D_MODEL + n * tn
            y = jnp.dot(h, w_ref[:, col:col + tn], preferred_element_type=F32)
            for j in range(tn // LANES):
                yj = y[:, j * LANES:(j + 1) * LANES]
                if rope and part < 2:
                    yj = _rope_chunk(yj, cos_ref[...], sa_ref[...], sb_ref[...])
                if part == 0:
                    yj = yj * qk_scale
                lo = n * tn + j * LANES
                outs[part][:, lo:lo + LANES] = yj.astype(outs[part].dtype)


def _qkv(x, g, shift, scale, w, rowmap, rope_tabs, kv_dtype, tm):
    B, L, D = x.shape
    rope = rope_tabs is not None
    row = pl.BlockSpec((None, tm, D), lambda b, i: (b, i, 0))
    vec = pl.BlockSpec((None, 1, D), lambda b, i: (rowmap(b), 0, 0))
    in_specs = [row, pl.BlockSpec((1, D), lambda b, i: (0, 0)), vec, vec,
                pl.BlockSpec((D, 3 * D), lambda b, i: (0, 0))]
    args = [x, g, shift, scale, w]
    if rope:
        tab = pl.BlockSpec((tm, LANES), lambda b, i: (i, 0))
        in_specs += [tab, tab, tab]
        args += list(rope_tabs)
    return pl.pallas_call(
        functools.partial(_qkv_kernel, rope=rope, tn=COL_CHUNK),
        grid=(B, L // tm),
        in_specs=in_specs,
        out_specs=[row, row, row],
        out_shape=[jax.ShapeDtypeStruct((B, L, D), BF16),
                   jax.ShapeDtypeStruct((B, L, D), kv_dtype),
                   jax.ShapeDtypeStruct((B, L, D), kv_dtype)],
        compiler_params=_params("parallel", "parallel"),
        name="qkv",
    )(*args)


def _hyena_in_kernel(xp_ref, x_ref, xn_ref, g_ref, sh_ref, sc_ref, w_ref, b_ref, cw_ref, cb_ref,
                     vx_ref, x0_ref, u_sc, *, tn):
    tm, D = x_ref.shape
    i = pl.program_id(1)
    xa = jnp.concatenate([xp_ref[...], x_ref[...], xn_ref[...]], axis=0)
    h = _norm_mod(xa, g_ref[...], sh_ref[...], sc_ref[...]).astype(BF16)
    keep_top = (i > 0).astype(F32)
    keep_bot = (i < pl.num_programs(1) - 1).astype(F32)

    def part(p, n, slot):
        sl = slice(p * D + n * tn, p * D + (n + 1) * tn)
        u = jnp.dot(h, w_ref[:, sl], preferred_element_type=F32) + b_ref[:, sl]
        u_sc[slot, 0:8] = u[:8] * keep_top
        u_sc[slot, 8:8 + tm] = u[8:8 + tm]
        u_sc[slot, 8 + tm:] = u[8 + tm:] * keep_bot
        w = cw_ref[:, sl]
        return (u_sc[slot, 7:7 + tm] * w[0:1] + u_sc[slot, 8:8 + tm] * w[1:2] + u_sc[slot, 9:9 + tm] * w[2:3]
                + cb_ref[:, sl])

    for n in range(D // tn):
        out = slice(n * tn, (n + 1) * tn)
        x0_ref[:, out] = part(0, n, 0)
        vx_ref[:, out] = (part(2, n, 1) * part(1, n, 2)).astype(vx_ref.dtype)


def _hyena_in(x, g, shift, scale, w, b, conv_w, conv_b, rowmap, tm):
    B, L, D = x.shape
    N = w.shape[1]
    hb = tm // 8
    tn = COL_CHUNK
    vec = pl.BlockSpec((None, 1, D), lambda b_, i: (rowmap(b_), 0, 0))
    row = pl.BlockSpec((None, tm, D), lambda b_, i: (b_, i, 0))
    const = lambda r: pl.BlockSpec((r, N), lambda b_, i: (0, 0))
    return pl.pallas_call(
        functools.partial(_hyena_in_kernel, tn=tn),
        grid=(B, L // tm),
        in_specs=[pl.BlockSpec((None, 8, D), lambda b_, i: (b_, jnp.maximum(i * hb - 1, 0), 0)),
                  row,
                  pl.BlockSpec((None, 8, D), lambda b_, i: (b_, jnp.minimum((i + 1) * hb, L // 8 - 1), 0)),
                  pl.BlockSpec((1, D), lambda b_, i: (0, 0)), vec, vec,
                  pl.BlockSpec((D, N), lambda b_, i: (0, 0)), const(1), const(3), const(1)],
        out_specs=[row, row],
        out_shape=[jax.ShapeDtypeStruct((B, L, D), BF16), jax.ShapeDtypeStruct((B, L, D), F32)],
        scratch_shapes=[pltpu.VMEM((3, tm + 16, tn), F32)],
        compiler_params=_params("parallel", "parallel"),
        name="hyena_in",
    )(x, x, x, g, shift, scale, w, b, conv_w, conv_b)


def _attn_kernel(*refs, heads, **static):
    n_scratch = 7
    lam_ref, g_ref = refs[:2]
    for h in range(heads):
        cols = slice(h * LANES, (h + 1) * LANES)
        per_head = [r.at[:, cols] for r in refs[2:len(refs) - n_scratch]]
        _attn_head(lam_ref, g_ref, *per_head, *refs[len(refs) - n_scratch:], **static)


def _attn_head(*refs, tq, tk, lam_init, cached):
    if cached:
        lam_ref, g_ref, q_ref, ck_ref, cv_ref, k_ref, v_ref, o_ref = refs[:8]
    else:
        lam_ref, g_ref, q_ref, k_ref, v_ref, o_ref = refs[:6]
    m_sc, acc_sc, vx_sc, q2_sc, s0_sc, s1_sc, s2_sc = refs[-7:]
    nc = ck_ref.shape[0] // tk if cached else 0
    nk = nc + k_ref.shape[0] // tk
    nq = q_ref.shape[0] // tq
    if cached:
        vx_sc[:nc * tk, :V_DIM] = cv_ref[...].astype(BF16)
    vx_sc[nc * tk:, :V_DIM] = v_ref[...].astype(BF16)
    vx_sc[:, V_DIM:] = jnp.ones((vx_sc.shape[0], V_DIM), BF16)
    lp = lam_ref[...]
    lam = (jnp.exp(jnp.sum(lp[0:1] * lp[1:2], axis=1, keepdims=True))
           - jnp.exp(jnp.sum(lp[2:3] * lp[3:4], axis=1, keepdims=True)) + lam_init)
    lane = lax.broadcasted_iota(jnp.int32, (tq, LANES), 1)

    def slot(j):
        if j == 0:
            return s0_sc
        return s1_sc if j % 2 == 1 else s2_sc

    def key_tile(j):
        if j < nc:
            return ck_ref[j * tk:(j + 1) * tk, :].astype(BF16)
        return k_ref[(j - nc) * tk:(j - nc + 1) * tk, :].astype(BF16)

    def load_q(i):
        q = q_ref[pl.ds(pl.multiple_of(i * tq, tq), tq), :]
        zero = jnp.zeros_like(q)
        q2_sc[:tq] = jnp.where(lane < HEAD_DIM, q, zero)
        q2_sc[tq:] = jnp.where(lane >= HEAD_DIM, q, zero)

    def scores(j):
        s_sc = slot(j)
        s_sc[...] = lax.dot_general(q2_sc[...], key_tile(j), (((1,), (1,)), ((), ())),
                                    preferred_element_type=F32)

    def accumulate(j):
        s_sc = slot(j)
        t = jnp.broadcast_to(jnp.max(s_sc[...], axis=1, keepdims=True), m_sc.shape)
        if j == 0:
            m_new = t
        else:
            m_prev = m_sc[...]
            m_new = jnp.maximum(m_prev, t)
            alpha = jnp.exp2(m_prev - m_new)
        e = jnp.exp2(s_sc[...] - _lane_tile(m_new, tk // LANES))
        pv = jnp.dot(e.astype(BF16), vx_sc[j * tk:(j + 1) * tk, :], preferred_element_type=F32)
        acc_sc[...] = pv if j == 0 else _lane_tile(alpha, 2) * acc_sc[...] + pv
        m_sc[...] = m_new

    def finalize(i):
        o = (acc_sc[:tq, :V_DIM] / acc_sc[:tq, V_DIM:] - lam * (acc_sc[tq:, :V_DIM] / acc_sc[tq:, V_DIM:]))
        ms = jnp.mean(o * o, axis=-1, keepdims=True)
        y = (o * lax.rsqrt(ms + SUBLN_EPS) * g_ref[...]) * (1.0 - lam_init)
        o_ref[pl.ds(pl.multiple_of(i * tq, tq), tq), :] = y.astype(o_ref.dtype)

    def query_tile(i, prefetch):
        for j in range(1, nk):
            accumulate(j - 1)
            scores(j)
        if prefetch:
            load_q(i + 1)
        accumulate(nk - 1)
        if prefetch:
            scores(0)
        finalize(i)

    load_q(0)
    scores(0)

    def body(i, carry):
        query_tile(i, True)
        return carry
    lax.fori_loop(0, nq - 1, body, 0)
    query_tile(nq - 1, False)


def _attention(q, k, v, cache, lam_params, subln_g, lam_init, tq, tk, heads):
    B, Lq, D = q.shape
    Lk = k.shape[1]
    width = heads * LANES
    kv = pl.BlockSpec((None, Lk, width), lambda b, h: (b, 0, h))
    qo = pl.BlockSpec((None, Lq, width), lambda b, h: (b, 0, h))
    in_specs = [pl.BlockSpec((4, HEAD_DIM), lambda b, h: (0, 0)),
                pl.BlockSpec((1, V_DIM), lambda b, h: (0, 0)), qo]
    args = [lam_params, subln_g, q]
    past = 0
    if cache is not None:
        ck, cv, layer = cache
        past = ck.shape[2]
        cspec = pl.BlockSpec((None, None, past, width), lambda b, h: (b, layer, 0, h))
        in_specs += [cspec, cspec]
        args += [ck, cv]
    in_specs += [kv, kv]
    args += [k, v]
    return pl.pallas_call(
        functools.partial(_attn_kernel, heads=heads, tq=tq, tk=tk, lam_init=lam_init,
                          cached=cache is not None),
        grid=(B, N_HEADS // heads),
        in_specs=in_specs,
        out_specs=qo,
        out_shape=jax.ShapeDtypeStruct((B, Lq, D), BF16),
        scratch_shapes=[pltpu.VMEM((2 * tq, LANES), F32),
                        pltpu.VMEM((2 * tq, 2 * V_DIM), F32),
                        pltpu.VMEM((past + Lk, 2 * V_DIM), BF16),
                        pltpu.VMEM((2 * tq, LANES), BF16)]
                       + [pltpu.VMEM((2 * tq, tk), F32)] * 3,
        compiler_params=_params("parallel", "parallel"),
        name="diff_attention",
    )(*args)


def _resident(shape):
    return pl.BlockSpec(shape, lambda b, i: (0,) * len(shape), pipeline_mode=pl.Buffered(1))


def _out_ffn_kernel(*refs, fc, final, bias):
    refs = list(refs)
    o_ref = refs.pop()
    fg_ref = refs.pop() if final else None
    a_ref, wo_ref = refs[:2]
    bo_ref = refs[2] if bias else None
    x_ref, gate1_ref, g_ref, sh_ref, sc_ref, gate2_ref, wgu_ref, wd_ref = refs[2 + bias:]
    y = jnp.dot(a_ref[...], wo_ref[...], preferred_element_type=F32)
    if bias:
        y = y + bo_ref[...]
    x = x_ref[...] + gate1_ref[...] * y
    h = _norm_mod(x, g_ref[...], sh_ref[...], sc_ref[...]).astype(BF16)
    acc = jnp.zeros(x.shape, F32)
    for f in range(D_FF // fc):
        gg = jnp.dot(h, wgu_ref[:, f * fc:(f + 1) * fc], preferred_element_type=F32)
        uu = jnp.dot(h, wgu_ref[:, D_FF + f * fc:D_FF + (f + 1) * fc], preferred_element_type=F32)
        act = (gg * jax.nn.sigmoid(gg)) * uu
        acc = acc + jnp.dot(act.astype(BF16), wd_ref[f * fc:(f + 1) * fc, :], preferred_element_type=F32)
    out = x + gate2_ref[...] * acc
    if final:
        ms = jnp.mean(out * out, axis=-1, keepdims=True)
        out = out * lax.rsqrt(ms + EPS) * fg_ref[...]
    o_ref[...] = out


def _out_ffn(a, w_o, b_o, x, gate1, g, shift, scale, gate2, w_gu, w_down, final_g, rowmap, tm):
    B, L, D = x.shape
    final, bias = final_g is not None, b_o is not None
    row = pl.BlockSpec((None, tm, D), lambda b, i: (b, i, 0))
    vec = pl.BlockSpec((None, 1, D), lambda b, i: (rowmap(b), 0, 0))
    const = pl.BlockSpec((1, D), lambda b, i: (0, 0))
    in_specs = [row, _resident((D, D))] + ([const] if bias else [])
    args = [a, w_o] + ([b_o] if bias else [])
    in_specs += [row, vec, const, vec, vec, vec, _resident((D, 2 * D_FF)), _resident((D_FF, D))]
    args += [x, gate1, g, shift, scale, gate2, w_gu, w_down]
    if final:
        in_specs.append(const)
        args.append(final_g)
    return pl.pallas_call(
        functools.partial(_out_ffn_kernel, fc=FFN_CHUNK, final=final, bias=bias),
        grid=(B, L // tm),
        in_specs=in_specs,
        out_specs=row,
        out_shape=jax.ShapeDtypeStruct((B, L, D), F32),
        compiler_params=_params("parallel", "parallel"),
        name="out_ffn",
    )(*args)


def _filter_kernel(bands_ref, w1t_ref, w1c_ref, w1s_ref, b1_ref, w2_ref, b2_ref, fr_ref,
                   w3f_ref, w3b_ref, b3f_ref, b3b_ref, delta_ref, h_ref, hid_sc):
    L = hid_sc.shape[0]
    pos = lax.broadcasted_iota(jnp.int32, (L, LANES), 0).astype(F32)
    t = pos / max(L - 1, 1)

    @pl.when(pl.program_id(0) == 0)
    def _():
        ang = (2.0 * math.pi * pos / L) * bands_ref[...]
        fr = fr_ref[...]
        pre = (t * w1t_ref[...]
               + jnp.dot(jnp.cos(ang), w1c_ref[...], preferred_element_type=F32, precision=HI)
               - jnp.dot(jnp.sin(ang), w1s_ref[...], preferred_element_type=F32, precision=HI)
               + b1_ref[...])
        hid = jnp.sin(fr * pre)
        hid_sc[...] = jnp.sin(fr * (jnp.dot(hid, w2_ref[...], preferred_element_type=F32, precision=HI)
                                    + b2_ref[...]))

    hid = hid_sc[...]
    window = jnp.exp(-t[:, 0:1] * delta_ref[...])
    hf = (jnp.dot(hid, w3f_ref[...], preferred_element_type=F32, precision=HI) + b3f_ref[...]) * window
    hb = (jnp.dot(hid, w3b_ref[...], preferred_element_type=F32, precision=HI) + b3b_ref[...]) * window
    norm = (jnp.sum(jnp.abs(hf), axis=0, keepdims=True) + jnp.sum(jnp.abs(hb), axis=0, keepdims=True)
            + FILT_EPS)
    row = lax.broadcasted_iota(jnp.int32, hb.shape, 0)
    h_ref[0] = (hf / norm).astype(h_ref.dtype)
    h_ref[1] = jnp.where(row == 0, 0.0, hb / norm).astype(h_ref.dtype)


def _pad2(a, rows, cols):
    return jnp.pad(a, ((0, rows - a.shape[0]), (0, cols - a.shape[1])))


def _implicit_filter(L, w1, b1, w2, b2, w3, b3, freq, tc):
    D = D_MODEL
    bands = _pad2(jnp.linspace(1e-4, FILT_BANDS - 1, FILT_BANDS, dtype=F32)[None], 1, LANES)
    deltas = jnp.abs(jnp.linspace(math.log(FILT_TARGET) / FILT_SLOW_PCT,
                                  math.log(FILT_TARGET) / FILT_FAST_PCT, D, dtype=F32))[None]
    w1 = w1.astype(F32)
    args = [bands,
            _pad2(w1[0:1], 1, LANES),
            _pad2(w1[1:1 + FILT_BANDS], LANES, LANES),
            _pad2(w1[1 + FILT_BANDS:], LANES, LANES),
            _pad2(b1[None], 1, LANES), _pad2(w2, LANES, LANES), _pad2(b2[None], 1, LANES),
            _pad2(freq[None], 1, LANES),
            _pad2(w3, LANES, 2 * D), _pad2(w3, LANES, 2 * D), b3[None], b3[None], deltas]
    small = lambda shape: pl.BlockSpec(shape, lambda c: (0, 0))
    nb = D // tc
    in_specs = [small((1, LANES)), small((1, LANES)), small((LANES, LANES)), small((LANES, LANES)),
                small((1, LANES)), small((LANES, LANES)), small((1, LANES)), small((1, LANES)),
                pl.BlockSpec((LANES, tc), lambda c: (0, c)),
                pl.BlockSpec((LANES, tc), lambda c: (0, nb + c)),
                pl.BlockSpec((1, tc), lambda c: (0, c)),
                pl.BlockSpec((1, tc), lambda c: (0, nb + c)),
                pl.BlockSpec((1, tc), lambda c: (0, c))]
    return pl.pallas_call(
        _filter_kernel,
        grid=(nb,),
        in_specs=in_specs,
        out_specs=pl.BlockSpec((2, L, tc), lambda c: (0, 0, c)),
        out_shape=jax.ShapeDtypeStruct((2, L, D), BF16),
        scratch_shapes=[pltpu.VMEM((L, LANES), F32)],
        compiler_params=_params("arbitrary"),
        name="hyena_filter",
    )(*args)


def _dft_gen_kernel(fre_ref, fim_ref, gim_ref, cb_sc, sb_sc, *, L):
    n = 2 * L
    tr = fre_ref.shape[0]
    step = 2.0 * math.pi / n
    i = pl.program_id(0)
    row = lax.broadcasted_iota(jnp.int32, (tr, L), 0)
    col = lax.broadcasted_iota(jnp.int32, (tr, L), 1)

    @pl.when(i == 0)
    def _():
        ang = ((row * col) & (n - 1)).astype(F32) * step
        cb_sc[...] = jnp.cos(ang)
        sb_sc[...] = jnp.sin(ang)

    col1 = lax.broadcasted_iota(jnp.int32, (1, L), 1)
    a = (((i * tr) * col1) & (n - 1)).astype(F32) * step
    ca, sa = jnp.cos(a), jnp.sin(a)
    cb, sb = cb_sc[...], sb_sc[...]
    c = cb * ca - sb * sa
    s = -(sb * ca + cb * sa)
    grow = row + i * tr
    fre_ref[...] = c.astype(fre_ref.dtype)
    fim_ref[...] = jnp.where(grow == 0, jnp.where((col & 1) == 0, 1.0, -1.0), s).astype(fim_ref.dtype)
    gim_ref[...] = jnp.where(col == 0, jnp.where((grow & 1) == 0, 1.0, -1.0), s).astype(gim_ref.dtype)


def _dft_tables(L, tr):
    spec = pl.BlockSpec((tr, L), lambda i: (i, 0))
    shape = jax.ShapeDtypeStruct((L, L), BF16)
    return pl.pallas_call(
        functools.partial(_dft_gen_kernel, L=L),
        grid=(L // tr,),
        out_specs=[spec, spec, spec],
        out_shape=[shape, shape, shape],
        scratch_shapes=[pltpu.VMEM((tr, L), F32), pltpu.VMEM((tr, L), F32)],
        compiler_params=_params("arbitrary"),
        name="dft_tables",
    )()


def _fwd_dft_kernel(*refs, with_filter):
    if with_filter:
        fre_ref, fim_ref, v_ref, kr_ref, ki_ref, y_ref = refs
    else:
        fre_ref, fim_ref, v_ref, y_ref = refs
    v = v_ref[...]
    vr = jnp.dot(fre_ref[...], v, preferred_element_type=F32)
    vi = jnp.dot(fim_ref[...], v, preferred_element_type=F32)
    if with_filter:
        kr, ki = kr_ref[...], ki_ref[...]
        tm = vr.shape[0]
        first = (lax.broadcasted_iota(jnp.int32, vr.shape, 0) + pl.program_id(1) * tm) == 0
        yr = vr * kr - jnp.where(first, 0.0, vi * ki)
        yi = jnp.where(first, vi * ki, vr * ki + vi * kr)
        y_ref[0] = yr.astype(y_ref.dtype)
        y_ref[1] = yi.astype(y_ref.dtype)
    else:
        y_ref[0] = vr.astype(y_ref.dtype)
        y_ref[1] = vi.astype(y_ref.dtype)


def _fwd_dft(fre, fim, v, kf, tm, out_dtype):
    B, L, D = v.shape
    ftile = pl.BlockSpec((tm, L), lambda b, m: (m, 0))
    in_specs = [ftile, ftile, pl.BlockSpec((None, L, D), lambda b, m: (b, 0, 0))]
    args = [fre, fim, v]
    if kf is not None:
        in_specs += [pl.BlockSpec((None, tm, D), lambda b, m: (0, m, 0)),
                     pl.BlockSpec((None, tm, D), lambda b, m: (1, m, 0))]
        args += [kf, kf]
    return pl.pallas_call(
        functools.partial(_fwd_dft_kernel, with_filter=kf is not None),
        grid=(B, L // tm),
        in_specs=in_specs,
        out_specs=pl.BlockSpec((None, 2, tm, D), lambda b, m: (b, 0, m, 0)),
        out_shape=jax.ShapeDtypeStruct((B, 2, L, D), out_dtype),
        compiler_params=_params("parallel", "parallel"),
        name="hyena_fwd_dft",
    )(*args)


def _filter_spectrum_kernel(hf_ref, hb_ref, k_ref, *, n):
    hfr, hfi = hf_ref[0], hf_ref[1]
    hbr, hbi = hb_ref[0], hb_ref[1]
    first = (lax.broadcasted_iota(jnp.int32, hfr.shape, 0) + pl.program_id(0) * hfr.shape[0]) == 0
    wgt = jnp.where(first, 1.0 / n, 2.0 / n)
    k_ref[0] = (hfr + hbr) * wgt
    k_ref[1] = jnp.where(first, hfi + hbi, hfi - hbi) * wgt


def _filter_spectrum(hspec, tm):
    _, _, L, D = hspec.shape
    return pl.pallas_call(
        functools.partial(_filter_spectrum_kernel, n=2 * L),
        grid=(L // tm,),
        in_specs=[pl.BlockSpec((None, 2, tm, D), lambda m: (0, 0, m, 0)),
                  pl.BlockSpec((None, 2, tm, D), lambda m: (1, 0, m, 0))],
        out_specs=pl.BlockSpec((2, tm, D), lambda m: (0, m, 0)),
        out_shape=jax.ShapeDtypeStruct((2, L, D), F32),
        compiler_params=_params("parallel"),
        name="hyena_filter_spectrum",
    )(hspec, hspec)


def _inv_dft_kernel(gre_ref, gim_ref, yr_ref, yi_ref, vx_ref, x0_ref, skip_ref, z_ref, acc_sc):
    kk = pl.program_id(2)

    @pl.when(kk == 0)
    def _():
        acc_sc[...] = jnp.zeros(acc_sc.shape, F32)

    acc_sc[...] += (jnp.dot(gre_ref[...], yr_ref[...], preferred_element_type=F32)
                    + jnp.dot(gim_ref[...], yi_ref[...], preferred_element_type=F32))

    @pl.when(kk == pl.num_programs(2) - 1)
    def _():
        z = (acc_sc[...] + vx_ref[...].astype(F32) * skip_ref[...]) * x0_ref[...]
        z_ref[...] = z.astype(z_ref.dtype)


def _inv_dft(gre, gim, y, vx, x0, skip, tm, tk):
    B, L, D = vx.shape
    gtile = pl.BlockSpec((tm, tk), lambda b, i, k: (i, k))
    row = pl.BlockSpec((None, tm, D), lambda b, i, k: (b, i, 0))
    return pl.pallas_call(
        _inv_dft_kernel,
        grid=(B, L // tm, L // tk),
        in_specs=[gtile, gtile,
                  pl.BlockSpec((None, None, tk, D), lambda b, i, k: (b, 0, k, 0)),
                  pl.BlockSpec((None, None, tk, D), lambda b, i, k: (b, 1, k, 0)),
                  row, row, pl.BlockSpec((1, D), lambda b, i, k: (0, 0))],
        out_specs=row,
        out_shape=jax.ShapeDtypeStruct((B, L, D), BF16),
        scratch_shapes=[pltpu.VMEM((tm, D), F32)],
        compiler_params=_params("parallel", "parallel", "arbitrary"),
        name="hyena_inv_dft",
    )(gre, gim, y, y, vx, x0, skip)


FFT_N1 = 64
FFT_SLOT_GROUP = 8
FFT_ROWS = 16


def _cos_units(idx, n):
    return jnp.cos((idx & (n - 1)).astype(F32) * (2.0 * math.pi / n))


def _fft_step1_tables_kernel(fb_ref, gb_ref, *, n1, n2):
    n = n1 * n2
    h1, quarter, R = n1 // 2, n // 4, FFT_ROWS
    lr, lh = R.bit_length() - 1, h1.bit_length() - 1
    g = pl.program_id(0)

    def entry(part, slot, t1):
        idx = jnp.where((part == 1) & (slot == 0), (n // 2) * t1, n2 * t1 * slot + quarter * part)
        return _cos_units(idx, n)

    rows, cols = fb_ref.shape
    r = lax.broadcasted_iota(jnp.int32, (rows, cols), 0) + g * rows
    c = lax.broadcasted_iota(jnp.int32, (rows, cols), 1)
    val = entry(r >> (lr + lh), (r >> lr) & (h1 - 1), c >> lr)
    fb_ref[...] = jnp.where((r & (R - 1)) == (c & (R - 1)), val, 0.0).astype(fb_ref.dtype)
    rows, cols = gb_ref.shape
    r = lax.broadcasted_iota(jnp.int32, (rows, cols), 0) + g * rows
    c = lax.broadcasted_iota(jnp.int32, (rows, cols), 1)
    slot = (c >> lr) & (h1 - 1)
    val = jnp.where(slot == 0, 1.0, 2.0) * entry(c >> (lr + lh), slot, r >> lr)
    gb_ref[...] = jnp.where((r & (R - 1)) == (c & (R - 1)), val, 0.0).astype(gb_ref.dtype)


def _fft_step2_tables_kernel(m3_ref, m3i_ref, *, n1, n2):
    n = n1 * n2
    quarter, half = n // 4, n2 // 2
    g = pl.program_id(0)
    r = lax.broadcasted_iota(jnp.int32, (2 * n2, 2 * n2), 0)
    c = lax.broadcasted_iota(jnp.int32, (2 * n2, 2 * n2), 1)
    rj, rim = r & (n2 - 1), (r >= n2).astype(jnp.int32)
    cj, cim = c & (n2 - 1), (c >= n2).astype(jnp.int32)
    for si in range(m3_ref.shape[0]):
        s = g * m3_ref.shape[0] + si
        fwd = _cos_units(cj * (s + n1 * rj) + quarter * (rim - cim), n)
        inv = _cos_units(rj * (s + n1 * cj) + quarter * (cim - rim), n)
        if si == 0:
            def slot0(j, t2, t_im, out_im):
                lo = (j < half) & (t_im == 0)
                hi = (j >= half) & (t_im == 1)
                idx0 = jnp.where((out_im == 1) & (j == 0), (n // 2) * t2, t2 * n1 * j + quarter * out_im)
                idxh = t2 * (n1 // 2 + n1 * (j - half)) + quarter * out_im
                return lo, hi, jnp.where(lo, idx0, idxh)
            lo, hi, idx = slot0(rj, cj, cim, rim)
            fwd0 = jnp.where(lo | hi, _cos_units(idx, n), 0.0)
            lo, hi, idx = slot0(cj, rj, rim, cim)
            amp = jnp.where(lo & (cj == 0), 1.0, 2.0)
            inv0 = jnp.where(lo | hi, amp * _cos_units(idx, n), 0.0)
            fwd = jnp.where(g == 0, fwd0, fwd)
            inv = jnp.where(g == 0, inv0, inv)
        m3_ref[si] = fwd.astype(m3_ref.dtype)
        m3i_ref[si] = inv.astype(m3i_ref.dtype)


def _fft_tables(n1, n2):
    h1, sg, R = n1 // 2, FFT_SLOT_GROUP, FFT_ROWS
    steps = 8
    fb, gb = pl.pallas_call(
        functools.partial(_fft_step1_tables_kernel, n1=n1, n2=n2),
        grid=(steps,),
        out_specs=[pl.BlockSpec((2 * h1 * R // steps, h1 * R), lambda g: (g, 0)),
                   pl.BlockSpec((h1 * R // steps, 2 * h1 * R), lambda g: (g, 0))],
        out_shape=[jax.ShapeDtypeStruct((2 * h1 * R, h1 * R), BF16),
                   jax.ShapeDtypeStruct((h1 * R, 2 * h1 * R), BF16)],
        compiler_params=_params("parallel"),
        name="fft_step1_tables",
    )()
    mspec = pl.BlockSpec((sg, 2 * n2, 2 * n2), lambda g: (g, 0, 0))
    m3, m3i = pl.pallas_call(
        functools.partial(_fft_step2_tables_kernel, n1=n1, n2=n2),
        grid=(h1 // sg,),
        out_specs=[mspec, mspec],
        out_shape=[jax.ShapeDtypeStruct((h1, 2 * n2, 2 * n2), BF16)] * 2,
        compiler_params=_params("parallel"),
        name="fft_step2_tables",
    )()
    return fb, gb, m3, m3i


def _fft_step1_kernel(fb_ref, x_ref, a_ref):
    h1, R, D = x_ref.shape
    a = jnp.dot(fb_ref[...], x_ref[...].reshape(h1 * R, D), preferred_element_type=F32)
    a_ref[...] = a.reshape(2, h1, R, D).astype(a_ref.dtype)


def _fft_step1(fb, x):
    B, L, D = x.shape
    R = FFT_ROWS
    h1 = fb.shape[1] // R
    n2 = L // h1
    return pl.pallas_call(
        _fft_step1_kernel,
        grid=(B, n2 // R),
        in_specs=[pl.BlockSpec(fb.shape, lambda b, c: (0, 0)),
                  pl.BlockSpec((None, h1, R, D), lambda b, c: (b, 0, c, 0))],
        out_specs=pl.BlockSpec((None, 2, h1, R, D), lambda b, c: (b, 0, 0, c, 0)),
        out_shape=jax.ShapeDtypeStruct((B, 2, h1, n2, D), BF16),
        compiler_params=_params("parallel", "parallel"),
        name="fft_step1",
    )(fb, x.reshape(B, h1, n2, D))


def _first_bin(shape, g):
    return (lax.broadcasted_iota(jnp.int32, shape, 0) == 0) & (g == 0)


def _fft_filter_spectrum_kernel(m3_ref, a_ref, k_ref, *, n):
    n2 = a_ref.shape[-2]
    g = pl.program_id(0)
    for si in range(m3_ref.shape[0]):
        hf = jnp.dot(m3_ref[si], jnp.concatenate([a_ref[0, 0, si], a_ref[0, 1, si]], axis=0),
                     preferred_element_type=F32)
        hb = jnp.dot(m3_ref[si], jnp.concatenate([a_ref[1, 0, si], a_ref[1, 1, si]], axis=0),
                     preferred_element_type=F32)
        k_ref[0, si] = (hf[:n2] + hb[:n2]) * (1.0 / n)
        ki = hf[n2:] - hb[n2:]
        if si == 0:
            ki = jnp.where(_first_bin(ki.shape, g), hf[n2:] + hb[n2:], ki)
        k_ref[1, si] = ki * (1.0 / n)


def _fft_filter_spectrum(m3, a):
    h1, n2 = m3.shape[0], m3.shape[1] // 2
    D = a.shape[-1]
    sg = FFT_SLOT_GROUP
    return pl.pallas_call(
        functools.partial(_fft_filter_spectrum_kernel, n=2 * h1 * n2),
        grid=(h1 // sg,),
        in_specs=[pl.BlockSpec((sg, 2 * n2, 2 * n2), lambda g: (g, 0, 0)),
                  pl.BlockSpec((2, 2, sg, n2, D), lambda g: (0, 0, g, 0, 0))],
        out_specs=pl.BlockSpec((2, sg, n2, D), lambda g: (0, g, 0, 0)),
        out_shape=jax.ShapeDtypeStruct((2, h1, n2, D), F32),
        compiler_params=_params("parallel"),
        name="fft_filter_spectrum",
    )(m3, a)


def _fft_step2_kernel(m3_ref, m3i_ref, kr_ref, ki_ref, a_ref, b_ref, y_sc):
    n2 = a_ref.shape[-2]
    g = pl.program_id(0)
    sg = m3_ref.shape[0]
    for si in range(sg):
        xh = jnp.dot(m3_ref[si], a_ref[:, si].reshape(2 * n2, a_ref.shape[-1]), preferred_element_type=F32)
        xr, xi = xh[:n2], xh[n2:]
        kr, ki = kr_ref[si], ki_ref[si]
        yr = xr * kr - xi * ki
        yi = xr * ki + xi * kr
        if si == 0:
            first = _first_bin(yr.shape, g)
            yr = jnp.where(first, xr * kr, yr)
            yi = jnp.where(first, xi * ki, yi)
        y_sc[si, :n2] = yr.astype(BF16)
        y_sc[si, n2:] = yi.astype(BF16)
    for si in range(sg):
        bb = jnp.dot(m3i_ref[si], y_sc[si], preferred_element_type=F32)
        b_ref[0, si] = bb[:n2].astype(b_ref.dtype)
        b_ref[1, si] = bb[n2:].astype(b_ref.dtype)


def _fft_step2(m3, m3i, kf, a):
    B, _, h1, n2, D = a.shape
    sg = FFT_SLOT_GROUP
    mspec = pl.BlockSpec((sg, 2 * n2, 2 * n2), lambda g, b: (g, 0, 0))
    blk = pl.BlockSpec((None, 2, sg, n2, D), lambda g, b: (b, 0, g, 0, 0))
    return pl.pallas_call(
        _fft_step2_kernel,
        grid=(h1 // sg, B),
        in_specs=[mspec, mspec,
                  pl.BlockSpec((None, sg, n2, D), lambda g, b: (0, g, 0, 0)),
                  pl.BlockSpec((None, sg, n2, D), lambda g, b: (1, g, 0, 0)),
                  blk],
        out_specs=blk,
        out_shape=jax.ShapeDtypeStruct((B, 2, h1, n2, D), BF16),
        scratch_shapes=[pltpu.VMEM((sg, 2 * n2, D), BF16)],
        compiler_params=_params("parallel", "parallel"),
        name="fft_step2",
    )(m3, m3i, kf, kf, a)


def _fft_inv_step1_kernel(gb_ref, b_ref, vx_ref, x0_ref, skip_ref, z_ref):
    _, h1, R, D = b_ref.shape
    y = jnp.dot(gb_ref[...], b_ref[...].reshape(2 * h1 * R, D), preferred_element_type=F32)
    z = (y.reshape(h1, R, D) + vx_ref[...].astype(F32) * skip_ref[...]) * x0_ref[...]
    z_ref[...] = z.astype(z_ref.dtype)


def _fft_inv_step1(gb, bm, vx, x0, skip):
    B, L, D = vx.shape
    R = FFT_ROWS
    h1 = gb.shape[0] // R
    n2 = L // h1
    blk = pl.BlockSpec((None, h1, R, D), lambda b, c: (b, 0, c, 0))
    z = pl.pallas_call(
        _fft_inv_step1_kernel,
        grid=(B, n2 // R),
        in_specs=[pl.BlockSpec(gb.shape, lambda b, c: (0, 0)),
                  pl.BlockSpec((None, 2, h1, R, D), lambda b, c: (b, 0, 0, c, 0)),
                  blk, blk, pl.BlockSpec((1, D), lambda b, c: (0, 0))],
        out_specs=blk,
        out_shape=jax.ShapeDtypeStruct((B, h1, n2, D), BF16),
        compiler_params=_params("parallel", "parallel"),
        name="fft_inv_step1",
    )(gb, bm, vx.reshape(B, h1, n2, D), x0.reshape(B, h1, n2, D), skip)
    return z.reshape(B, L, D)


def _rope_tables(L):
    pos = jnp.arange(L, dtype=jnp.int32)
    rows = (pos // GRID_W).astype(F32)
    cols = (pos % GRID_W).astype(F32)
    inv = ROPE_BASE ** (-jnp.arange(ROPE_PAIRS, dtype=F32) / ROPE_PAIRS)
    lane = jnp.arange(LANES)
    within = lane % HEAD_DIM
    axis = within // (2 * ROPE_PAIRS)
    half = (within % (2 * ROPE_PAIRS)) // ROPE_PAIRS
    ang = jnp.where(axis[None, :] == 0, rows[:, None], cols[:, None]) * inv[within % ROPE_PAIRS][None, :]
    cos, sin = jnp.cos(ang), jnp.sin(ang)
    sin_a = jnp.where(half[None, :] == 0, -sin, 0.0)
    sin_b = jnp.where(half[None, :] == 1, sin, 0.0)
    return cos, sin_a, sin_b


def _tile(L, pref):
    return min(L, pref)


def _four_step(L):
    return (2 * L) // FFT_N1 >= 32


def _conv_tables(L):
    return _fft_tables(FFT_N1, 2 * L // FFT_N1) if _four_step(L) else _dft_tables(L, _tile(L, DFT_ROW_TILE))


def _run_group(x, rowmap, shared_mod, mods, p, cache, rope_tabs, dft, kv_dtype):
    B, L, D = x.shape
    tm = _tile(L, ROW_TILE)
    rows = (1, B * L, D) if shared_mod else (B, L, D)
    tmr = _tile(rows[1], ROW_TILE)
    ctx_k, ctx_v = [], []
    for i in range(DEPTH):
        j = i // 2
        sh1, sc1, g1, sh2, sc2, g2 = (mods[i, :, s * D:(s + 1) * D].reshape(MOD_ROWS, 1, D) for s in range(6))
        n1 = p["norm1_g"][i][None]
        if i % 2 == 0:
            lam_init = 0.8 - 0.6 * math.exp(-0.3 * i)
            q, k, v = (a.reshape(B, L, D) for a in
                       _qkv(x.reshape(rows), n1, sh1, sc1, p["attn_w_qkv"][j], rowmap, rope_tabs, kv_dtype, tmr))
            ctx_k.append(k)
            ctx_v.append(v)
            mix = _attention(q, k, v, None if cache is None else cache + (j,),
                             p["attn_lambda"][j].astype(F32), p["attn_subln_g"][j][None],
                             lam_init, _tile(L, ATTN_TILE), _tile(L, ATTN_TILE),
                             N_HEADS if L <= ATTN_TILE else 1)
            w_o, b_o = p["attn_w_o"][j], None
        else:
            vx, x0 = _hyena_in(x, n1, sh1, sc1, p["hy_w_in"][j], p["hy_b_in"][j][None],
                               p["hy_conv_w"][j], p["hy_conv_b"][j][None], rowmap, tm)
            filt = _implicit_filter(L, p["filt_w1"][j], p["filt_b1"][j], p["filt_w2"][j], p["filt_b2"][j],
                                    p["filt_w3"][j], p["filt_b3"][j], p["filt_freq"][j], FILT_COL_TILE)
            skip = p["hy_skip"][j][None]
            if _four_step(L):
                fb, gb, m3, m3i = dft
                kf = _fft_filter_spectrum(m3, _fft_step1(fb, filt))
                bm = _fft_step2(m3, m3i, kf, _fft_step1(fb, vx))
                mix = _fft_inv_step1(gb, bm, vx, x0, skip)
            else:
                fre, fim, gim = dft
                tmf = _tile(L, DFT_ROW_TILE)
                kf = _filter_spectrum(_fwd_dft(fre, fim, filt, None, tmf, F32), tmf)
                y = _fwd_dft(fre, fim, vx, kf, tmf, BF16)
                mix = _inv_dft(fre, gim, y, vx, x0, skip, tm, _tile(L, ROW_TILE))
            w_o, b_o = p["hy_w_out"][j], p["hy_b_out"][j][None]
        final_g = p["final_g"][None] if i == DEPTH - 1 else None
        x = _out_ffn(mix.reshape(rows), w_o, b_o, x.reshape(rows), g1, p["norm2_g"][i][None], sh2, sc2, g2,
                     p["ffn_w_gu"][i], p["ffn_w_down"][i], final_g, rowmap, tmr).reshape(B, L, D)
    return x, ctx_k, ctx_v


def kernel(x_prompt, x_sample, cache_k, cache_v, c, c_ctx, ada_w, ada_b, norm1_g, norm2_g, attn_w_qkv, attn_lambda, attn_subln_g, attn_w_o, hy_w_in, hy_b_in, hy_conv_w, hy_conv_b, filt_w1, filt_b1, filt_w2, filt_b2, filt_w3, filt_b3, filt_freq, hy_skip, hy_w_out, hy_b_out, ffn_w_gu, ffn_w_down, final_g):
    D = D_MODEL
    nb = c.shape[0]
    cmat = jnp.concatenate([c, c_ctx[None], jnp.zeros((MOD_ROWS - nb - 1, D), F32)], axis=0)
    mods = _adaln_all(cmat, ada_w, ada_b)

    p = dict(norm1_g=norm1_g, norm2_g=norm2_g, attn_lambda=attn_lambda, attn_subln_g=attn_subln_g,
             hy_b_in=hy_b_in, hy_conv_w=hy_conv_w, hy_conv_b=hy_conv_b,
             filt_w1=filt_w1, filt_b1=filt_b1, filt_w2=filt_w2, filt_b2=filt_b2, filt_w3=filt_w3,
             filt_b3=filt_b3, filt_freq=filt_freq, hy_skip=hy_skip, hy_b_out=hy_b_out, final_g=final_g,
             attn_w_qkv=attn_w_qkv.astype(BF16), attn_w_o=attn_w_o.astype(BF16),
             hy_w_in=hy_w_in.astype(BF16), hy_w_out=hy_w_out.astype(BF16),
             ffn_w_gu=ffn_w_gu.astype(BF16), ffn_w_down=ffn_w_down.astype(BF16))

    Lp, Ls = x_prompt.shape[1], x_sample.shape[1]
    y_prompt, ctx_k, ctx_v = _run_group(x_prompt, lambda b: CTX_ROW, True, mods, p, None, None,
                                        _conv_tables(Lp), F32)
    cache = tuple(a.reshape(a.shape[0], a.shape[1], a.shape[2], D) for a in (cache_k, cache_v))
    y_sample, _, _ = _run_group(x_sample, lambda b: b, False, mods, p, cache, _rope_tables(Ls),
                                _conv_tables(Ls), BF16)

    Bp = x_prompt.shape[0]
    new_k = jnp.stack(ctx_k, axis=1).reshape(Bp, len(ctx_k), Lp, N_HEADS, 2, HEAD_DIM)
    new_v = jnp.stack(ctx_v, axis=1).reshape(Bp, len(ctx_v), Lp, N_HEADS, V_DIM)
    return y_prompt, y_sample, new_k, new_v
```

```python
import functools
import math

import jax
import jax.numpy as jnp
from jax import lax
from jax.experimental import pallas as pl
from jax.experimental.pallas import tpu as pltpu

D_MODEL = 1024
DEPTH = 4
GRID_W = 64
N_HEADS = 8
HEAD_DIM = 64
V_DIM = 2 * HEAD_DIM
ROPE_PAIRS = HEAD_DIM // 4
ROPE_BASE = 10000.0
FILT_BANDS = 16
FILT_ORDER = 64
FILT_TARGET = 1e-2
FILT_FAST_PCT = 0.3
FILT_SLOW_PCT = 1.5
FILT_EPS = 1e-6
D_FF = 2816
EPS = 1e-6
SUBLN_EPS = 1e-5

LANES = 128
MOD_ROWS = 16
CTX_ROW = 8
VMEM_LIMIT = 56 * 1024 * 1024

ROW_TILE = 512
ATTN_TILE = 512
DFT_ROW_TILE = 256
FILT_COL_TILE = 256
COL_CHUNK = 512
FFN_CHUNK = 256
assert D_FF % FFN_CHUNK == 0 and D_MODEL % COL_CHUNK == 0

F32 = jnp.float32
BF16 = jnp.bfloat16


def _dot_split(a, b):
    a_hi, b_hi = a.astype(BF16), b.astype(BF16)
    a_lo = (a - a_hi.astype(F32)).astype(BF16)
    b_lo = (b - b_hi.astype(F32)).astype(BF16)
    dot = functools.partial(jnp.dot, preferred_element_type=F32)
    return dot(a_hi, b_hi) + (dot(a_hi, b_lo) + dot(a_lo, b_hi))


def _params(*sem):
    return pltpu.CompilerParams(dimension_semantics=sem, vmem_limit_bytes=VMEM_LIMIT)


def _lane_tile(x, reps):
    return x if reps == 1 else jnp.concatenate([x] * reps, axis=1)


def _norm_mod(x, g, shift, scale):
    ms = jnp.mean(x * x, axis=-1, keepdims=True)
    return (x * lax.rsqrt(ms + EPS) * g) * (1.0 + scale) + shift


def _mod_kernel(c_ref, w_ref, b_ref, o_ref):
    c = c_ref[...]
    a = c * jax.nn.sigmoid(c)
    o_ref[...] = _dot_split(a, w_ref[...]) + b_ref[...]


def _adaln_all(cmat, ada_w, ada_b):
    n6 = 6 * D_MODEL
    tn = 1536
    return pl.pallas_call(
        _mod_kernel,
        grid=(DEPTH, n6 // tn),
        in_specs=[
            pl.BlockSpec((MOD_ROWS, D_MODEL), lambda i, n: (0, 0)),
            pl.BlockSpec((None, D_MODEL, tn), lambda i, n: (i, 0, n)),
            pl.BlockSpec((None, 1, tn), lambda i, n: (i, 0, n)),
        ],
        out_specs=pl.BlockSpec((None, MOD_ROWS, tn), lambda i, n: (i, 0, n)),
        out_shape=jax.ShapeDtypeStruct((DEPTH, MOD_ROWS, n6), F32),
        compiler_params=_params("parallel", "parallel"),
        name="adaln",
    )(cmat, ada_w, ada_b.reshape(DEPTH, 1, n6))


def _rope_chunk(x, cos, sin_a, sin_b):
    return x * cos + pltpu.roll(x, LANES - ROPE_PAIRS, 1) * sin_a + pltpu.roll(x, ROPE_PAIRS, 1) * sin_b


def _qkv_kernel(*refs, rope, tn):
    if rope:
        x_ref, g_ref, sh_ref, sc_ref, w_ref, cos_ref, sa_ref, sb_ref, q_ref, k_ref, v_ref = refs
    else:
        x_ref, g_ref, sh_ref, sc_ref, w_ref, q_ref, k_ref, v_ref = refs
    h = _norm_mod(x_ref[...], g_ref[...], sh_ref[...], sc_ref[...]).astype(BF16)
    outs = (q_ref, k_ref, v_ref)
    qk_scale = HEAD_DIM ** -0.5 * math.log2(math.e)
    for part in range(3):
        for n in range(D_MODEL // tn):
            col = part * D_MODEL + n * tn
            y = jnp.dot(h, w_ref[:, col:col + tn], preferred_element_type=F32)
            for j in range(tn // LANES):
                yj = y[:, j * LANES:(j + 1) * LANES]
                if rope and part < 2:
                    yj = _rope_chunk(yj, cos_ref[...], sa_ref[...], sb_ref[...])
                if part == 0:
                    yj = yj * qk_scale
                lo = n * tn + j * LANES
                outs[part][:, lo:lo + LANES] = yj.astype(outs[part].dtype)


def _qkv(x, g, shift, scale, w, rowmap, rope_tabs, kv_dtype, tm):
    B, L, D = x.shape
    rope = rope_tabs is not None
    row = pl.BlockSpec((None, tm, D), lambda b, i: (b, i, 0))
    vec = pl.BlockSpec((None, 1, D), lambda b, i: (rowmap(b), 0, 0))
    in_specs = [row, pl.BlockSpec((1, D), lambda b, i: (0, 0)), vec, vec,
                pl.BlockSpec((D, 3 * D), lambda b, i: (0, 0))]
    args = [x, g, shift, scale, w]
    if rope:
        tab = pl.BlockSpec((tm, LANES), lambda b, i: (i, 0))
        in_specs += [tab, tab, tab]
        args += list(rope_tabs)
    return pl.pallas_call(
        functools.partial(_qkv_kernel, rope=rope, tn=COL_CHUNK),
        grid=(B, L // tm),
        in_specs=in_specs,
        out_specs=[row, row, row],
        out_shape=[jax.ShapeDtypeStruct((B, L, D), BF16),
                   jax.ShapeDtypeStruct((B, L, D), kv_dtype),
                   jax.ShapeDtypeStruct((B, L, D), kv_dtype)],
        compiler_params=_params("parallel", "parallel"),
        name="qkv",
    )(*args)


def _hyena_in_kernel(xp_ref, x_ref, xn_ref, g_ref, sh_ref, sc_ref, w_ref, b_ref, cw_ref, cb_ref,
                     vx_ref, x0_ref, u_sc, *, tn):
    tm, D = x_ref.shape
    i = pl.program_id(1)
    xa = jnp.concatenate([xp_ref[...], x_ref[...], xn_ref[...]], axis=0)
    h = _norm_mod(xa, g_ref[...], sh_ref[...], sc_ref[...]).astype(BF16)
    keep_top = (i > 0).astype(F32)
    keep_bot = (i < pl.num_programs(1) - 1).astype(F32)

    def part(p, n, slot):
        sl = slice(p * D + n * tn, p * D + (n + 1) * tn)
        u = jnp.dot(h, w_ref[:, sl], preferred_element_type=F32) + b_ref[:, sl]
        u_sc[slot, 0:8] = u[:8] * keep_top
        u_sc[slot, 8:8 + tm] = u[8:8 + tm]
        u_sc[slot, 8 + tm:] = u[8 + tm:] * keep_bot
        w = cw_ref[:, sl]
        return (u_sc[slot, 7:7 + tm] * w[0:1] + u_sc[slot, 8:8 + tm] * w[1:2] + u_sc[slot, 9:9 + tm] * w[2:3]
                + cb_ref[:, sl])

    for n in range(D // tn):
        out = slice(n * tn, (n + 1) * tn)
        x0_ref[:, out] = part(0, n, 0)
        vx_ref[:, out] = (part(2, n, 1) * part(1, n, 2)).astype(vx_ref.dtype)


def _hyena_in(x, g, shift, scale, w, b, conv_w, conv_b, rowmap, tm):
    B, L, D = x.shape
    N = w.shape[1]
    hb = tm // 8
    tn = COL_CHUNK
    vec = pl.BlockSpec((None, 1, D), lambda b_, i: (rowmap(b_), 0, 0))
    row = pl.BlockSpec((None, tm, D), lambda b_, i: (b_, i, 0))
    const = lambda r: pl.BlockSpec((r, N), lambda b_, i: (0, 0))
    return pl.pallas_call(
        functools.partial(_hyena_in_kernel, tn=tn),
        grid=(B, L // tm),
        in_specs=[pl.BlockSpec((None, 8, D), lambda b_, i: (b_, jnp.maximum(i * hb - 1, 0), 0)),
                  row,
                  pl.BlockSpec((None, 8, D), lambda b_, i: (b_, jnp.minimum((i + 1) * hb, L // 8 - 1), 0)),
                  pl.BlockSpec((1, D), lambda b_, i: (0, 0)), vec, vec,
                  pl.BlockSpec((D, N), lambda b_, i: (0, 0)), const(1), const(3), const(1)],
        out_specs=[row, row],
        out_shape=[jax.ShapeDtypeStruct((B, L, D), BF16), jax.ShapeDtypeStruct((B, L, D), F32)],
        scratch_shapes=[pltpu.VMEM((3, tm + 16, tn), F32)],
        compiler_params=_params("parallel", "parallel"),
        name="hyena_in",
    )(x, x, x, g, shift, scale, w, b, conv_w, conv_b)


def _attn_kernel(*refs, heads, **static):
    n_scratch = 7
    lam_ref, g_ref = refs[:2]
    for h in range(heads):
        cols = slice(h * LANES, (h + 1) * LANES)
        per_head = [r.at[:, cols] for r in refs[2:len(refs) - n_scratch]]
        _attn_head(lam_ref, g_ref, *per_head, *refs[len(refs) - n_scratch:], **static)


def _attn_head(*refs, tq, tk, lam_init, cached):
    if cached:
        lam_ref, g_ref, q_ref, ck_ref, cv_ref, k_ref, v_ref, o_ref = refs[:8]
    else:
        lam_ref, g_ref, q_ref, k_ref, v_ref, o_ref = refs[:6]
    m_sc, acc_sc, vx_sc, q2_sc, s0_sc, s1_sc, s2_sc = refs[-7:]
    nc = ck_ref.shape[0] // tk if cached else 0
    nk = nc + k_ref.shape[0] // tk
    nq = q_ref.shape[0] // tq
    if cached:
        vx_sc[:nc * tk, :V_DIM] = cv_ref[...].astype(BF16)
    vx_sc[nc * tk:, :V_DIM] = v_ref[...].astype(BF16)
    vx_sc[:, V_DIM:] = jnp.ones((vx_sc.shape[0], V_DIM), BF16)
    lp = lam_ref[...]
    lam = (jnp.exp(jnp.sum(lp[0:1] * lp[1:2], axis=1, keepdims=True))
           - jnp.exp(jnp.sum(lp[2:3] * lp[3:4], axis=1, keepdims=True)) + lam_init)
    lane = lax.broadcasted_iota(jnp.int32, (tq, LANES), 1)

    def slot(j):
        if j == 0:
            return s0_sc
        return s1_sc if j % 2 == 1 else s2_sc

    def key_tile(j):
        if j < nc:
            return ck_ref[j * tk:(j + 1) * tk, :].astype(BF16)
        return k_ref[(j - nc) * tk:(j - nc + 1) * tk, :].astype(BF16)

    def load_q(i):
        q = q_ref[pl.ds(pl.multiple_of(i * tq, tq), tq), :]
        zero = jnp.zeros_like(q)
        q2_sc[:tq] = jnp.where(lane < HEAD_DIM, q, zero)
        q2_sc[tq:] = jnp.where(lane >= HEAD_DIM, q, zero)

    def scores(j):
        s_sc = slot(j)
        s_sc[...] = lax.dot_general(q2_sc[...], key_tile(j), (((1,), (1,)), ((), ())),
                                    preferred_element_type=F32)

    def accumulate(j):
        s_sc = slot(j)
        t = jnp.broadcast_to(jnp.max(s_sc[...], axis=1, keepdims=True), m_sc.shape)
        if j == 0:
            m_new = t
        else:
            m_prev = m_sc[...]
            m_new = jnp.maximum(m_prev, t)
            alpha = jnp.exp2(m_prev - m_new)
        e = jnp.exp2(s_sc[...] - _lane_tile(m_new, tk // LANES))
        pv = jnp.dot(e.astype(BF16), vx_sc[j * tk:(j + 1) * tk, :], preferred_element_type=F32)
        acc_sc[...] = pv if j == 0 else _lane_tile(alpha, 2) * acc_sc[...] + pv
        m_sc[...] = m_new

    def finalize(i):
        o = (acc_sc[:tq, :V_DIM] / acc_sc[:tq, V_DIM:] - lam * (acc_sc[tq:, :V_DIM] / acc_sc[tq:, V_DIM:]))
        ms = jnp.mean(o * o, axis=-1, keepdims=True)
        y = (o * lax.rsqrt(ms + SUBLN_EPS) * g_ref[...]) * (1.0 - lam_init)
        o_ref[pl.ds(pl.multiple_of(i * tq, tq), tq), :] = y.astype(o_ref.dtype)

    def query_tile(i, prefetch):
        for j in range(1, nk):
            accumulate(j - 1)
            scores(j)
        if prefetch:
            load_q(i + 1)
        accumulate(nk - 1)
        if prefetch:
            scores(0)
        finalize(i)

    load_q(0)
    scores(0)

    def body(i, carry):
        query_tile(i, True)
        return carry
    lax.fori_loop(0, nq - 1, body, 0)
    query_tile(nq - 1, False)


def _attention(q, k, v, cache, lam_params, subln_g, lam_init, tq, tk, heads):
    B, Lq, D = q.shape
    Lk = k.shape[1]
    width = heads * LANES
    kv = pl.BlockSpec((None, Lk, width), lambda b, h: (b, 0, h))
    qo = pl.BlockSpec((None, Lq, width), lambda b, h: (b, 0, h))
    in_specs = [pl.BlockSpec((4, HEAD_DIM), lambda b, h: (0, 0)),
                pl.BlockSpec((1, V_DIM), lambda b, h: (0, 0)), qo]
    args = [lam_params, subln_g, q]
    past = 0
    if cache is not None:
        ck, cv, layer = cache
        past = ck.shape[2]
        cspec = pl.BlockSpec((None, None, past, width), lambda b, h: (b, layer, 0, h))
        in_specs += [cspec, cspec]
        args += [ck, cv]
    in_specs += [kv, kv]
    args += [k, v]
    return pl.pallas_call(
        functools.partial(_attn_kernel, heads=heads, tq=tq, tk=tk, lam_init=lam_init,
                          cached=cache is not None),
        grid=(B, N_HEADS // heads),
        in_specs=in_specs,
        out_specs=qo,
        out_shape=jax.ShapeDtypeStruct((B, Lq, D), BF16),
        scratch_shapes=[pltpu.VMEM((2 * tq, LANES), F32),
                        pltpu.VMEM((2 * tq, 2 * V_DIM), F32),
                        pltpu.VMEM((past + Lk, 2 * V_DIM), BF16),
                        pltpu.VMEM((2 * tq, LANES), BF16)]
                       + [pltpu.VMEM((2 * tq, tk), F32)] * 3,
        compiler_params=_params("parallel", "parallel"),
        name="diff_attention",
    )(*args)


def _resident(shape):
    return pl.BlockSpec(shape, lambda b, i: (0,) * len(shape), pipeline_mode=pl.Buffered(1))


def _out_ffn_kernel(*refs, fc, final, bias):
    refs = list(refs)
    o_ref = refs.pop()
    fg_ref = refs.pop() if final else None
    a_ref, wo_ref = refs[:2]
    bo_ref = refs[2] if bias else None
    x_ref, gate1_ref, g_ref, sh_ref, sc_ref, gate2_ref, wgu_ref, wd_ref = refs[2 + bias:]
    y = jnp.dot(a_ref[...], wo_ref[...], preferred_element_type=F32)
    if bias:
        y = y + bo_ref[...]
    x = x_ref[...] + gate1_ref[...] * y
    h = _norm_mod(x, g_ref[...], sh_ref[...], sc_ref[...]).astype(BF16)
    acc = jnp.zeros(x.shape, F32)
    for f in range(D_FF // fc):
        gg = jnp.dot(h, wgu_ref[:, f * fc:(f + 1) * fc], preferred_element_type=F32)
        uu = jnp.dot(h, wgu_ref[:, D_FF + f * fc:D_FF + (f + 1) * fc], preferred_element_type=F32)
        act = (gg * jax.nn.sigmoid(gg)) * uu
        acc = acc + jnp.dot(act.astype(BF16), wd_ref[f * fc:(f + 1) * fc, :], preferred_element_type=F32)
    out = x + gate2_ref[...] * acc
    if final:
        ms = jnp.mean(out * out, axis=-1, keepdims=True)
        out = out * lax.rsqrt(ms + EPS) * fg_ref[...]
    o_ref[...] = out


def _out_ffn(a, w_o, b_o, x, gate1, g, shift, scale, gate2, w_gu, w_down, final_g, rowmap, tm):
    B, L, D = x.shape
    final, bias = final_g is not None, b_o is not None
    row = pl.BlockSpec((None, tm, D), lambda b, i: (b, i, 0))
    vec = pl.BlockSpec((None, 1, D), lambda b, i: (rowmap(b), 0, 0))
    const = pl.BlockSpec((1, D), lambda b, i: (0, 0))
    in_specs = [row, _resident((D, D))] + ([const] if bias else [])
    args = [a, w_o] + ([b_o] if bias else [])
    in_specs += [row, vec, const, vec, vec, vec, _resident((D, 2 * D_FF)), _resident((D_FF, D))]
    args += [x, gate1, g, shift, scale, gate2, w_gu, w_down]
    if final:
        in_specs.append(const)
        args.append(final_g)
    return pl.pallas_call(
        functools.partial(_out_ffn_kernel, fc=FFN_CHUNK, final=final, bias=bias),
        grid=(B, L // tm),
        in_specs=in_specs,
        out_specs=row,
        out_shape=jax.ShapeDtypeStruct((B, L, D), F32),
        compiler_params=_params("parallel", "parallel"),
        name="out_ffn",
    )(*args)


def _filter_kernel(bands_ref, w1t_ref, w1c_ref, w1s_ref, b1_ref, w2_ref, b2_ref, fr_ref,
                   w3f_ref, w3b_ref, b3f_ref, b3b_ref, delta_ref, h_ref, hid_sc):
    L = hid_sc.shape[0]
    pos = lax.broadcasted_iota(jnp.int32, (L, LANES), 0).astype(F32)
    t = pos / max(L - 1, 1)

    @pl.when(pl.program_id(0) == 0)
    def _():
        ang = (2.0 * math.pi * pos / L) * bands_ref[...]
        fr = fr_ref[...]
        pre = (t * w1t_ref[...]
               + _dot_split(jnp.cos(ang), w1c_ref[...])
               - _dot_split(jnp.sin(ang), w1s_ref[...])
               + b1_ref[...])
        hid = jnp.sin(fr * pre)
        hid_sc[...] = jnp.sin(fr * (_dot_split(hid, w2_ref[...])
                                    + b2_ref[...]))

    hid = hid_sc[...]
    window = jnp.exp(-t[:, 0:1] * delta_ref[...])
    hf = (_dot_split(hid, w3f_ref[...]) + b3f_ref[...]) * window
    hb = (_dot_split(hid, w3b_ref[...]) + b3b_ref[...]) * window
    norm = (jnp.sum(jnp.abs(hf), axis=0, keepdims=True) + jnp.sum(jnp.abs(hb), axis=0, keepdims=True)
            + FILT_EPS)
    row = lax.broadcasted_iota(jnp.int32, hb.shape, 0)
    h_ref[0] = (hf / norm).astype(h_ref.dtype)
    h_ref[1] = jnp.where(row == 0, 0.0, hb / norm).astype(h_ref.dtype)


def _pad2(a, rows, cols):
    return jnp.pad(a, ((0, rows - a.shape[0]), (0, cols - a.shape[1])))


def _implicit_filter(L, w1, b1, w2, b2, w3, b3, freq, tc):
    D = D_MODEL
    bands = _pad2(jnp.linspace(1e-4, FILT_BANDS - 1, FILT_BANDS, dtype=F32)[None], 1, LANES)
    deltas = jnp.abs(jnp.linspace(math.log(FILT_TARGET) / FILT_SLOW_PCT,
                                  math.log(FILT_TARGET) / FILT_FAST_PCT, D, dtype=F32))[None]
    w1 = w1.astype(F32)
    args = [bands,
            _pad2(w1[0:1], 1, LANES),
            _pad2(w1[1:1 + FILT_BANDS], LANES, LANES),
            _pad2(w1[1 + FILT_BANDS:], LANES, LANES),
            _pad2(b1[None], 1, LANES), _pad2(w2, LANES, LANES), _pad2(b2[None], 1, LANES),
            _pad2(freq[None], 1, LANES),
            _pad2(w3, LANES, 2 * D), _pad2(w3, LANES, 2 * D), b3[None], b3[None], deltas]
    small = lambda shape: pl.BlockSpec(shape, lambda c: (0, 0))
    nb = D // tc
    in_specs = [small((1, LANES)), small((1, LANES)), small((LANES, LANES)), small((LANES, LANES)),
                small((1, LANES)), small((LANES, LANES)), small((1, LANES)), small((1, LANES)),
                pl.BlockSpec((LANES, tc), lambda c: (0, c)),
                pl.BlockSpec((LANES, tc), lambda c: (0, nb + c)),
                pl.BlockSpec((1, tc), lambda c: (0, c)),
                pl.BlockSpec((1, tc), lambda c: (0, nb + c)),
                pl.BlockSpec((1, tc), lambda c: (0, c))]
    return pl.pallas_call(
        _filter_kernel,
        grid=(nb,),
        in_specs=in_specs,
        out_specs=pl.BlockSpec((2, L, tc), lambda c: (0, 0, c)),
        out_shape=jax.ShapeDtypeStruct((2, L, D), BF16),
        scratch_shapes=[pltpu.VMEM((L, LANES), F32)],
        compiler_params=_params("arbitrary"),
        name="hyena_filter",
    )(*args)


def _dft_gen_kernel(fre_ref, fim_ref, gim_ref, cb_sc, sb_sc, *, L):
    n = 2 * L
    tr = fre_ref.shape[0]
    step = 2.0 * math.pi / n
    i = pl.program_id(0)
    row = lax.broadcasted_iota(jnp.int32, (tr, L), 0)
    col = lax.broadcasted_iota(jnp.int32, (tr, L), 1)

    @pl.when(i == 0)
    def _():
        ang = ((row * col) & (n - 1)).astype(F32) * step
        cb_sc[...] = jnp.cos(ang)
        sb_sc[...] = jnp.sin(ang)

    col1 = lax.broadcasted_iota(jnp.int32, (1, L), 1)
    a = (((i * tr) * col1) & (n - 1)).astype(F32) * step
    ca, sa = jnp.cos(a), jnp.sin(a)
    cb, sb = cb_sc[...], sb_sc[...]
    c = cb * ca - sb * sa
    s = -(sb * ca + cb * sa)
    grow = row + i * tr
    fre_ref[...] = c.astype(fre_ref.dtype)
    fim_ref[...] = jnp.where(grow == 0, jnp.where((col & 1) == 0, 1.0, -1.0), s).astype(fim_ref.dtype)
    gim_ref[...] = jnp.where(col == 0, jnp.where((grow & 1) == 0, 1.0, -1.0), s).astype(gim_ref.dtype)


def _dft_tables(L, tr):
    spec = pl.BlockSpec((tr, L), lambda i: (i, 0))
    shape = jax.ShapeDtypeStruct((L, L), BF16)
    return pl.pallas_call(
        functools.partial(_dft_gen_kernel, L=L),
        grid=(L // tr,),
        out_specs=[spec, spec, spec],
        out_shape=[shape, shape, shape],
        scratch_shapes=[pltpu.VMEM((tr, L), F32), pltpu.VMEM((tr, L), F32)],
        compiler_params=_params("arbitrary"),
        name="dft_tables",
    )()


def _fwd_dft_kernel(*refs, with_filter):
    if with_filter:
        fre_ref, fim_ref, v_ref, kr_ref, ki_ref, y_ref = refs
    else:
        fre_ref, fim_ref, v_ref, y_ref = refs
    v = v_ref[...]
    vr = jnp.dot(fre_ref[...], v, preferred_element_type=F32)
    vi = jnp.dot(fim_ref[...], v, preferred_element_type=F32)
    if with_filter:
        kr, ki = kr_ref[...], ki_ref[...]
        tm = vr.shape[0]
        first = (lax.broadcasted_iota(jnp.int32, vr.shape, 0) + pl.program_id(1) * tm) == 0
        yr = vr * kr - jnp.where(first, 0.0, vi * ki)
        yi = jnp.where(first, vi * ki, vr * ki + vi * kr)
        y_ref[0] = yr.astype(y_ref.dtype)
        y_ref[1] = yi.astype(y_ref.dtype)
    else:
        y_ref[0] = vr.astype(y_ref.dtype)
        y_ref[1] = vi.astype(y_ref.dtype)


def _fwd_dft(fre, fim, v, kf, tm, out_dtype):
    B, L, D = v.shape
    ftile = pl.BlockSpec((tm, L), lambda b, m: (m, 0))
    in_specs = [ftile, ftile, pl.BlockSpec((None, L, D), lambda b, m: (b, 0, 0))]
    args = [fre, fim, v]
    if kf is not None:
        in_specs += [pl.BlockSpec((None, tm, D), lambda b, m: (0, m, 0)),
                     pl.BlockSpec((None, tm, D), lambda b, m: (1, m, 0))]
        args += [kf, kf]
    return pl.pallas_call(
        functools.partial(_fwd_dft_kernel, with_filter=kf is not None),
        grid=(B, L // tm),
        in_specs=in_specs,
        out_specs=pl.BlockSpec((None, 2, tm, D), lambda b, m: (b, 0, m, 0)),
        out_shape=jax.ShapeDtypeStruct((B, 2, L, D), out_dtype),
        compiler_params=_params("parallel", "parallel"),
        name="hyena_fwd_dft",
    )(*args)


def _filter_spectrum_kernel(hf_ref, hb_ref, k_ref, *, n):
    hfr, hfi = hf_ref[0], hf_ref[1]
    hbr, hbi = hb_ref[0], hb_ref[1]
    first = (lax.broadcasted_iota(jnp.int32, hfr.shape, 0) + pl.program_id(0) * hfr.shape[0]) == 0
    wgt = jnp.where(first, 1.0 / n, 2.0 / n)
    k_ref[0] = (hfr + hbr) * wgt
    k_ref[1] = jnp.where(first, hfi + hbi, hfi - hbi) * wgt


def _filter_spectrum(hspec, tm):
    _, _, L, D = hspec.shape
    return pl.pallas_call(
        functools.partial(_filter_spectrum_kernel, n=2 * L),
        grid=(L // tm,),
        in_specs=[pl.BlockSpec((None, 2, tm, D), lambda m: (0, 0, m, 0)),
                  pl.BlockSpec((None, 2, tm, D), lambda m: (1, 0, m, 0))],
        out_specs=pl.BlockSpec((2, tm, D), lambda m: (0, m, 0)),
        out_shape=jax.ShapeDtypeStruct((2, L, D), F32),
        compiler_params=_params("parallel"),
        name="hyena_filter_spectrum",
    )(hspec, hspec)


def _inv_dft_kernel(gre_ref, gim_ref, yr_ref, yi_ref, vx_ref, x0_ref, skip_ref, z_ref, acc_sc):
    kk = pl.program_id(2)

    @pl.when(kk == 0)
    def _():
        acc_sc[...] = jnp.zeros(acc_sc.shape, F32)

    acc_sc[...] += (jnp.dot(gre_ref[...], yr_ref[...], preferred_element_type=F32)
                    + jnp.dot(gim_ref[...], yi_ref[...], preferred_element_type=F32))

    @pl.when(kk == pl.num_programs(2) - 1)
    def _():
        z = (acc_sc[...] + vx_ref[...].astype(F32) * skip_ref[...]) * x0_ref[...]
        z_ref[...] = z.astype(z_ref.dtype)


def _inv_dft(gre, gim, y, vx, x0, skip, tm, tk):
    B, L, D = vx.shape
    gtile = pl.BlockSpec((tm, tk), lambda b, i, k: (i, k))
    row = pl.BlockSpec((None, tm, D), lambda b, i, k: (b, i, 0))
    return pl.pallas_call(
        _inv_dft_kernel,
        grid=(B, L // tm, L // tk),
        in_specs=[gtile, gtile,
                  pl.BlockSpec((None, None, tk, D), lambda b, i, k: (b, 0, k, 0)),
                  pl.BlockSpec((None, None, tk, D), lambda b, i, k: (b, 1, k, 0)),
                  row, row, pl.BlockSpec((1, D), lambda b, i, k: (0, 0))],
        out_specs=row,
        out_shape=jax.ShapeDtypeStruct((B, L, D), BF16),
        scratch_shapes=[pltpu.VMEM((tm, D), F32)],
        compiler_params=_params("parallel", "parallel", "arbitrary"),
        name="hyena_inv_dft",
    )(gre, gim, y, y, vx, x0, skip)


FFT_N1 = 64
FFT_SLOT_GROUP = 8
FFT_ROWS = 16


def _cos_units(idx, n):
    return jnp.cos((idx & (n - 1)).astype(F32) * (2.0 * math.pi / n))


def _fft_step1_tables_kernel(fb_ref, gb_ref, *, n1, n2):
    n = n1 * n2
    h1, quarter, R = n1 // 2, n // 4, FFT_ROWS
    lr, lh = R.bit_length() - 1, h1.bit_length() - 1
    g = pl.program_id(0)

    def entry(part, slot, t1):
        idx = jnp.where((part == 1) & (slot == 0), (n // 2) * t1, n2 * t1 * slot + quarter * part)
        return _cos_units(idx, n)

    rows, cols = fb_ref.shape
    r = lax.broadcasted_iota(jnp.int32, (rows, cols), 0) + g * rows
    c = lax.broadcasted_iota(jnp.int32, (rows, cols), 1)
    val = entry(r >> (lr + lh), (r >> lr) & (h1 - 1), c >> lr)
    fb_ref[...] = jnp.where((r & (R - 1)) == (c & (R - 1)), val, 0.0).astype(fb_ref.dtype)
    rows, cols = gb_ref.shape
    r = lax.broadcasted_iota(jnp.int32, (rows, cols), 0) + g * rows
    c = lax.broadcasted_iota(jnp.int32, (rows, cols), 1)
    slot = (c >> lr) & (h1 - 1)
    val = jnp.where(slot == 0, 1.0, 2.0) * entry(c >> (lr + lh), slot, r >> lr)
    gb_ref[...] = jnp.where((r & (R - 1)) == (c & (R - 1)), val, 0.0).astype(gb_ref.dtype)


def _fft_step2_tables_kernel(m3_ref, m3i_ref, *, n1, n2):
    n = n1 * n2
    quarter, half = n // 4, n2 // 2
    g = pl.program_id(0)
    r = lax.broadcasted_iota(jnp.int32, (2 * n2, 2 * n2), 0)
    c = lax.broadcasted_iota(jnp.int32, (2 * n2, 2 * n2), 1)
    rj, rim = r & (n2 - 1), (r >= n2).astype(jnp.int32)
    cj, cim = c & (n2 - 1), (c >= n2).astype(jnp.int32)
    for si in range(m3_ref.shape[0]):
        s = g * m3_ref.shape[0] + si
        fwd = _cos_units(cj * (s + n1 * rj) + quarter * (rim - cim), n)
        inv = _cos_units(rj * (s + n1 * cj) + quarter * (cim - rim), n)
        if si == 0:
            def slot0(j, t2, t_im, out_im):
                lo = (j < half) & (t_im == 0)
                hi = (j >= half) & (t_im == 1)
                idx0 = jnp.where((out_im == 1) & (j == 0), (n // 2) * t2, t2 * n1 * j + quarter * out_im)
                idxh = t2 * (n1 // 2 + n1 * (j - half)) + quarter * out_im
                return lo, hi, jnp.where(lo, idx0, idxh)
            lo, hi, idx = slot0(rj, cj, cim, rim)
            fwd0 = jnp.where(lo | hi, _cos_units(idx, n), 0.0)
            lo, hi, idx = slot0(cj, rj, rim, cim)
            amp = jnp.where(lo & (cj == 0), 1.0, 2.0)
            inv0 = jnp.where(lo | hi, amp * _cos_units(idx, n), 0.0)
            fwd = jnp.where(g == 0, fwd0, fwd)
            inv = jnp.where(g == 0, inv0, inv)
        m3_ref[si] = fwd.astype(m3_ref.dtype)
        m3i_ref[si] = inv.astype(m3i_ref.dtype)


def _fft_tables(n1, n2):
    h1, sg, R = n1 // 2, FFT_SLOT_GROUP, FFT_ROWS
    steps = 8
    fb, gb = pl.pallas_call(
        functools.partial(_fft_step1_tables_kernel, n1=n1, n2=n2),
        grid=(steps,),
        out_specs=[pl.BlockSpec((2 * h1 * R // steps, h1 * R), lambda g: (g, 0)),
                   pl.BlockSpec((h1 * R // steps, 2 * h1 * R), lambda g: (g, 0))],
        out_shape=[jax.ShapeDtypeStruct((2 * h1 * R, h1 * R), BF16),
                   jax.ShapeDtypeStruct((h1 * R, 2 * h1 * R), BF16)],
        compiler_params=_params("parallel"),
        name="fft_step1_tables",
    )()
    mspec = pl.BlockSpec((sg, 2 * n2, 2 * n2), lambda g: (g, 0, 0))
    m3, m3i = pl.pallas_call(
        functools.partial(_fft_step2_tables_kernel, n1=n1, n2=n2),
        grid=(h1 // sg,),
        out_specs=[mspec, mspec],
        out_shape=[jax.ShapeDtypeStruct((h1, 2 * n2, 2 * n2), BF16)] * 2,
        compiler_params=_params("parallel"),
        name="fft_step2_tables",
    )()
    return fb, gb, m3, m3i


def _fft_step1_kernel(fb_ref, x_ref, a_ref):
    h1, R, D = x_ref.shape
    a = jnp.dot(fb_ref[...], x_ref[...].reshape(h1 * R, D), preferred_element_type=F32)
    a_ref[...] = a.reshape(2, h1, R, D).astype(a_ref.dtype)


def _fft_step1(fb, x):
    B, L, D = x.shape
    R = FFT_ROWS
    h1 = fb.shape[1] // R
    n2 = L // h1
    return pl.pallas_call(
        _fft_step1_kernel,
        grid=(B, n2 // R),
        in_specs=[pl.BlockSpec(fb.shape, lambda b, c: (0, 0)),
                  pl.BlockSpec((None, h1, R, D), lambda b, c: (b, 0, c, 0))],
        out_specs=pl.BlockSpec((None, 2, h1, R, D), lambda b, c: (b, 0, 0, c, 0)),
        out_shape=jax.ShapeDtypeStruct((B, 2, h1, n2, D), BF16),
        compiler_params=_params("parallel", "parallel"),
        name="fft_step1",
    )(fb, x.reshape(B, h1, n2, D))


def _first_bin(shape, g):
    return (lax.broadcasted_iota(jnp.int32, shape, 0) == 0) & (g == 0)


def _fft_filter_spectrum_kernel(m3_ref, a_ref, k_ref, *, n):
    n2 = a_ref.shape[-2]
    g = pl.program_id(0)
    for si in range(m3_ref.shape[0]):
        hf = jnp.dot(m3_ref[si], jnp.concatenate([a_ref[0, 0, si], a_ref[0, 1, si]], axis=0),
                     preferred_element_type=F32)
        hb = jnp.dot(m3_ref[si], jnp.concatenate([a_ref[1, 0, si], a_ref[1, 1, si]], axis=0),
                     preferred_element_type=F32)
        k_ref[0, si] = (hf[:n2] + hb[:n2]) * (1.0 / n)
        ki = hf[n2:] - hb[n2:]
        if si == 0:
            ki = jnp.where(_first_bin(ki.shape, g), hf[n2:] + hb[n2:], ki)
        k_ref[1, si] = ki * (1.0 / n)


def _fft_filter_spectrum(m3, a):
    h1, n2 = m3.shape[0], m3.shape[1] // 2
    D = a.shape[-1]
    sg = FFT_SLOT_GROUP
    return pl.pallas_call(
        functools.partial(_fft_filter_spectrum_kernel, n=2 * h1 * n2),
        grid=(h1 // sg,),
        in_specs=[pl.BlockSpec((sg, 2 * n2, 2 * n2), lambda g: (g, 0, 0)),
                  pl.BlockSpec((2, 2, sg, n2, D), lambda g: (0, 0, g, 0, 0))],
        out_specs=pl.BlockSpec((2, sg, n2, D), lambda g: (0, g, 0, 0)),
        out_shape=jax.ShapeDtypeStruct((2, h1, n2, D), F32),
        compiler_params=_params("parallel"),
        name="fft_filter_spectrum",
    )(m3, a)


def _fft_step2_kernel(m3_ref, m3i_ref, kr_ref, ki_ref, a_ref, b_ref, y_sc):
    n2 = a_ref.shape[-2]
    g = pl.program_id(0)
    sg = m3_ref.shape[0]
    for si in range(sg):
        xh = jnp.dot(m3_ref[si], a_ref[:, si].reshape(2 * n2, a_ref.shape[-1]), preferred_element_type=F32)
        xr, xi = xh[:n2], xh[n2:]
        kr, ki = kr_ref[si], ki_ref[si]
        yr = xr * kr - xi * ki
        yi = xr * ki + xi * kr
        if si == 0:
            first = _first_bin(yr.shape, g)
            yr = jnp.where(first, xr * kr, yr)
            yi = jnp.where(first, xi * ki, yi)
        y_sc[si, :n2] = yr.astype(BF16)
        y_sc[si, n2:] = yi.astype(BF16)
    for si in range(sg):
        bb = jnp.dot(m3i_ref[si], y_sc[si], preferred_element_type=F32)
        b_ref[0, si] = bb[:n2].astype(b_ref.dtype)
        b_ref[1, si] = bb[n2:].astype(b_ref.dtype)


def _fft_step2(m3, m3i, kf, a):
    B, _, h1, n2, D = a.shape
    sg = FFT_SLOT_GROUP
    mspec = pl.BlockSpec((sg, 2 * n2, 2 * n2), lambda g, b: (g, 0, 0))
    blk = pl.BlockSpec((None, 2, sg, n2, D), lambda g, b: (b, 0, g, 0, 0))
    return pl.pallas_call(
        _fft_step2_kernel,
        grid=(h1 // sg, B),
        in_specs=[mspec, mspec,
                  pl.BlockSpec((None, sg, n2, D), lambda g, b: (0, g, 0, 0)),
                  pl.BlockSpec((None, sg, n2, D), lambda g, b: (1, g, 0, 0)),
                  blk],
        out_specs=blk,
        out_shape=jax.ShapeDtypeStruct((B, 2, h1, n2, D), BF16),
        scratch_shapes=[pltpu.VMEM((sg, 2 * n2, D), BF16)],
        compiler_params=_params("parallel", "parallel"),
        name="fft_step2",
    )(m3, m3i, kf, kf, a)


def _fft_inv_step1_kernel(gb_ref, b_ref, vx_ref, x0_ref, skip_ref, z_ref):
    _, h1, R, D = b_ref.shape
    y = jnp.dot(gb_ref[...], b_ref[...].reshape(2 * h1 * R, D), preferred_element_type=F32)
    z = (y.reshape(h1, R, D) + vx_ref[...].astype(F32) * skip_ref[...]) * x0_ref[...]
    z_ref[...] = z.astype(z_ref.dtype)


def _fft_inv_step1(gb, bm, vx, x0, skip):
    B, L, D = vx.shape
    R = FFT_ROWS
    h1 = gb.shape[0] // R
    n2 = L // h1
    blk = pl.BlockSpec((None, h1, R, D), lambda b, c: (b, 0, c, 0))
    z = pl.pallas_call(
        _fft_inv_step1_kernel,
        grid=(B, n2 // R),
        in_specs=[pl.BlockSpec(gb.shape, lambda b, c: (0, 0)),
                  pl.BlockSpec((None, 2, h1, R, D), lambda b, c: (b, 0, 0, c, 0)),
                  blk, blk, pl.BlockSpec((1, D), lambda b, c: (0, 0))],
        out_specs=blk,
        out_shape=jax.ShapeDtypeStruct((B, h1, n2, D), BF16),
        compiler_params=_params("parallel", "parallel"),
        name="fft_inv_step1",
    )(gb, bm, vx.reshape(B, h1, n2, D), x0.reshape(B, h1, n2, D), skip)
    return z.reshape(B, L, D)


def _rope_tables(L):
    pos = jnp.arange(L, dtype=jnp.int32)
    rows = (pos // GRID_W).astype(F32)
    cols = (pos % GRID_W).astype(F32)
    inv = ROPE_BASE ** (-jnp.arange(ROPE_PAIRS, dtype=F32) / ROPE_PAIRS)
    lane = jnp.arange(LANES)
    within = lane % HEAD_DIM
    axis = within // (2 * ROPE_PAIRS)
    half = (within % (2 * ROPE_PAIRS)) // ROPE_PAIRS
    ang = jnp.where(axis[None, :] == 0, rows[:, None], cols[:, None]) * inv[within % ROPE_PAIRS][None, :]
    cos, sin = jnp.cos(ang), jnp.sin(ang)
    sin_a = jnp.where(half[None, :] == 0, -sin, 0.0)
    sin_b = jnp.where(half[None, :] == 1, sin, 0.0)
    return cos, sin_a, sin_b


def _tile(L, pref):
    return min(L, pref)


def _four_step(L):
    return (2 * L) // FFT_N1 >= 32


def _conv_tables(L):
    return _fft_tables(FFT_N1, 2 * L // FFT_N1) if _four_step(L) else _dft_tables(L, _tile(L, DFT_ROW_TILE))


def _run_group(x, rowmap, shared_mod, mods, p, cache, rope_tabs, dft, kv_dtype):
    B, L, D = x.shape
    tm = _tile(L, ROW_TILE)
    rows = (1, B * L, D) if shared_mod else (B, L, D)
    tmr = _tile(rows[1], ROW_TILE)
    ctx_k, ctx_v = [], []
    for i in range(DEPTH):
        j = i // 2
        sh1, sc1, g1, sh2, sc2, g2 = (mods[i, :, s * D:(s + 1) * D].reshape(MOD_ROWS, 1, D) for s in range(6))
        n1 = p["norm1_g"][i][None]
        if i % 2 == 0:
            lam_init = 0.8 - 0.6 * math.exp(-0.3 * i)
            q, k, v = (a.reshape(B, L, D) for a in
                       _qkv(x.reshape(rows), n1, sh1, sc1, p["attn_w_qkv"][j], rowmap, rope_tabs, kv_dtype, tmr))
            ctx_k.append(k)
            ctx_v.append(v)
            mix = _attention(q, k, v, None if cache is None else cache + (j,),
                             p["attn_lambda"][j].astype(F32), p["attn_subln_g"][j][None],
                             lam_init, _tile(L, ATTN_TILE), _tile(L, ATTN_TILE),
                             N_HEADS if L <= ATTN_TILE else 1)
            w_o, b_o = p["attn_w_o"][j], None
        else:
            vx, x0 = _hyena_in(x, n1, sh1, sc1, p["hy_w_in"][j], p["hy_b_in"][j][None],
                               p["hy_conv_w"][j], p["hy_conv_b"][j][None], rowmap, tm)
            filt = _implicit_filter(L, p["filt_w1"][j], p["filt_b1"][j], p["filt_w2"][j], p["filt_b2"][j],
                                    p["filt_w3"][j], p["filt_b3"][j], p["filt_freq"][j], FILT_COL_TILE)
            skip = p["hy_skip"][j][None]
            if _four_step(L):
                fb, gb, m3, m3i = dft
                kf = _fft_filter_spectrum(m3, _fft_step1(fb, filt))
                bm = _fft_step2(m3, m3i, kf, _fft_step1(fb, vx))
                mix = _fft_inv_step1(gb, bm, vx, x0, skip)
            else:
                fre, fim, gim = dft
                tmf = _tile(L, DFT_ROW_TILE)
                kf = _filter_spectrum(_fwd_dft(fre, fim, filt, None, tmf, F32), tmf)
                y = _fwd_dft(fre, fim, vx, kf, tmf, BF16)
                mix = _inv_dft(fre, gim, y, vx, x0, skip, tm, _tile(L, ROW_TILE))
            w_o, b_o = p["hy_w_out"][j], p["hy_b_out"][j][None]
        final_g = p["final_g"][None] if i == DEPTH - 1 else None
        x = _out_ffn(mix.reshape(rows), w_o, b_o, x.reshape(rows), g1, p["norm2_g"][i][None], sh2, sc2, g2,
                     p["ffn_w_gu"][i], p["ffn_w_down"][i], final_g, rowmap, tmr).reshape(B, L, D)
    return x, ctx_k, ctx_v


def kernel(x_prompt, x_sample, cache_k, cache_v, c, c_ctx, ada_w, ada_b, norm1_g, norm2_g, attn_w_qkv, attn_lambda, attn_subln_g, attn_w_o, hy_w_in, hy_b_in, hy_conv_w, hy_conv_b, filt_w1, filt_b1, filt_w2, filt_b2, filt_w3, filt_b3, filt_freq, hy_skip, hy_w_out, hy_b_out, ffn_w_gu, ffn_w_down, final_g):
    D = D_MODEL
    nb = c.shape[0]
    cmat = jnp.concatenate([c, c_ctx[None], jnp.zeros((MOD_ROWS - nb - 1, D), F32)], axis=0)
    mods = _adaln_all(cmat, ada_w, ada_b)

    p = dict(norm1_g=norm1_g, norm2_g=norm2_g, attn_lambda=attn_lambda, attn_subln_g=attn_subln_g,
             hy_b_in=hy_b_in, hy_conv_w=hy_conv_w, hy_conv_b=hy_conv_b,
             filt_w1=filt_w1, filt_b1=filt_b1, filt_w2=filt_w2, filt_b2=filt_b2, filt_w3=filt_w3,
             filt_b3=filt_b3, filt_freq=filt_freq, hy_skip=hy_skip, hy_b_out=hy_b_out, final_g=final_g,
             attn_w_qkv=attn_w_qkv.astype(BF16), attn_w_o=attn_w_o.astype(BF16),
             hy_w_in=hy_w_in.astype(BF16), hy_w_out=hy_w_out.astype(BF16),
             ffn_w_gu=ffn_w_gu.astype(BF16), ffn_w_down=ffn_w_down.astype(BF16))

    Lp, Ls = x_prompt.shape[1], x_sample.shape[1]
    y_prompt, ctx_k, ctx_v = _run_group(x_prompt, lambda b: CTX_ROW, True, mods, p, None, None,
                                        _conv_tables(Lp), F32)
    cache = tuple(a.reshape(a.shape[0], a.shape[1], a.shape[2], D) for a in (cache_k, cache_v))
    y_sample, _, _ = _run_group(x_sample, lambda b: b, False, mods, p, cache, _rope_tables(Ls),
                                _conv_tables(Ls), BF16)

    Bp = x_prompt.shape[0]
    new_k = jnp.stack(ctx_k, axis=1).reshape(Bp, len(ctx_k), Lp, N_HEADS, 2, HEAD_DIM)
    new_v = jnp.stack(ctx_v, axis=1).reshape(Bp, len(ctx_v), Lp, N_HEADS, V_DIM)
    return y_prompt, y_sample, new_k, new_v
```

```python
import functools
import math

import jax
import jax.numpy as jnp
from jax import lax
from jax.experimental import pallas as pl
from jax.experimental.pallas import tpu as pltpu

D_MODEL = 1024
DEPTH = 4
GRID_W = 64
N_HEADS = 8
HEAD_DIM = 64
V_DIM = 2 * HEAD_DIM
ROPE_PAIRS = HEAD_DIM // 4
ROPE_BASE = 10000.0
FILT_BANDS = 16
FILT_ORDER = 64
FILT_TARGET = 1e-2
FILT_FAST_PCT = 0.3
FILT_SLOW_PCT = 1.5
FILT_EPS = 1e-6
D_FF = 2816
EPS = 1e-6
SUBLN_EPS = 1e-5

LANES = 128
MOD_ROWS = 16
CTX_ROW = 8
VMEM_LIMIT = 56 * 1024 * 1024

ROW_TILE = 512
ATTN_TILE = 512
DFT_ROW_TILE = 256
FILT_COL_TILE = 256
COL_CHUNK = 512
FFN_CHUNK = 256
assert D_FF % FFN_CHUNK == 0 and D_MODEL % COL_CHUNK == 0

F32 = jnp.float32
BF16 = jnp.bfloat16


def _dot_split(a, b):
    a_hi, b_hi = a.astype(BF16), b.astype(BF16)
    a_lo = (a - a_hi.astype(F32)).astype(BF16)
    b_lo = (b - b_hi.astype(F32)).astype(BF16)
    dot = functools.partial(jnp.dot, preferred_element_type=F32)
    return dot(a_hi, b_hi) + (dot(a_hi, b_lo) + dot(a_lo, b_hi))


def _params(*sem):
    return pltpu.CompilerParams(dimension_semantics=sem, vmem_limit_bytes=VMEM_LIMIT)


def _lane_tile(x, reps):
    return x if reps == 1 else jnp.concatenate([x] * reps, axis=1)


def _norm_mod(x, g, shift, scale):
    ms = jnp.mean(x * x, axis=-1, keepdims=True)
    return (x * lax.rsqrt(ms + EPS) * g) * (1.0 + scale) + shift


def _mod_kernel(c_ref, w_ref, b_ref, o_ref):
    c = c_ref[...]
    a = c * jax.nn.sigmoid(c)
    o_ref[...] = _dot_split(a, w_ref[...]) + b_ref[...]


def _adaln_all(cmat, ada_w, ada_b):
    n6 = 6 * D_MODEL
    tn = 1536
    return pl.pallas_call(
        _mod_kernel,
        grid=(DEPTH, n6 // tn),
        in_specs=[
            pl.BlockSpec((MOD_ROWS, D_MODEL), lambda i, n: (0, 0)),
            pl.BlockSpec((None, D_MODEL, tn), lambda i, n: (i, 0, n)),
            pl.BlockSpec((None, 1, tn), lambda i, n: (i, 0, n)),
        ],
        out_specs=pl.BlockSpec((None, MOD_ROWS, tn), lambda i, n: (i, 0, n)),
        out_shape=jax.ShapeDtypeStruct((DEPTH, MOD_ROWS, n6), F32),
        compiler_params=_params("parallel", "parallel"),
        name="adaln",
    )(cmat, ada_w, ada_b.reshape(DEPTH, 1, n6))


def _rope_chunk(x, cos, sin_a, sin_b):
    return x * cos + pltpu.roll(x, LANES - ROPE_PAIRS, 1) * sin_a + pltpu.roll(x, ROPE_PAIRS, 1) * sin_b


def _qkv_kernel(*refs, rope, tn):
    if rope:
        x_ref, g_ref, sh_ref, sc_ref, w_ref, cos_ref, sa_ref, sb_ref, q_ref, k_ref, v_ref = refs
    else:
        x_ref, g_ref, sh_ref, sc_ref, w_ref, q_ref, k_ref, v_ref = refs
    h = _norm_mod(x_ref[...], g_ref[...], sh_ref[...], sc_ref[...]).astype(BF16)
    outs = (q_ref, k_ref, v_ref)
    qk_scale = HEAD_DIM ** -0.5 * math.log2(math.e)
    for part in range(3):
        for n in range(D_MODEL // tn):
            col = part * D_MODEL + n * tn
            y = jnp.dot(h, w_ref[:, col:col + tn], preferred_element_type=F32)
            for j in range(tn // LANES):
                yj = y[:, j * LANES:(j + 1) * LANES]
                if rope and part < 2:
                    yj = _rope_chunk(yj, cos_ref[...], sa_ref[...], sb_ref[...])
                if part == 0:
                    yj = yj * qk_scale
                lo = n * tn + j * LANES
                outs[part][:, lo:lo + LANES] = yj.astype(outs[part].dtype)


def _qkv(x, g, shift, scale, w, rowmap, rope_tabs, kv_dtype, tm):
    B, L, D = x.shape
    rope = rope_tabs is not None
    row = pl.BlockSpec((None, tm, D), lambda b, i: (b, i, 0))
    vec = pl.BlockSpec((None, 1, D), lambda b, i: (rowmap(b), 0, 0))
    in_specs = [row, pl.BlockSpec((1, D), lambda b, i: (0, 0)), vec, vec,
                pl.BlockSpec((D, 3 * D), lambda b, i: (0, 0))]
    args = [x, g, shift, scale, w]
    if rope:
        tab = pl.BlockSpec((tm, LANES), lambda b, i: (i, 0))
        in_specs += [tab, tab, tab]
        args += list(rope_tabs)
    return pl.pallas_call(
        functools.partial(_qkv_kernel, rope=rope, tn=COL_CHUNK),
        grid=(B, L // tm),
        in_specs=in_specs,
        out_specs=[row, row, row],
        out_shape=[jax.ShapeDtypeStruct((B, L, D), BF16),
                   jax.ShapeDtypeStruct((B, L, D), kv_dtype),
                   jax.ShapeDtypeStruct((B, L, D), kv_dtype)],
        compiler_params=_params("parallel", "parallel"),
        name="qkv",
    )(*args)


def _hyena_in_kernel(xp_ref, x_ref, xn_ref, g_ref, sh_ref, sc_ref, w_ref, b_ref, cw_ref, cb_ref,
                     vx_ref, x0_ref, u_sc, *, tn):
    tm, D = x_ref.shape
    i = pl.program_id(1)
    xa = jnp.concatenate([xp_ref[...], x_ref[...], xn_ref[...]], axis=0)
    h = _norm_mod(xa, g_ref[...], sh_ref[...], sc_ref[...]).astype(BF16)
    keep_top = (i > 0).astype(F32)
    keep_bot = (i < pl.num_programs(1) - 1).astype(F32)

    def part(p, n, slot):
        sl = slice(p * D + n * tn, p * D + (n + 1) * tn)
        u = jnp.dot(h, w_ref[:, sl], preferred_element_type=F32) + b_ref[:, sl]
        u_sc[slot, 0:8] = u[:8] * keep_top
        u_sc[slot, 8:8 + tm] = u[8:8 + tm]
        u_sc[slot, 8 + tm:] = u[8 + tm:] * keep_bot
        w = cw_ref[:, sl]
        return (u_sc[slot, 7:7 + tm] * w[0:1] + u_sc[slot, 8:8 + tm] * w[1:2] + u_sc[slot, 9:9 + tm] * w[2:3]
                + cb_ref[:, sl])

    for n in range(D // tn):
        out = slice(n * tn, (n + 1) * tn)
        x0_ref[:, out] = part(0, n, 0)
        vx_ref[:, out] = (part(2, n, 1) * part(1, n, 2)).astype(vx_ref.dtype)


def _hyena_in(x, g, shift, scale, w, b, conv_w, conv_b, rowmap, tm):
    B, L, D = x.shape
    N = w.shape[1]
    hb = tm // 8
    tn = COL_CHUNK
    vec = pl.BlockSpec((None, 1, D), lambda b_, i: (rowmap(b_), 0, 0))
    row = pl.BlockSpec((None, tm, D), lambda b_, i: (b_, i, 0))
    const = lambda r: pl.BlockSpec((r, N), lambda b_, i: (0, 0))
    return pl.pallas_call(
        functools.partial(_hyena_in_kernel, tn=tn),
        grid=(B, L // tm),
        in_specs=[pl.BlockSpec((None, 8, D), lambda b_, i: (b_, jnp.maximum(i * hb - 1, 0), 0)),
                  row,
                  pl.BlockSpec((None, 8, D), lambda b_, i: (b_, jnp.minimum((i + 1) * hb, L // 8 - 1), 0)),
                  pl.BlockSpec((1, D), lambda b_, i: (0, 0)), vec, vec,
                  pl.BlockSpec((D, N), lambda b_, i: (0, 0)), const(1), const(3), const(1)],
        out_specs=[row, row],
        out_shape=[jax.ShapeDtypeStruct((B, L, D), BF16), jax.ShapeDtypeStruct((B, L, D), F32)],
        scratch_shapes=[pltpu.VMEM((3, tm + 16, tn), F32)],
        compiler_params=_params("parallel", "parallel"),
        name="hyena_in",
    )(x, x, x, g, shift, scale, w, b, conv_w, conv_b)


def _attn_kernel(*refs, heads, **static):
    n_scratch = 7
    lam_ref, g_ref = refs[:2]
    for h in range(heads):
        cols = slice(h * LANES, (h + 1) * LANES)
        per_head = [r.at[:, cols] for r in refs[2:len(refs) - n_scratch]]
        _attn_head(lam_ref, g_ref, *per_head, *refs[len(refs) - n_scratch:], **static)


def _attn_head(*refs, tq, tk, lam_init, cached):
    if cached:
        lam_ref, g_ref, q_ref, ck_ref, cv_ref, k_ref, v_ref, o_ref = refs[:8]
    else:
        lam_ref, g_ref, q_ref, k_ref, v_ref, o_ref = refs[:6]
    m_sc, acc_sc, vx_sc, q2_sc, s0_sc, s1_sc, s2_sc = refs[-7:]
    nc = ck_ref.shape[0] // tk if cached else 0
    nk = nc + k_ref.shape[0] // tk
    nq = q_ref.shape[0] // tq
    if cached:
        vx_sc[:nc * tk, :V_DIM] = cv_ref[...].astype(BF16)
    vx_sc[nc * tk:, :V_DIM] = v_ref[...].astype(BF16)
    vx_sc[:, V_DIM:] = jnp.ones((vx_sc.shape[0], V_DIM), BF16)
    lp = lam_ref[...]
    lam = (jnp.exp(jnp.sum(lp[0:1] * lp[1:2], axis=1, keepdims=True))
           - jnp.exp(jnp.sum(lp[2:3] * lp[3:4], axis=1, keepdims=True)) + lam_init)
    lane = lax.broadcasted_iota(jnp.int32, (tq, LANES), 1)

    def slot(j):
        if j == 0:
            return s0_sc
        return s1_sc if j % 2 == 1 else s2_sc

    def key_tile(j):
        if j < nc:
            return ck_ref[j * tk:(j + 1) * tk, :].astype(BF16)
        return k_ref[(j - nc) * tk:(j - nc + 1) * tk, :].astype(BF16)

    def load_q(i):
        q = q_ref[pl.ds(pl.multiple_of(i * tq, tq), tq), :]
        zero = jnp.zeros_like(q)
        q2_sc[:tq] = jnp.where(lane < HEAD_DIM, q, zero)
        q2_sc[tq:] = jnp.where(lane >= HEAD_DIM, q, zero)

    def scores(j):
        s_sc = slot(j)
        s_sc[...] = lax.dot_general(q2_sc[...], key_tile(j), (((1,), (1,)), ((), ())),
                                    preferred_element_type=F32)

    def accumulate(j):
        s_sc = slot(j)
        t = jnp.broadcast_to(jnp.max(s_sc[...], axis=1, keepdims=True), m_sc.shape)
        if j == 0:
            m_new = t
        else:
            m_prev = m_sc[...]
            m_new = jnp.maximum(m_prev, t)
            alpha = jnp.exp2(m_prev - m_new)
        e = jnp.exp2(s_sc[...] - _lane_tile(m_new, tk // LANES))
        pv = jnp.dot(e.astype(BF16), vx_sc[j * tk:(j + 1) * tk, :], preferred_element_type=F32)
        acc_sc[...] = pv if j == 0 else _lane_tile(alpha, 2) * acc_sc[...] + pv
        m_sc[...] = m_new

    def finalize(i):
        o = (acc_sc[:tq, :V_DIM] / acc_sc[:tq, V_DIM:] - lam * (acc_sc[tq:, :V_DIM] / acc_sc[tq:, V_DIM:]))
        ms = jnp.mean(o * o, axis=-1, keepdims=True)
        y = (o * lax.rsqrt(ms + SUBLN_EPS) * g_ref[...]) * (1.0 - lam_init)
        o_ref[pl.ds(pl.multiple_of(i * tq, tq), tq), :] = y.astype(o_ref.dtype)

    def query_tile(i, prefetch):
        for j in range(1, nk):
            accumulate(j - 1)
            scores(j)
        if prefetch:
            load_q(i + 1)
        accumulate(nk - 1)
        if prefetch:
            scores(0)
        finalize(i)

    load_q(0)
    scores(0)

    def body(i, carry):
        query_tile(i, True)
        return carry
    lax.fori_loop(0, nq - 1, body, 0)
    query_tile(nq - 1, False)


def _attention(q, k, v, cache, lam_params, subln_g, lam_init, tq, tk, heads):
    B, Lq, D = q.shape
    Lk = k.shape[1]
    width = heads * LANES
    kv = pl.BlockSpec((None, Lk, width), lambda b, h: (b, 0, h))
    qo = pl.BlockSpec((None, Lq, width), lambda b, h: (b, 0, h))
    in_specs = [pl.BlockSpec((4, HEAD_DIM), lambda b, h: (0, 0)),
                pl.BlockSpec((1, V_DIM), lambda b, h: (0, 0)), qo]
    args = [lam_params, subln_g, q]
    past = 0
    if cache is not None:
        ck, cv, layer = cache
        past = ck.shape[2]
        cspec = pl.BlockSpec((None, None, past, width), lambda b, h: (b, layer, 0, h))
        in_specs += [cspec, cspec]
        args += [ck, cv]
    in_specs += [kv, kv]
    args += [k, v]
    return pl.pallas_call(
        functools.partial(_attn_kernel, heads=heads, tq=tq, tk=tk, lam_init=lam_init,
                          cached=cache is not None),
        grid=(B, N_HEADS // heads),
        in_specs=in_specs,
        out_specs=qo,
        out_shape=jax.ShapeDtypeStruct((B, Lq, D), BF16),
        scratch_shapes=[pltpu.VMEM((2 * tq, LANES), F32),
                        pltpu.VMEM((2 * tq, 2 * V_DIM), F32),
                        pltpu.VMEM((past + Lk, 2 * V_DIM), BF16),
                        pltpu.VMEM((2 * tq, LANES), BF16)]
                       + [pltpu.VMEM((2 * tq, tk), F32)] * 3,
        compiler_params=_params("parallel", "parallel"),
        name="diff_attention",
    )(*args)


def _resident(shape):
    return pl.BlockSpec(shape, lambda b, i: (0,) * len(shape), pipeline_mode=pl.Buffered(1))


def _out_ffn_kernel(*refs, fc, final, bias):
    refs = list(refs)
    o_ref = refs.pop()
    fg_ref = refs.pop() if final else None
    a_ref, wo_ref = refs[:2]
    bo_ref = refs[2] if bias else None
    x_ref, gate1_ref, g_ref, sh_ref, sc_ref, gate2_ref, wgu_ref, wd_ref = refs[2 + bias:]
    y = jnp.dot(a_ref[...], wo_ref[...], preferred_element_type=F32)
    if bias:
        y = y + bo_ref[...]
    x = x_ref[...] + gate1_ref[...] * y
    h = _norm_mod(x, g_ref[...], sh_ref[...], sc_ref[...]).astype(BF16)
    acc = jnp.zeros(x.shape, F32)
    for f in range(D_FF // fc):
        gg = jnp.dot(h, wgu_ref[:, f * fc:(f + 1) * fc], preferred_element_type=F32)
        uu = jnp.dot(h, wgu_ref[:, D_FF + f * fc:D_FF + (f + 1) * fc], preferred_element_type=F32)
        act = (gg * jax.nn.sigmoid(gg)) * uu
        acc = acc + jnp.dot(act.astype(BF16), wd_ref[f * fc:(f + 1) * fc, :], preferred_element_type=F32)
    out = x + gate2_ref[...] * acc
    if final:
        ms = jnp.mean(out * out, axis=-1, keepdims=True)
        out = out * lax.rsqrt(ms + EPS) * fg_ref[...]
    o_ref[...] = out


def _out_ffn(a, w_o, b_o, x, gate1, g, shift, scale, gate2, w_gu, w_down, final_g, rowmap, tm):
    B, L, D = x.shape
    final, bias = final_g is not None, b_o is not None
    row = pl.BlockSpec((None, tm, D), lambda b, i: (b, i, 0))
    vec = pl.BlockSpec((None, 1, D), lambda b, i: (rowmap(b), 0, 0))
    const = pl.BlockSpec((1, D), lambda b, i: (0, 0))
    in_specs = [row, _resident((D, D))] + ([const] if bias else [])
    args = [a, w_o] + ([b_o] if bias else [])
    in_specs += [row, vec, const, vec, vec, vec, _resident((D, 2 * D_FF)), _resident((D_FF, D))]
    args += [x, gate1, g, shift, scale, gate2, w_gu, w_down]
    if final:
        in_specs.append(const)
        args.append(final_g)
    return pl.pallas_call(
        functools.partial(_out_ffn_kernel, fc=FFN_CHUNK, final=final, bias=bias),
        grid=(B, L // tm),
        in_specs=in_specs,
        out_specs=row,
        out_shape=jax.ShapeDtypeStruct((B, L, D), F32),
        compiler_params=_params("parallel", "parallel"),
        name="out_ffn",
    )(*args)


def _filter_kernel(bands_ref, w1t_ref, w1c_ref, w1s_ref, b1_ref, w2_ref, b2_ref, fr_ref,
                   w3f_ref, w3b_ref, b3f_ref, b3b_ref, delta_ref, h_ref, hid_sc):
    L = hid_sc.shape[0]
    pos = lax.broadcasted_iota(jnp.int32, (L, LANES), 0).astype(F32)
    t = pos / max(L - 1, 1)

    @pl.when(pl.program_id(0) == 0)
    def _():
        ang = (2.0 * math.pi * pos / L) * bands_ref[...]
        fr = fr_ref[...]
        pre = (t * w1t_ref[...]
               + _dot_split(jnp.cos(ang), w1c_ref[...])
               - _dot_split(jnp.sin(ang), w1s_ref[...])
               + b1_ref[...])
        hid = jnp.sin(fr * pre)
        hid_sc[...] = jnp.sin(fr * (_dot_split(hid, w2_ref[...])
                                    + b2_ref[...]))

    hid = hid_sc[...]
    window = jnp.exp(-t[:, 0:1] * delta_ref[...])
    hf = (_dot_split(hid, w3f_ref[...]) + b3f_ref[...]) * window
    hb = (_dot_split(hid, w3b_ref[...]) + b3b_ref[...]) * window
    norm = (jnp.sum(jnp.abs(hf), axis=0, keepdims=True) + jnp.sum(jnp.abs(hb), axis=0, keepdims=True)
            + FILT_EPS)
    row = lax.broadcasted_iota(jnp.int32, hb.shape, 0)
    h_ref[0] = (hf / norm).astype(h_ref.dtype)
    h_ref[1] = jnp.where(row == 0, 0.0, hb / norm).astype(h_ref.dtype)


def _pad2(a, rows, cols):
    return jnp.pad(a, ((0, rows - a.shape[0]), (0, cols - a.shape[1])))


def _implicit_filter(L, w1, b1, w2, b2, w3, b3, freq, tc):
    D = D_MODEL
    bands = _pad2(jnp.linspace(1e-4, FILT_BANDS - 1, FILT_BANDS, dtype=F32)[None], 1, LANES)
    deltas = jnp.abs(jnp.linspace(math.log(FILT_TARGET) / FILT_SLOW_PCT,
                                  math.log(FILT_TARGET) / FILT_FAST_PCT, D, dtype=F32))[None]
    w1 = w1.astype(F32)
    args = [bands,
            _pad2(w1[0:1], 1, LANES),
            _pad2(w1[1:1 + FILT_BANDS], LANES, LANES),
            _pad2(w1[1 + FILT_BANDS:], LANES, LANES),
            _pad2(b1[None], 1, LANES), _pad2(w2, LANES, LANES), _pad2(b2[None], 1, LANES),
            _pad2(freq[None], 1, LANES),
            _pad2(w3, LANES, 2 * D), _pad2(w3, LANES, 2 * D), b3[None], b3[None], deltas]
    small = lambda shape: pl.BlockSpec(shape, lambda c: (0, 0))
    nb = D // tc
    in_specs = [small((1, LANES)), small((1, LANES)), small((LANES, LANES)), small((LANES, LANES)),
                small((1, LANES)), small((LANES, LANES)), small((1, LANES)), small((1, LANES)),
                pl.BlockSpec((LANES, tc), lambda c: (0, c)),
                pl.BlockSpec((LANES, tc), lambda c: (0, nb + c)),
                pl.BlockSpec((1, tc), lambda c: (0, c)),
                pl.BlockSpec((1, tc), lambda c: (0, nb + c)),
                pl.BlockSpec((1, tc), lambda c: (0, c))]
    return pl.pallas_call(
        _filter_kernel,
        grid=(nb,),
        in_specs=in_specs,
        out_specs=pl.BlockSpec((2, L, tc), lambda c: (0, 0, c)),
        out_shape=jax.ShapeDtypeStruct((2, L, D), BF16),
        scratch_shapes=[pltpu.VMEM((L, LANES), F32)],
        compiler_params=_params("arbitrary"),
        name="hyena_filter",
    )(*args)


def _dft_gen_kernel(fre_ref, fim_ref, gim_ref, cb_sc, sb_sc, *, L):
    n = 2 * L
    tr = fre_ref.shape[0]
    step = 2.0 * math.pi / n
    i = pl.program_id(0)
    row = lax.broadcasted_iota(jnp.int32, (tr, L), 0)
    col = lax.broadcasted_iota(jnp.int32, (tr, L), 1)

    @pl.when(i == 0)
    def _():
        ang = ((row * col) & (n - 1)).astype(F32) * step
        cb_sc[...] = jnp.cos(ang)
        sb_sc[...] = jnp.sin(ang)

    col1 = lax.broadcasted_iota(jnp.int32, (1, L), 1)
    a = (((i * tr) * col1) & (n - 1)).astype(F32) * step
    ca, sa = jnp.cos(a), jnp.sin(a)
    cb, sb = cb_sc[...], sb_sc[...]
    c = cb * ca - sb * sa
    s = -(sb * ca + cb * sa)
    grow = row + i * tr
    fre_ref[...] = c.astype(fre_ref.dtype)
    fim_ref[...] = jnp.where(grow == 0, jnp.where((col & 1) == 0, 1.0, -1.0), s).astype(fim_ref.dtype)
    gim_ref[...] = jnp.where(col == 0, jnp.where((grow & 1) == 0, 1.0, -1.0), s).astype(gim_ref.dtype)


def _dft_tables(L, tr):
    spec = pl.BlockSpec((tr, L), lambda i: (i, 0))
    shape = jax.ShapeDtypeStruct((L, L), BF16)
    return pl.pallas_call(
        functools.partial(_dft_gen_kernel, L=L),
        grid=(L // tr,),
        out_specs=[spec, spec, spec],
        out_shape=[shape, shape, shape],
        scratch_shapes=[pltpu.VMEM((tr, L), F32), pltpu.VMEM((tr, L), F32)],
        compiler_params=_params("arbitrary"),
        name="dft_tables",
    )()


def _fwd_dft_kernel(*refs, with_filter):
    if with_filter:
        fre_ref, fim_ref, v_ref, kr_ref, ki_ref, y_ref = refs
    else:
        fre_ref, fim_ref, v_ref, y_ref = refs
    v = v_ref[...]
    vr = jnp.dot(fre_ref[...], v, preferred_element_type=F32)
    vi = jnp.dot(fim_ref[...], v, preferred_element_type=F32)
    if with_filter:
        kr, ki = kr_ref[...], ki_ref[...]
        tm = vr.shape[0]
        first = (lax.broadcasted_iota(jnp.int32, vr.shape, 0) + pl.program_id(1) * tm) == 0
        yr = vr * kr - jnp.where(first, 0.0, vi * ki)
        yi = jnp.where(first, vi * ki, vr * ki + vi * kr)
        y_ref[0] = yr.astype(y_ref.dtype)
        y_ref[1] = yi.astype(y_ref.dtype)
    else:
        y_ref[0] = vr.astype(y_ref.dtype)
        y_ref[1] = vi.astype(y_ref.dtype)


def _fwd_dft(fre, fim, v, kf, tm, out_dtype):
    B, L, D = v.shape
    ftile = pl.BlockSpec((tm, L), lambda b, m: (m, 0))
    in_specs = [ftile, ftile, pl.BlockSpec((None, L, D), lambda b, m: (b, 0, 0))]
    args = [fre, fim, v]
    if kf is not None:
        in_specs += [pl.BlockSpec((None, tm, D), lambda b, m: (0, m, 0)),
                     pl.BlockSpec((None, tm, D), lambda b, m: (1, m, 0))]
        args += [kf, kf]
    return pl.pallas_call(
        functools.partial(_fwd_dft_kernel, with_filter=kf is not None),
        grid=(B, L // tm),
        in_specs=in_specs,
        out_specs=pl.BlockSpec((None, 2, tm, D), lambda b, m: (b, 0, m, 0)),
        out_shape=jax.ShapeDtypeStruct((B, 2, L, D), out_dtype),
        compiler_params=_params("parallel", "parallel"),
        name="hyena_fwd_dft",
    )(*args)


def _filter_spectrum_kernel(hf_ref, hb_ref, k_ref, *, n):
    hfr, hfi = hf_ref[0], hf_ref[1]
    hbr, hbi = hb_ref[0], hb_ref[1]
    first = (lax.broadcasted_iota(jnp.int32, hfr.shape, 0) + pl.program_id(0) * hfr.shape[0]) == 0
    wgt = jnp.where(first, 1.0 / n, 2.0 / n)
    k_ref[0] = (hfr + hbr) * wgt
    k_ref[1] = jnp.where(first, hfi + hbi, hfi - hbi) * wgt


def _filter_spectrum(hspec, tm):
    _, _, L, D = hspec.shape
    return pl.pallas_call(
        functools.partial(_filter_spectrum_kernel, n=2 * L),
        grid=(L // tm,),
        in_specs=[pl.BlockSpec((None, 2, tm, D), lambda m: (0, 0, m, 0)),
                  pl.BlockSpec((None, 2, tm, D), lambda m: (1, 0, m, 0))],
        out_specs=pl.BlockSpec((2, tm, D), lambda m: (0, m, 0)),
        out_shape=jax.ShapeDtypeStruct((2, L, D), F32),
        compiler_params=_params("parallel"),
        name="hyena_filter_spectrum",
    )(hspec, hspec)


def _inv_dft_kernel(gre_ref, gim_ref, yr_ref, yi_ref, vx_ref, x0_ref, skip_ref, z_ref, acc_sc):
    kk = pl.program_id(2)

    @pl.when(kk == 0)
    def _():
        acc_sc[...] = jnp.zeros(acc_sc.shape, F32)

    acc_sc[...] += (jnp.dot(gre_ref[...], yr_ref[...], preferred_element_type=F32)
                    + jnp.dot(gim_ref[...], yi_ref[...], preferred_element_type=F32))

    @pl.when(kk == pl.num_programs(2) - 1)
    def _():
        z = (acc_sc[...] + vx_ref[...].astype(F32) * skip_ref[...]) * x0_ref[...]
        z_ref[...] = z.astype(z_ref.dtype)


def _inv_dft(gre, gim, y, vx, x0, skip, tm, tk):
    B, L, D = vx.shape
    gtile = pl.BlockSpec((tm, tk), lambda b, i, k: (i, k))
    row = pl.BlockSpec((None, tm, D), lambda b, i, k: (b, i, 0))
    return pl.pallas_call(
        _inv_dft_kernel,
        grid=(B, L // tm, L // tk),
        in_specs=[gtile, gtile,
                  pl.BlockSpec((None, None, tk, D), lambda b, i, k: (b, 0, k, 0)),
                  pl.BlockSpec((None, None, tk, D), lambda b, i, k: (b, 1, k, 0)),
                  row, row, pl.BlockSpec((1, D), lambda b, i, k: (0, 0))],
        out_specs=row,
        out_shape=jax.ShapeDtypeStruct((B, L, D), BF16),
        scratch_shapes=[pltpu.VMEM((tm, D), F32)],
        compiler_params=_params("parallel", "parallel", "arbitrary"),
        name="hyena_inv_dft",
    )(gre, gim, y, y, vx, x0, skip)


FFT_N1 = 64
FFT_SLOT_GROUP = 8
FFT_ROWS = 16
FFT_BATCH = 2


def _cos_units(idx, n):
    return jnp.cos((idx & (n - 1)).astype(F32) * (2.0 * math.pi / n))


def _fft_step1_tables_kernel(fb_ref, gb_ref, *, n1, n2):
    n = n1 * n2
    h1, quarter, R = n1 // 2, n // 4, FFT_ROWS
    lr, lh = R.bit_length() - 1, h1.bit_length() - 1
    g = pl.program_id(0)

    def entry(part, slot, t1):
        idx = jnp.where((part == 1) & (slot == 0), (n // 2) * t1, n2 * t1 * slot + quarter * part)
        return _cos_units(idx, n)

    rows, cols = fb_ref.shape
    r = lax.broadcasted_iota(jnp.int32, (rows, cols), 0) + g * rows
    c = lax.broadcasted_iota(jnp.int32, (rows, cols), 1)
    val = entry(r >> (lr + lh), (r >> lr) & (h1 - 1), c >> lr)
    fb_ref[...] = jnp.where((r & (R - 1)) == (c & (R - 1)), val, 0.0).astype(fb_ref.dtype)
    rows, cols = gb_ref.shape
    r = lax.broadcasted_iota(jnp.int32, (rows, cols), 0) + g * rows
    c = lax.broadcasted_iota(jnp.int32, (rows, cols), 1)
    slot = (c >> lr) & (h1 - 1)
    val = jnp.where(slot == 0, 1.0, 2.0) * entry(c >> (lr + lh), slot, r >> lr)
    gb_ref[...] = jnp.where((r & (R - 1)) == (c & (R - 1)), val, 0.0).astype(gb_ref.dtype)


def _fft_step2_tables_kernel(m3_ref, m3i_ref, *, n1, n2):
    n = n1 * n2
    quarter, half = n // 4, n2 // 2
    g = pl.program_id(0)
    r = lax.broadcasted_iota(jnp.int32, (2 * n2, 2 * n2), 0)
    c = lax.broadcasted_iota(jnp.int32, (2 * n2, 2 * n2), 1)
    rj, rim = r & (n2 - 1), (r >= n2).astype(jnp.int32)
    cj, cim = c & (n2 - 1), (c >= n2).astype(jnp.int32)
    for si in range(m3_ref.shape[0]):
        s = g * m3_ref.shape[0] + si
        fwd = _cos_units(cj * (s + n1 * rj) + quarter * (rim - cim), n)
        inv = _cos_units(rj * (s + n1 * cj) + quarter * (cim - rim), n)
        if si == 0:
            def slot0(j, t2, t_im, out_im):
                lo = (j < half) & (t_im == 0)
                hi = (j >= half) & (t_im == 1)
                idx0 = jnp.where((out_im == 1) & (j == 0), (n // 2) * t2, t2 * n1 * j + quarter * out_im)
                idxh = t2 * (n1 // 2 + n1 * (j - half)) + quarter * out_im
                return lo, hi, jnp.where(lo, idx0, idxh)
            lo, hi, idx = slot0(rj, cj, cim, rim)
            fwd0 = jnp.where(lo | hi, _cos_units(idx, n), 0.0)
            lo, hi, idx = slot0(cj, rj, rim, cim)
            amp = jnp.where(lo & (cj == 0), 1.0, 2.0)
            inv0 = jnp.where(lo | hi, amp * _cos_units(idx, n), 0.0)
            fwd = jnp.where(g == 0, fwd0, fwd)
            inv = jnp.where(g == 0, inv0, inv)
        m3_ref[si] = fwd.astype(m3_ref.dtype)
        m3i_ref[si] = inv.astype(m3i_ref.dtype)


def _fft_tables(n1, n2):
    h1, sg, R = n1 // 2, FFT_SLOT_GROUP, FFT_ROWS
    steps = 8
    fb, gb = pl.pallas_call(
        functools.partial(_fft_step1_tables_kernel, n1=n1, n2=n2),
        grid=(steps,),
        out_specs=[pl.BlockSpec((2 * h1 * R // steps, h1 * R), lambda g: (g, 0)),
                   pl.BlockSpec((h1 * R // steps, 2 * h1 * R), lambda g: (g, 0))],
        out_shape=[jax.ShapeDtypeStruct((2 * h1 * R, h1 * R), BF16),
                   jax.ShapeDtypeStruct((h1 * R, 2 * h1 * R), BF16)],
        compiler_params=_params("parallel"),
        name="fft_step1_tables",
    )()
    mspec = pl.BlockSpec((sg, 2 * n2, 2 * n2), lambda g: (g, 0, 0))
    m3, m3i = pl.pallas_call(
        functools.partial(_fft_step2_tables_kernel, n1=n1, n2=n2),
        grid=(h1 // sg,),
        out_specs=[mspec, mspec],
        out_shape=[jax.ShapeDtypeStruct((h1, 2 * n2, 2 * n2), BF16)] * 2,
        compiler_params=_params("parallel"),
        name="fft_step2_tables",
    )()
    return fb, gb, m3, m3i


def _fft_step1_kernel(fb_ref, x_ref, a_ref):
    nb, h1, R, D = x_ref.shape
    for i in range(nb):
        a = jnp.dot(fb_ref[...], x_ref[i].reshape(h1 * R, D), preferred_element_type=F32)
        a_ref[i] = a.reshape(2, h1, R, D).astype(a_ref.dtype)


def _fft_step1(fb, x):
    B, L, D = x.shape
    R = FFT_ROWS
    h1 = fb.shape[1] // R
    n2 = L // h1
    return pl.pallas_call(
        _fft_step1_kernel,
        grid=(B // FFT_BATCH, n2 // R),
        in_specs=[pl.BlockSpec(fb.shape, lambda b, c: (0, 0)),
                  pl.BlockSpec((FFT_BATCH, h1, R, D), lambda b, c: (b, 0, c, 0))],
        out_specs=pl.BlockSpec((FFT_BATCH, 2, h1, R, D), lambda b, c: (b, 0, 0, c, 0)),
        out_shape=jax.ShapeDtypeStruct((B, 2, h1, n2, D), BF16),
        compiler_params=_params("parallel", "parallel"),
        name="fft_step1",
    )(fb, x.reshape(B, h1, n2, D))


def _first_bin(shape, g):
    return (lax.broadcasted_iota(jnp.int32, shape, 0) == 0) & (g == 0)


def _fft_filter_spectrum_kernel(m3_ref, a_ref, k_ref, *, n):
    n2 = a_ref.shape[-2]
    g = pl.program_id(0)
    for si in range(m3_ref.shape[0]):
        hf = jnp.dot(m3_ref[si], jnp.concatenate([a_ref[0, 0, si], a_ref[0, 1, si]], axis=0),
                     preferred_element_type=F32)
        hb = jnp.dot(m3_ref[si], jnp.concatenate([a_ref[1, 0, si], a_ref[1, 1, si]], axis=0),
                     preferred_element_type=F32)
        k_ref[0, si] = (hf[:n2] + hb[:n2]) * (1.0 / n)
        ki = hf[n2:] - hb[n2:]
        if si == 0:
            ki = jnp.where(_first_bin(ki.shape, g), hf[n2:] + hb[n2:], ki)
        k_ref[1, si] = ki * (1.0 / n)


def _fft_filter_spectrum(m3, a):
    h1, n2 = m3.shape[0], m3.shape[1] // 2
    D = a.shape[-1]
    sg = FFT_SLOT_GROUP
    return pl.pallas_call(
        functools.partial(_fft_filter_spectrum_kernel, n=2 * h1 * n2),
        grid=(h1 // sg,),
        in_specs=[pl.BlockSpec((sg, 2 * n2, 2 * n2), lambda g: (g, 0, 0)),
                  pl.BlockSpec((2, 2, sg, n2, D), lambda g: (0, 0, g, 0, 0))],
        out_specs=pl.BlockSpec((2, sg, n2, D), lambda g: (0, g, 0, 0)),
        out_shape=jax.ShapeDtypeStruct((2, h1, n2, D), F32),
        compiler_params=_params("parallel"),
        name="fft_filter_spectrum",
    )(m3, a)


def _fft_step2_kernel(m3_ref, m3i_ref, kr_ref, ki_ref, a_ref, b_ref, y_sc):
    n2 = a_ref.shape[-2]
    g = pl.program_id(0)
    sg = m3_ref.shape[0]
    for si in range(sg):
        xh = jnp.dot(m3_ref[si], a_ref[:, si].reshape(2 * n2, a_ref.shape[-1]), preferred_element_type=F32)
        xr, xi = xh[:n2], xh[n2:]
        kr, ki = kr_ref[si], ki_ref[si]
        yr = xr * kr - xi * ki
        yi = xr * ki + xi * kr
        if si == 0:
            first = _first_bin(yr.shape, g)
            yr = jnp.where(first, xr * kr, yr)
            yi = jnp.where(first, xi * ki, yi)
        y_sc[si, :n2] = yr.astype(BF16)
        y_sc[si, n2:] = yi.astype(BF16)
    for si in range(sg):
        bb = jnp.dot(m3i_ref[si], y_sc[si], preferred_element_type=F32)
        b_ref[0, si] = bb[:n2].astype(b_ref.dtype)
        b_ref[1, si] = bb[n2:].astype(b_ref.dtype)


def _fft_step2(m3, m3i, kf, a):
    B, _, h1, n2, D = a.shape
    sg = FFT_SLOT_GROUP
    mspec = pl.BlockSpec((sg, 2 * n2, 2 * n2), lambda g, b: (g, 0, 0))
    blk = pl.BlockSpec((None, 2, sg, n2, D), lambda g, b: (b, 0, g, 0, 0))
    return pl.pallas_call(
        _fft_step2_kernel,
        grid=(h1 // sg, B),
        in_specs=[mspec, mspec,
                  pl.BlockSpec((None, sg, n2, D), lambda g, b: (0, g, 0, 0)),
                  pl.BlockSpec((None, sg, n2, D), lambda g, b: (1, g, 0, 0)),
                  blk],
        out_specs=blk,
        out_shape=jax.ShapeDtypeStruct((B, 2, h1, n2, D), BF16),
        scratch_shapes=[pltpu.VMEM((sg, 2 * n2, D), BF16)],
        compiler_params=_params("parallel", "parallel"),
        name="fft_step2",
    )(m3, m3i, kf, kf, a)


def _fft_inv_step1_kernel(gb_ref, b_ref, vx_ref, x0_ref, skip_ref, z_ref):
    nb, _, h1, R, D = b_ref.shape
    for i in range(nb):
        y = jnp.dot(gb_ref[...], b_ref[i].reshape(2 * h1 * R, D), preferred_element_type=F32)
        z = (y.reshape(h1, R, D) + vx_ref[i].astype(F32) * skip_ref[...]) * x0_ref[i]
        z_ref[i] = z.astype(z_ref.dtype)


def _fft_inv_step1(gb, bm, vx, x0, skip):
    B, L, D = vx.shape
    R = FFT_ROWS
    h1 = gb.shape[0] // R
    n2 = L // h1
    blk = pl.BlockSpec((FFT_BATCH, h1, R, D), lambda b, c: (b, 0, c, 0))
    z = pl.pallas_call(
        _fft_inv_step1_kernel,
        grid=(B // FFT_BATCH, n2 // R),
        in_specs=[pl.BlockSpec(gb.shape, lambda b, c: (0, 0)),
                  pl.BlockSpec((FFT_BATCH, 2, h1, R, D), lambda b, c: (b, 0, 0, c, 0)),
                  blk, blk, pl.BlockSpec((1, D), lambda b, c: (0, 0))],
        out_specs=blk,
        out_shape=jax.ShapeDtypeStruct((B, h1, n2, D), BF16),
        compiler_params=_params("parallel", "parallel"),
        name="fft_inv_step1",
    )(gb, bm, vx.reshape(B, h1, n2, D), x0.reshape(B, h1, n2, D), skip)
    return z.reshape(B, L, D)


def _rope_tables(L):
    pos = jnp.arange(L, dtype=jnp.int32)
    rows = (pos // GRID_W).astype(F32)
    cols = (pos % GRID_W).astype(F32)
    inv = ROPE_BASE ** (-jnp.arange(ROPE_PAIRS, dtype=F32) / ROPE_PAIRS)
    lane = jnp.arange(LANES)
    within = lane % HEAD_DIM
    axis = within // (2 * ROPE_PAIRS)
    half = (within % (2 * ROPE_PAIRS)) // ROPE_PAIRS
    ang = jnp.where(axis[None, :] == 0, rows[:, None], cols[:, None]) * inv[within % ROPE_PAIRS][None, :]
    cos, sin = jnp.cos(ang), jnp.sin(ang)
    sin_a = jnp.where(half[None, :] == 0, -sin, 0.0)
    sin_b = jnp.where(half[None, :] == 1, sin, 0.0)
    return cos, sin_a, sin_b


def _tile(L, pref):
    return min(L, pref)


def _four_step(L):
    return (2 * L) // FFT_N1 >= 32


def _conv_tables(L):
    return _fft_tables(FFT_N1, 2 * L // FFT_N1) if _four_step(L) else _dft_tables(L, _tile(L, DFT_ROW_TILE))


def _run_group(x, rowmap, shared_mod, mods, p, cache, rope_tabs, dft, kv_dtype):
    B, L, D = x.shape
    tm = _tile(L, ROW_TILE)
    rows = (1, B * L, D) if shared_mod else (B, L, D)
    tmr = _tile(rows[1], ROW_TILE)
    ctx_k, ctx_v = [], []
    for i in range(DEPTH):
        j = i // 2
        sh1, sc1, g1, sh2, sc2, g2 = (mods[i, :, s * D:(s + 1) * D].reshape(MOD_ROWS, 1, D) for s in range(6))
        n1 = p["norm1_g"][i][None]
        if i % 2 == 0:
            lam_init = 0.8 - 0.6 * math.exp(-0.3 * i)
            q, k, v = (a.reshape(B, L, D) for a in
                       _qkv(x.reshape(rows), n1, sh1, sc1, p["attn_w_qkv"][j], rowmap, rope_tabs, kv_dtype, tmr))
            ctx_k.append(k)
            ctx_v.append(v)
            mix = _attention(q, k, v, None if cache is None else cache + (j,),
                             p["attn_lambda"][j].astype(F32), p["attn_subln_g"][j][None],
                             lam_init, _tile(L, ATTN_TILE), _tile(L, ATTN_TILE),
                             N_HEADS if L <= ATTN_TILE else 1)
            w_o, b_o = p["attn_w_o"][j], None
        else:
            vx, x0 = _hyena_in(x, n1, sh1, sc1, p["hy_w_in"][j], p["hy_b_in"][j][None],
                               p["hy_conv_w"][j], p["hy_conv_b"][j][None], rowmap, tm)
            filt = _implicit_filter(L, p["filt_w1"][j], p["filt_b1"][j], p["filt_w2"][j], p["filt_b2"][j],
                                    p["filt_w3"][j], p["filt_b3"][j], p["filt_freq"][j], FILT_COL_TILE)
            skip = p["hy_skip"][j][None]
            if _four_step(L):
                fb, gb, m3, m3i = dft
                kf = _fft_filter_spectrum(m3, _fft_step1(fb, filt))
                bm = _fft_step2(m3, m3i, kf, _fft_step1(fb, vx))
                mix = _fft_inv_step1(gb, bm, vx, x0, skip)
            else:
                fre, fim, gim = dft
                tmf = _tile(L, DFT_ROW_TILE)
                kf = _filter_spectrum(_fwd_dft(fre, fim, filt, None, tmf, F32), tmf)
                y = _fwd_dft(fre, fim, vx, kf, tmf, BF16)
                mix = _inv_dft(fre, gim, y, vx, x0, skip, tm, _tile(L, ROW_TILE))
            w_o, b_o = p["hy_w_out"][j], p["hy_b_out"][j][None]
        final_g = p["final_g"][None] if i == DEPTH - 1 else None
        x = _out_ffn(mix.reshape(rows), w_o, b_o, x.reshape(rows), g1, p["norm2_g"][i][None], sh2, sc2, g2,
                     p["ffn_w_gu"][i], p["ffn_w_down"][i], final_g, rowmap, tmr).reshape(B, L, D)
    return x, ctx_k, ctx_v


def kernel(x_prompt, x_sample, cache_k, cache_v, c, c_ctx, ada_w, ada_b, norm1_g, norm2_g, attn_w_qkv, attn_lambda, attn_subln_g, attn_w_o, hy_w_in, hy_b_in, hy_conv_w, hy_conv_b, filt_w1, filt_b1, filt_w2, filt_b2, filt_w3, filt_b3, filt_freq, hy_skip, hy_w_out, hy_b_out, ffn_w_gu, ffn_w_down, final_g):
    D = D_MODEL
    nb = c.shape[0]
    cmat = jnp.concatenate([c, c_ctx[None], jnp.zeros((MOD_ROWS - nb - 1, D), F32)], axis=0)
    mods = _adaln_all(cmat, ada_w, ada_b)

    p = dict(norm1_g=norm1_g, norm2_g=norm2_g, attn_lambda=attn_lambda, attn_subln_g=attn_subln_g,
             hy_b_in=hy_b_in, hy_conv_w=hy_conv_w, hy_conv_b=hy_conv_b,
             filt_w1=filt_w1, filt_b1=filt_b1, filt_w2=filt_w2, filt_b2=filt_b2, filt_w3=filt_w3,
             filt_b3=filt_b3, filt_freq=filt_freq, hy_skip=hy_skip, hy_b_out=hy_b_out, final_g=final_g,
             attn_w_qkv=attn_w_qkv.astype(BF16), attn_w_o=attn_w_o.astype(BF16),
             hy_w_in=hy_w_in.astype(BF16), hy_w_out=hy_w_out.astype(BF16),
             ffn_w_gu=ffn_w_gu.astype(BF16), ffn_w_down=ffn_w_down.astype(BF16))

    Lp, Ls = x_prompt.shape[1], x_sample.shape[1]
    y_prompt, ctx_k, ctx_v = _run_group(x_prompt, lambda b: CTX_ROW, True, mods, p, None, None,
                                        _conv_tables(Lp), F32)
    cache = tuple(a.reshape(a.shape[0], a.shape[1], a.shape[2], D) for a in (cache_k, cache_v))
    y_sample, _, _ = _run_group(x_sample, lambda b: b, False, mods, p, cache, _rope_tables(Ls),
                                _conv_tables(Ls), BF16)

    Bp = x_prompt.shape[0]
    new_k = jnp.stack(ctx_k, axis=1).reshape(Bp, len(ctx_k), Lp, N_HEADS, 2, HEAD_DIM)
    new_v = jnp.stack(ctx_v, axis=1).reshape(Bp, len(ctx_v), Lp, N_HEADS, V_DIM)
    return y_prompt, y_sample, new_k, new_v
```

```python
import functools
import math

import jax
import jax.numpy as jnp
from jax import lax
from jax.experimental import pallas as pl
from jax.experimental.pallas import tpu as pltpu

D_MODEL = 1024
DEPTH = 4
GRID_W = 64
N_HEADS = 8
HEAD_DIM = 64
V_DIM = 2 * HEAD_DIM
ROPE_PAIRS = HEAD_DIM // 4
ROPE_BASE = 10000.0
FILT_BANDS = 16
FILT_ORDER = 64
FILT_TARGET = 1e-2
FILT_FAST_PCT = 0.3
FILT_SLOW_PCT = 1.5
FILT_EPS = 1e-6
D_FF = 2816
EPS = 1e-6
SUBLN_EPS = 1e-5

LANES = 128
MOD_ROWS = 16
CTX_ROW = 8
VMEM_LIMIT = 56 * 1024 * 1024

ROW_TILE = 1024
ATTN_TILE = 512
DFT_ROW_TILE = 256
FILT_COL_TILE = 256
COL_CHUNK = 512
FFN_CHUNK = 256
assert D_FF % FFN_CHUNK == 0 and D_MODEL % COL_CHUNK == 0

F32 = jnp.float32
BF16 = jnp.bfloat16


def _dot_split(a, b):
    a_hi, b_hi = a.astype(BF16), b.astype(BF16)
    a_lo = (a - a_hi.astype(F32)).astype(BF16)
    b_lo = (b - b_hi.astype(F32)).astype(BF16)
    dot = functools.partial(jnp.dot, preferred_element_type=F32)
    return dot(a_hi, b_hi) + (dot(a_hi, b_lo) + dot(a_lo, b_hi))


def _params(*sem):
    return pltpu.CompilerParams(dimension_semantics=sem, vmem_limit_bytes=VMEM_LIMIT)


def _lane_tile(x, reps):
    return x if reps == 1 else jnp.concatenate([x] * reps, axis=1)


def _norm_mod(x, g, shift, scale):
    ms = jnp.mean(x * x, axis=-1, keepdims=True)
    return (x * lax.rsqrt(ms + EPS) * g) * (1.0 + scale) + shift


def _mod_kernel(c_ref, w_ref, b_ref, o_ref):
    c = c_ref[...]
    a = c * jax.nn.sigmoid(c)
    o_ref[...] = _dot_split(a, w_ref[...]) + b_ref[...]


def _adaln_all(cmat, ada_w, ada_b):
    n6 = 6 * D_MODEL
    tn = 1536
    return pl.pallas_call(
        _mod_kernel,
        grid=(DEPTH, n6 // tn),
        in_specs=[
            pl.BlockSpec((MOD_ROWS, D_MODEL), lambda i, n: (0, 0)),
            pl.BlockSpec((None, D_MODEL, tn), lambda i, n: (i, 0, n)),
            pl.BlockSpec((None, 1, tn), lambda i, n: (i, 0, n)),
        ],
        out_specs=pl.BlockSpec((None, MOD_ROWS, tn), lambda i, n: (i, 0, n)),
        out_shape=jax.ShapeDtypeStruct((DEPTH, MOD_ROWS, n6), F32),
        compiler_params=_params("parallel", "parallel"),
        name="adaln",
    )(cmat, ada_w, ada_b.reshape(DEPTH, 1, n6))


def _rope_chunk(x, cos, sin_a, sin_b):
    return x * cos + pltpu.roll(x, LANES - ROPE_PAIRS, 1) * sin_a + pltpu.roll(x, ROPE_PAIRS, 1) * sin_b


def _qkv_kernel(*refs, rope, tn):
    if rope:
        x_ref, g_ref, sh_ref, sc_ref, w_ref, cos_ref, sa_ref, sb_ref, q_ref, k_ref, v_ref = refs
    else:
        x_ref, g_ref, sh_ref, sc_ref, w_ref, q_ref, k_ref, v_ref = refs
    h = _norm_mod(x_ref[...], g_ref[...], sh_ref[...], sc_ref[...]).astype(BF16)
    outs = (q_ref, k_ref, v_ref)
    qk_scale = HEAD_DIM ** -0.5 * math.log2(math.e)
    for part in range(3):
        for n in range(D_MODEL // tn):
            col = part * D_MODEL + n * tn
            y = jnp.dot(h, w_ref[:, col:col + tn], preferred_element_type=F32)
            for j in range(tn // LANES):
                yj = y[:, j * LANES:(j + 1) * LANES]
                if rope and part < 2:
                    yj = _rope_chunk(yj, cos_ref[...], sa_ref[...], sb_ref[...])
                if part == 0:
                    yj = yj * qk_scale
                lo = n * tn + j * LANES
                outs[part][:, lo:lo + LANES] = yj.astype(outs[part].dtype)


def _qkv(x, g, shift, scale, w, rowmap, rope_tabs, kv_dtype, tm):
    B, L, D = x.shape
    rope = rope_tabs is not None
    row = pl.BlockSpec((None, tm, D), lambda b, i: (b, i, 0))
    vec = pl.BlockSpec((None, 1, D), lambda b, i: (rowmap(b), 0, 0))
    in_specs = [row, pl.BlockSpec((1, D), lambda b, i: (0, 0)), vec, vec,
                pl.BlockSpec((D, 3 * D), lambda b, i: (0, 0))]
    args = [x, g, shift, scale, w]
    if rope:
        tab = pl.BlockSpec((tm, LANES), lambda b, i: (i, 0))
        in_specs += [tab, tab, tab]
        args += list(rope_tabs)
    return pl.pallas_call(
        functools.partial(_qkv_kernel, rope=rope, tn=COL_CHUNK),
        grid=(B, L // tm),
        in_specs=in_specs,
        out_specs=[row, row, row],
        out_shape=[jax.ShapeDtypeStruct((B, L, D), BF16),
                   jax.ShapeDtypeStruct((B, L, D), kv_dtype),
                   jax.ShapeDtypeStruct((B, L, D), kv_dtype)],
        compiler_params=_params("parallel", "parallel"),
        name="qkv",
    )(*args)


def _hyena_in_kernel(xp_ref, x_ref, xn_ref, g_ref, sh_ref, sc_ref, w_ref, b_ref, cw_ref, cb_ref,
                     vx_ref, x0_ref, u_sc, *, tn):
    tm, D = x_ref.shape
    i = pl.program_id(1)
    xa = jnp.concatenate([xp_ref[...], x_ref[...], xn_ref[...]], axis=0)
    h = _norm_mod(xa, g_ref[...], sh_ref[...], sc_ref[...]).astype(BF16)
    keep_top = (i > 0).astype(F32)
    keep_bot = (i < pl.num_programs(1) - 1).astype(F32)

    def part(p, n, slot):
        sl = slice(p * D + n * tn, p * D + (n + 1) * tn)
        u = jnp.dot(h, w_ref[:, sl], preferred_element_type=F32) + b_ref[:, sl]
        u_sc[slot, 0:8] = u[:8] * keep_top
        u_sc[slot, 8:8 + tm] = u[8:8 + tm]
        u_sc[slot, 8 + tm:] = u[8 + tm:] * keep_bot
        w = cw_ref[:, sl]
        return (u_sc[slot, 7:7 + tm] * w[0:1] + u_sc[slot, 8:8 + tm] * w[1:2] + u_sc[slot, 9:9 + tm] * w[2:3]
                + cb_ref[:, sl])

    for n in range(D // tn):
        out = slice(n * tn, (n + 1) * tn)
        x0_ref[:, out] = part(0, n, 0)
        vx_ref[:, out] = (part(2, n, 1) * part(1, n, 2)).astype(vx_ref.dtype)


def _hyena_in(x, g, shift, scale, w, b, conv_w, conv_b, rowmap, tm):
    B, L, D = x.shape
    N = w.shape[1]
    hb = tm // 8
    tn = COL_CHUNK
    vec = pl.BlockSpec((None, 1, D), lambda b_, i: (rowmap(b_), 0, 0))
    row = pl.BlockSpec((None, tm, D), lambda b_, i: (b_, i, 0))
    const = lambda r: pl.BlockSpec((r, N), lambda b_, i: (0, 0))
    return pl.pallas_call(
        functools.partial(_hyena_in_kernel, tn=tn),
        grid=(B, L // tm),
        in_specs=[pl.BlockSpec((None, 8, D), lambda b_, i: (b_, jnp.maximum(i * hb - 1, 0), 0)),
                  row,
                  pl.BlockSpec((None, 8, D), lambda b_, i: (b_, jnp.minimum((i + 1) * hb, L // 8 - 1), 0)),
                  pl.BlockSpec((1, D), lambda b_, i: (0, 0)), vec, vec,
                  pl.BlockSpec((D, N), lambda b_, i: (0, 0)), const(1), const(3), const(1)],
        out_specs=[row, row],
        out_shape=[jax.ShapeDtypeStruct((B, L, D), BF16), jax.ShapeDtypeStruct((B, L, D), F32)],
        scratch_shapes=[pltpu.VMEM((3, tm + 16, tn), F32)],
        compiler_params=_params("parallel", "parallel"),
        name="hyena_in",
    )(x, x, x, g, shift, scale, w, b, conv_w, conv_b)


def _attn_kernel(*refs, heads, **static):
    n_scratch = 7
    lam_ref, g_ref = refs[:2]
    for h in range(heads):
        cols = slice(h * LANES, (h + 1) * LANES)
        per_head = [r.at[:, cols] for r in refs[2:len(refs) - n_scratch]]
        _attn_head(lam_ref, g_ref, *per_head, *refs[len(refs) - n_scratch:], **static)


def _attn_head(*refs, tq, tk, lam_init, cached):
    if cached:
        lam_ref, g_ref, q_ref, ck_ref, cv_ref, k_ref, v_ref, o_ref = refs[:8]
    else:
        lam_ref, g_ref, q_ref, k_ref, v_ref, o_ref = refs[:6]
    m_sc, acc_sc, vx_sc, q2_sc, s0_sc, s1_sc, s2_sc = refs[-7:]
    nc = ck_ref.shape[0] // tk if cached else 0
    nk = nc + k_ref.shape[0] // tk
    nq = q_ref.shape[0] // tq
    if cached:
        vx_sc[:nc * tk, :V_DIM] = cv_ref[...].astype(BF16)
    vx_sc[nc * tk:, :V_DIM] = v_ref[...].astype(BF16)
    vx_sc[:, V_DIM:] = jnp.ones((vx_sc.shape[0], V_DIM), BF16)
    lp = lam_ref[...]
    lam = (jnp.exp(jnp.sum(lp[0:1] * lp[1:2], axis=1, keepdims=True))
           - jnp.exp(jnp.sum(lp[2:3] * lp[3:4], axis=1, keepdims=True)) + lam_init)
    lane = lax.broadcasted_iota(jnp.int32, (tq, LANES), 1)

    def slot(j):
        if j == 0:
            return s0_sc
        return s1_sc if j % 2 == 1 else s2_sc

    def key_tile(j):
        if j < nc:
            return ck_ref[j * tk:(j + 1) * tk, :].astype(BF16)
        return k_ref[(j - nc) * tk:(j - nc + 1) * tk, :].astype(BF16)

    def load_q(i):
        q = q_ref[pl.ds(pl.multiple_of(i * tq, tq), tq), :]
        zero = jnp.zeros_like(q)
        q2_sc[:tq] = jnp.where(lane < HEAD_DIM, q, zero)
        q2_sc[tq:] = jnp.where(lane >= HEAD_DIM, q, zero)

    def scores(j):
        s_sc = slot(j)
        s_sc[...] = lax.dot_general(q2_sc[...], key_tile(j), (((1,), (1,)), ((), ())),
                                    preferred_element_type=F32)

    def accumulate(j):
        s_sc = slot(j)
        t = jnp.broadcast_to(jnp.max(s_sc[...], axis=1, keepdims=True), m_sc.shape)
        if j == 0:
            m_new = t
        else:
            m_prev = m_sc[...]
            m_new = jnp.maximum(m_prev, t)
            alpha = jnp.exp2(m_prev - m_new)
        e = jnp.exp2(s_sc[...] - _lane_tile(m_new, tk // LANES))
        pv = jnp.dot(e.astype(BF16), vx_sc[j * tk:(j + 1) * tk, :], preferred_element_type=F32)
        acc_sc[...] = pv if j == 0 else _lane_tile(alpha, 2) * acc_sc[...] + pv
        m_sc[...] = m_new

    def finalize(i):
        o = (acc_sc[:tq, :V_DIM] / acc_sc[:tq, V_DIM:] - lam * (acc_sc[tq:, :V_DIM] / acc_sc[tq:, V_DIM:]))
        ms = jnp.mean(o * o, axis=-1, keepdims=True)
        y = (o * lax.rsqrt(ms + SUBLN_EPS) * g_ref[...]) * (1.0 - lam_init)
        o_ref[pl.ds(pl.multiple_of(i * tq, tq), tq), :] = y.astype(o_ref.dtype)

    def query_tile(i, prefetch):
        for j in range(1, nk):
            accumulate(j - 1)
            scores(j)
        if prefetch:
            load_q(i + 1)
        accumulate(nk - 1)
        if prefetch:
            scores(0)
        finalize(i)

    load_q(0)
    scores(0)

    def body(i, carry):
        query_tile(i, True)
        return carry
    lax.fori_loop(0, nq - 1, body, 0)
    query_tile(nq - 1, False)


def _attention(q, k, v, cache, lam_params, subln_g, lam_init, tq, tk, heads):
    B, Lq, D = q.shape
    Lk = k.shape[1]
    width = heads * LANES
    kv = pl.BlockSpec((None, Lk, width), lambda b, h: (b, 0, h))
    qo = pl.BlockSpec((None, Lq, width), lambda b, h: (b, 0, h))
    in_specs = [pl.BlockSpec((4, HEAD_DIM), lambda b, h: (0, 0)),
                pl.BlockSpec((1, V_DIM), lambda b, h: (0, 0)), qo]
    args = [lam_params, subln_g, q]
    past = 0
    if cache is not None:
        ck, cv, layer = cache
        past = ck.shape[2]
        cspec = pl.BlockSpec((None, None, past, width), lambda b, h: (b, layer, 0, h))
        in_specs += [cspec, cspec]
        args += [ck, cv]
    in_specs += [kv, kv]
    args += [k, v]
    return pl.pallas_call(
        functools.partial(_attn_kernel, heads=heads, tq=tq, tk=tk, lam_init=lam_init,
                          cached=cache is not None),
        grid=(B, N_HEADS // heads),
        in_specs=in_specs,
        out_specs=qo,
        out_shape=jax.ShapeDtypeStruct((B, Lq, D), BF16),
        scratch_shapes=[pltpu.VMEM((2 * tq, LANES), F32),
                        pltpu.VMEM((2 * tq, 2 * V_DIM), F32),
                        pltpu.VMEM((past + Lk, 2 * V_DIM), BF16),
                        pltpu.VMEM((2 * tq, LANES), BF16)]
                       + [pltpu.VMEM((2 * tq, tk), F32)] * 3,
        compiler_params=_params("parallel", "parallel"),
        name="diff_attention",
    )(*args)


def _resident(shape):
    return pl.BlockSpec(shape, lambda b, i: (0,) * len(shape), pipeline_mode=pl.Buffered(1))


def _out_ffn_kernel(*refs, fc, final, bias):
    refs = list(refs)
    o_ref = refs.pop()
    fg_ref = refs.pop() if final else None
    a_ref, wo_ref = refs[:2]
    bo_ref = refs[2] if bias else None
    x_ref, gate1_ref, g_ref, sh_ref, sc_ref, gate2_ref, wgu_ref, wd_ref = refs[2 + bias:]
    y = jnp.dot(a_ref[...], wo_ref[...], preferred_element_type=F32)
    if bias:
        y = y + bo_ref[...]
    x = x_ref[...] + gate1_ref[...] * y
    h = _norm_mod(x, g_ref[...], sh_ref[...], sc_ref[...]).astype(BF16)
    acc = jnp.zeros(x.shape, F32)
    for f in range(D_FF // fc):
        gg = jnp.dot(h, wgu_ref[:, f * fc:(f + 1) * fc], preferred_element_type=F32)
        uu = jnp.dot(h, wgu_ref[:, D_FF + f * fc:D_FF + (f + 1) * fc], preferred_element_type=F32)
        act = (gg * jax.nn.sigmoid(gg)) * uu
        acc = acc + jnp.dot(act.astype(BF16), wd_ref[f * fc:(f + 1) * fc, :], preferred_element_type=F32)
    out = x + gate2_ref[...] * acc
    if final:
        ms = jnp.mean(out * out, axis=-1, keepdims=True)
        out = out * lax.rsqrt(ms + EPS) * fg_ref[...]
    o_ref[...] = out


def _out_ffn(a, w_o, b_o, x, gate1, g, shift, scale, gate2, w_gu, w_down, final_g, rowmap, tm):
    B, L, D = x.shape
    final, bias = final_g is not None, b_o is not None
    row = pl.BlockSpec((None, tm, D), lambda b, i: (b, i, 0))
    vec = pl.BlockSpec((None, 1, D), lambda b, i: (rowmap(b), 0, 0))
    const = pl.BlockSpec((1, D), lambda b, i: (0, 0))
    in_specs = [row, _resident((D, D))] + ([const] if bias else [])
    args = [a, w_o] + ([b_o] if bias else [])
    in_specs += [row, vec, const, vec, vec, vec, _resident((D, 2 * D_FF)), _resident((D_FF, D))]
    args += [x, gate1, g, shift, scale, gate2, w_gu, w_down]
    if final:
        in_specs.append(const)
        args.append(final_g)
    return pl.pallas_call(
        functools.partial(_out_ffn_kernel, fc=FFN_CHUNK, final=final, bias=bias),
        grid=(B, L // tm),
        in_specs=in_specs,
        out_specs=row,
        out_shape=jax.ShapeDtypeStruct((B, L, D), F32),
        compiler_params=_params("parallel", "parallel"),
        name="out_ffn",
    )(*args)


def _filter_kernel(bands_ref, w1t_ref, w1c_ref, w1s_ref, b1_ref, w2_ref, b2_ref, fr_ref,
                   w3f_ref, w3b_ref, b3f_ref, b3b_ref, delta_ref, h_ref, hid_sc):
    L = hid_sc.shape[0]
    pos = lax.broadcasted_iota(jnp.int32, (L, LANES), 0).astype(F32)
    t = pos / max(L - 1, 1)

    @pl.when(pl.program_id(0) == 0)
    def _():
        ang = (2.0 * math.pi * pos / L) * bands_ref[...]
        fr = fr_ref[...]
        pre = (t * w1t_ref[...]
               + _dot_split(jnp.cos(ang), w1c_ref[...])
               - _dot_split(jnp.sin(ang), w1s_ref[...])
               + b1_ref[...])
        hid = jnp.sin(fr * pre)
        hid_sc[...] = jnp.sin(fr * (_dot_split(hid, w2_ref[...])
                                    + b2_ref[...]))

    hid = hid_sc[...]
    window = jnp.exp(-t[:, 0:1] * delta_ref[...])
    hf = (_dot_split(hid, w3f_ref[...]) + b3f_ref[...]) * window
    hb = (_dot_split(hid, w3b_ref[...]) + b3b_ref[...]) * window
    norm = (jnp.sum(jnp.abs(hf), axis=0, keepdims=True) + jnp.sum(jnp.abs(hb), axis=0, keepdims=True)
            + FILT_EPS)
    row = lax.broadcasted_iota(jnp.int32, hb.shape, 0)
    h_ref[0] = (hf / norm).astype(h_ref.dtype)
    h_ref[1] = jnp.where(row == 0, 0.0, hb / norm).astype(h_ref.dtype)


def _pad2(a, rows, cols):
    return jnp.pad(a, ((0, rows - a.shape[0]), (0, cols - a.shape[1])))


def _implicit_filter(L, w1, b1, w2, b2, w3, b3, freq, tc):
    D = D_MODEL
    bands = _pad2(jnp.linspace(1e-4, FILT_BANDS - 1, FILT_BANDS, dtype=F32)[None], 1, LANES)
    deltas = jnp.abs(jnp.linspace(math.log(FILT_TARGET) / FILT_SLOW_PCT,
                                  math.log(FILT_TARGET) / FILT_FAST_PCT, D, dtype=F32))[None]
    w1 = w1.astype(F32)
    args = [bands,
            _pad2(w1[0:1], 1, LANES),
            _pad2(w1[1:1 + FILT_BANDS], LANES, LANES),
            _pad2(w1[1 + FILT_BANDS:], LANES, LANES),
            _pad2(b1[None], 1, LANES), _pad2(w2, LANES, LANES), _pad2(b2[None], 1, LANES),
            _pad2(freq[None], 1, LANES),
            _pad2(w3, LANES, 2 * D), _pad2(w3, LANES, 2 * D), b3[None], b3[None], deltas]
    small = lambda shape: pl.BlockSpec(shape, lambda c: (0, 0))
    nb = D // tc
    in_specs = [small((1, LANES)), small((1, LANES)), small((LANES, LANES)), small((LANES, LANES)),
                small((1, LANES)), small((LANES, LANES)), small((1, LANES)), small((1, LANES)),
                pl.BlockSpec((LANES, tc), lambda c: (0, c)),
                pl.BlockSpec((LANES, tc), lambda c: (0, nb + c)),
                pl.BlockSpec((1, tc), lambda c: (0, c)),
                pl.BlockSpec((1, tc), lambda c: (0, nb + c)),
                pl.BlockSpec((1, tc), lambda c: (0, c))]
    return pl.pallas_call(
        _filter_kernel,
        grid=(nb,),
        in_specs=in_specs,
        out_specs=pl.BlockSpec((2, L, tc), lambda c: (0, 0, c)),
        out_shape=jax.ShapeDtypeStruct((2, L, D), BF16),
        scratch_shapes=[pltpu.VMEM((L, LANES), F32)],
        compiler_params=_params("arbitrary"),
        name="hyena_filter",
    )(*args)


def _dft_gen_kernel(fre_ref, fim_ref, gim_ref, cb_sc, sb_sc, *, L):
    n = 2 * L
    tr = fre_ref.shape[0]
    step = 2.0 * math.pi / n
    i = pl.program_id(0)
    row = lax.broadcasted_iota(jnp.int32, (tr, L), 0)
    col = lax.broadcasted_iota(jnp.int32, (tr, L), 1)

    @pl.when(i == 0)
    def _():
        ang = ((row * col) & (n - 1)).astype(F32) * step
        cb_sc[...] = jnp.cos(ang)
        sb_sc[...] = jnp.sin(ang)

    col1 = lax.broadcasted_iota(jnp.int32, (1, L), 1)
    a = (((i * tr) * col1) & (n - 1)).astype(F32) * step
    ca, sa = jnp.cos(a), jnp.sin(a)
    cb, sb = cb_sc[...], sb_sc[...]
    c = cb * ca - sb * sa
    s = -(sb * ca + cb * sa)
    grow = row + i * tr
    fre_ref[...] = c.astype(fre_ref.dtype)
    fim_ref[...] = jnp.where(grow == 0, jnp.where((col & 1) == 0, 1.0, -1.0), s).astype(fim_ref.dtype)
    gim_ref[...] = jnp.where(col == 0, jnp.where((grow & 1) == 0, 1.0, -1.0), s).astype(gim_ref.dtype)


def _dft_tables(L, tr):
    spec = pl.BlockSpec((tr, L), lambda i: (i, 0))
    shape = jax.ShapeDtypeStruct((L, L), BF16)
    return pl.pallas_call(
        functools.partial(_dft_gen_kernel, L=L),
        grid=(L // tr,),
        out_specs=[spec, spec, spec],
        out_shape=[shape, shape, shape],
        scratch_shapes=[pltpu.VMEM((tr, L), F32), pltpu.VMEM((tr, L), F32)],
        compiler_params=_params("arbitrary"),
        name="dft_tables",
    )()


def _fwd_dft_kernel(*refs, with_filter):
    if with_filter:
        fre_ref, fim_ref, v_ref, kr_ref, ki_ref, y_ref = refs
    else:
        fre_ref, fim_ref, v_ref, y_ref = refs
    v = v_ref[...]
    vr = jnp.dot(fre_ref[...], v, preferred_element_type=F32)
    vi = jnp.dot(fim_ref[...], v, preferred_element_type=F32)
    if with_filter:
        kr, ki = kr_ref[...], ki_ref[...]
        tm = vr.shape[0]
        first = (lax.broadcasted_iota(jnp.int32, vr.shape, 0) + pl.program_id(1) * tm) == 0
        yr = vr * kr - jnp.where(first, 0.0, vi * ki)
        yi = jnp.where(first, vi * ki, vr * ki + vi * kr)
        y_ref[0] = yr.astype(y_ref.dtype)
        y_ref[1] = yi.astype(y_ref.dtype)
    else:
        y_ref[0] = vr.astype(y_ref.dtype)
        y_ref[1] = vi.astype(y_ref.dtype)


def _fwd_dft(fre, fim, v, kf, tm, out_dtype):
    B, L, D = v.shape
    ftile = pl.BlockSpec((tm, L), lambda b, m: (m, 0))
    in_specs = [ftile, ftile, pl.BlockSpec((None, L, D), lambda b, m: (b, 0, 0))]
    args = [fre, fim, v]
    if kf is not None:
        in_specs += [pl.BlockSpec((None, tm, D), lambda b, m: (0, m, 0)),
                     pl.BlockSpec((None, tm, D), lambda b, m: (1, m, 0))]
        args += [kf, kf]
    return pl.pallas_call(
        functools.partial(_fwd_dft_kernel, with_filter=kf is not None),
        grid=(B, L // tm),
        in_specs=in_specs,
        out_specs=pl.BlockSpec((None, 2, tm, D), lambda b, m: (b, 0, m, 0)),
        out_shape=jax.ShapeDtypeStruct((B, 2, L, D), out_dtype),
        compiler_params=_params("parallel", "parallel"),
        name="hyena_fwd_dft",
    )(*args)


def _filter_spectrum_kernel(hf_ref, hb_ref, k_ref, *, n):
    hfr, hfi = hf_ref[0], hf_ref[1]
    hbr, hbi = hb_ref[0], hb_ref[1]
    first = (lax.broadcasted_iota(jnp.int32, hfr.shape, 0) + pl.program_id(0) * hfr.shape[0]) == 0
    wgt = jnp.where(first, 1.0 / n, 2.0 / n)
    k_ref[0] = (hfr + hbr) * wgt
    k_ref[1] = jnp.where(first, hfi + hbi, hfi - hbi) * wgt


def _filter_spectrum(hspec, tm):
    _, _, L, D = hspec.shape
    return pl.pallas_call(
        functools.partial(_filter_spectrum_kernel, n=2 * L),
        grid=(L // tm,),
        in_specs=[pl.BlockSpec((None, 2, tm, D), lambda m: (0, 0, m, 0)),
                  pl.BlockSpec((None, 2, tm, D), lambda m: (1, 0, m, 0))],
        out_specs=pl.BlockSpec((2, tm, D), lambda m: (0, m, 0)),
        out_shape=jax.ShapeDtypeStruct((2, L, D), F32),
        compiler_params=_params("parallel"),
        name="hyena_filter_spectrum",
    )(hspec, hspec)


def _inv_dft_kernel(gre_ref, gim_ref, yr_ref, yi_ref, vx_ref, x0_ref, skip_ref, z_ref, acc_sc):
    kk = pl.program_id(2)

    @pl.when(kk == 0)
    def _():
        acc_sc[...] = jnp.zeros(acc_sc.shape, F32)

    acc_sc[...] += (jnp.dot(gre_ref[...], yr_ref[...], preferred_element_type=F32)
                    + jnp.dot(gim_ref[...], yi_ref[...], preferred_element_type=F32))

    @pl.when(kk == pl.num_programs(2) - 1)
    def _():
        z = (acc_sc[...] + vx_ref[...].astype(F32) * skip_ref[...]) * x0_ref[...]
        z_ref[...] = z.astype(z_ref.dtype)


def _inv_dft(gre, gim, y, vx, x0, skip, tm, tk):
    B, L, D = vx.shape
    gtile = pl.BlockSpec((tm, tk), lambda b, i, k: (i, k))
    row = pl.BlockSpec((None, tm, D), lambda b, i, k: (b, i, 0))
    return pl.pallas_call(
        _inv_dft_kernel,
        grid=(B, L // tm, L // tk),
        in_specs=[gtile, gtile,
                  pl.BlockSpec((None, None, tk, D), lambda b, i, k: (b, 0, k, 0)),
                  pl.BlockSpec((None, None, tk, D), lambda b, i, k: (b, 1, k, 0)),
                  row, row, pl.BlockSpec((1, D), lambda b, i, k: (0, 0))],
        out_specs=row,
        out_shape=jax.ShapeDtypeStruct((B, L, D), BF16),
        scratch_shapes=[pltpu.VMEM((tm, D), F32)],
        compiler_params=_params("parallel", "parallel", "arbitrary"),
        name="hyena_inv_dft",
    )(gre, gim, y, y, vx, x0, skip)


FFT_N1 = 64
FFT_SLOT_GROUP = 8
FFT_ROWS = 16
FFT_BATCH = 2


def _cos_units(idx, n):
    return jnp.cos((idx & (n - 1)).astype(F32) * (2.0 * math.pi / n))


def _fft_step1_tables_kernel(fb_ref, gb_ref, *, n1, n2):
    n = n1 * n2
    h1, quarter, R = n1 // 2, n // 4, FFT_ROWS
    lr, lh = R.bit_length() - 1, h1.bit_length() - 1
    g = pl.program_id(0)

    def entry(part, slot, t1):
        idx = jnp.where((part == 1) & (slot == 0), (n // 2) * t1, n2 * t1 * slot + quarter * part)
        return _cos_units(idx, n)

    rows, cols = fb_ref.shape
    r = lax.broadcasted_iota(jnp.int32, (rows, cols), 0) + g * rows
    c = lax.broadcasted_iota(jnp.int32, (rows, cols), 1)
    val = entry(r >> (lr + lh), (r >> lr) & (h1 - 1), c >> lr)
    fb_ref[...] = jnp.where((r & (R - 1)) == (c & (R - 1)), val, 0.0).astype(fb_ref.dtype)
    rows, cols = gb_ref.shape
    r = lax.broadcasted_iota(jnp.int32, (rows, cols), 0) + g * rows
    c = lax.broadcasted_iota(jnp.int32, (rows, cols), 1)
    slot = (c >> lr) & (h1 - 1)
    val = jnp.where(slot == 0, 1.0, 2.0) * entry(c >> (lr + lh), slot, r >> lr)
    gb_ref[...] = jnp.where((r & (R - 1)) == (c & (R - 1)), val, 0.0).astype(gb_ref.dtype)


def _fft_step2_tables_kernel(m3_ref, m3i_ref, *, n1, n2):
    n = n1 * n2
    quarter, half = n // 4, n2 // 2
    g = pl.program_id(0)
    r = lax.broadcasted_iota(jnp.int32, (2 * n2, 2 * n2), 0)
    c = lax.broadcasted_iota(jnp.int32, (2 * n2, 2 * n2), 1)
    rj, rim = r & (n2 - 1), (r >= n2).astype(jnp.int32)
    cj, cim = c & (n2 - 1), (c >= n2).astype(jnp.int32)
    for si in range(m3_ref.shape[0]):
        s = g * m3_ref.shape[0] + si
        fwd = _cos_units(cj * (s + n1 * rj) + quarter * (rim - cim), n)
        inv = _cos_units(rj * (s + n1 * cj) + quarter * (cim - rim), n)
        if si == 0:
            def slot0(j, t2, t_im, out_im):
                lo = (j < half) & (t_im == 0)
                hi = (j >= half) & (t_im == 1)
                idx0 = jnp.where((out_im == 1) & (j == 0), (n // 2) * t2, t2 * n1 * j + quarter * out_im)
                idxh = t2 * (n1 // 2 + n1 * (j - half)) + quarter * out_im
                return lo, hi, jnp.where(lo, idx0, idxh)
            lo, hi, idx = slot0(rj, cj, cim, rim)
            fwd0 = jnp.where(lo | hi, _cos_units(idx, n), 0.0)
            lo, hi, idx = slot0(cj, rj, rim, cim)
            amp = jnp.where(lo & (cj == 0), 1.0, 2.0)
            inv0 = jnp.where(lo | hi, amp * _cos_units(idx, n), 0.0)
            fwd = jnp.where(g == 0, fwd0, fwd)
            inv = jnp.where(g == 0, inv0, inv)
        m3_ref[si] = fwd.astype(m3_ref.dtype)
        m3i_ref[si] = inv.astype(m3i_ref.dtype)


def _fft_tables(n1, n2):
    h1, sg, R = n1 // 2, FFT_SLOT_GROUP, FFT_ROWS
    steps = 8
    fb, gb = pl.pallas_call(
        functools.partial(_fft_step1_tables_kernel, n1=n1, n2=n2),
        grid=(steps,),
        out_specs=[pl.BlockSpec((2 * h1 * R // steps, h1 * R), lambda g: (g, 0)),
                   pl.BlockSpec((h1 * R // steps, 2 * h1 * R), lambda g: (g, 0))],
        out_shape=[jax.ShapeDtypeStruct((2 * h1 * R, h1 * R), BF16),
                   jax.ShapeDtypeStruct((h1 * R, 2 * h1 * R), BF16)],
        compiler_params=_params("parallel"),
        name="fft_step1_tables",
    )()
    mspec = pl.BlockSpec((sg, 2 * n2, 2 * n2), lambda g: (g, 0, 0))
    m3, m3i = pl.pallas_call(
        functools.partial(_fft_step2_tables_kernel, n1=n1, n2=n2),
        grid=(h1 // sg,),
        out_specs=[mspec, mspec],
        out_shape=[jax.ShapeDtypeStruct((h1, 2 * n2, 2 * n2), BF16)] * 2,
        compiler_params=_params("parallel"),
        name="fft_step2_tables",
    )()
    return fb, gb, m3, m3i


def _fft_step1_kernel(fb_ref, x_ref, a_ref):
    nb, h1, R, D = x_ref.shape
    for i in range(nb):
        a = jnp.dot(fb_ref[...], x_ref[i].reshape(h1 * R, D), preferred_element_type=F32)
        a_ref[i] = a.reshape(2, h1, R, D).astype(a_ref.dtype)


def _fft_step1(fb, x):
    B, L, D = x.shape
    R = FFT_ROWS
    h1 = fb.shape[1] // R
    n2 = L // h1
    return pl.pallas_call(
        _fft_step1_kernel,
        grid=(B // FFT_BATCH, n2 // R),
        in_specs=[pl.BlockSpec(fb.shape, lambda b, c: (0, 0)),
                  pl.BlockSpec((FFT_BATCH, h1, R, D), lambda b, c: (b, 0, c, 0))],
        out_specs=pl.BlockSpec((FFT_BATCH, 2, h1, R, D), lambda b, c: (b, 0, 0, c, 0)),
        out_shape=jax.ShapeDtypeStruct((B, 2, h1, n2, D), BF16),
        compiler_params=_params("parallel", "parallel"),
        name="fft_step1",
    )(fb, x.reshape(B, h1, n2, D))


def _first_bin(shape, g):
    return (lax.broadcasted_iota(jnp.int32, shape, 0) == 0) & (g == 0)


def _fft_filter_spectrum_kernel(m3_ref, a_ref, k_ref, *, n):
    n2 = a_ref.shape[-2]
    g = pl.program_id(0)
    for si in range(m3_ref.shape[0]):
        hf = jnp.dot(m3_ref[si], jnp.concatenate([a_ref[0, 0, si], a_ref[0, 1, si]], axis=0),
                     preferred_element_type=F32)
        hb = jnp.dot(m3_ref[si], jnp.concatenate([a_ref[1, 0, si], a_ref[1, 1, si]], axis=0),
                     preferred_element_type=F32)
        k_ref[0, si] = (hf[:n2] + hb[:n2]) * (1.0 / n)
        ki = hf[n2:] - hb[n2:]
        if si == 0:
            ki = jnp.where(_first_bin(ki.shape, g), hf[n2:] + hb[n2:], ki)
        k_ref[1, si] = ki * (1.0 / n)


def _fft_filter_spectrum(m3, a):
    h1, n2 = m3.shape[0], m3.shape[1] // 2
    D = a.shape[-1]
    sg = FFT_SLOT_GROUP
    return pl.pallas_call(
        functools.partial(_fft_filter_spectrum_kernel, n=2 * h1 * n2),
        grid=(h1 // sg,),
        in_specs=[pl.BlockSpec((sg, 2 * n2, 2 * n2), lambda g: (g, 0, 0)),
                  pl.BlockSpec((2, 2, sg, n2, D), lambda g: (0, 0, g, 0, 0))],
        out_specs=pl.BlockSpec((2, sg, n2, D), lambda g: (0, g, 0, 0)),
        out_shape=jax.ShapeDtypeStruct((2, h1, n2, D), F32),
        compiler_params=_params("parallel"),
        name="fft_filter_spectrum",
    )(m3, a)


def _fft_step2_kernel(m3_ref, m3i_ref, kr_ref, ki_ref, a_ref, b_ref, y_sc):
    n2 = a_ref.shape[-2]
    g = pl.program_id(0)
    sg = m3_ref.shape[0]
    for si in range(sg):
        xh = jnp.dot(m3_ref[si], a_ref[:, si].reshape(2 * n2, a_ref.shape[-1]), preferred_element_type=F32)
        xr, xi = xh[:n2], xh[n2:]
        kr, ki = kr_ref[si], ki_ref[si]
        yr = xr * kr - xi * ki
        yi = xr * ki + xi * kr
        if si == 0:
            first = _first_bin(yr.shape, g)
            yr = jnp.where(first, xr * kr, yr)
            yi = jnp.where(first, xi * ki, yi)
        y_sc[si, :n2] = yr.astype(BF16)
        y_sc[si, n2:] = yi.astype(BF16)
    for si in range(sg):
        bb = jnp.dot(m3i_ref[si], y_sc[si], preferred_element_type=F32)
        b_ref[0, si] = bb[:n2].astype(b_ref.dtype)
        b_ref[1, si] = bb[n2:].astype(b_ref.dtype)


def _fft_step2(m3, m3i, kf, a):
    B, _, h1, n2, D = a.shape
    sg = FFT_SLOT_GROUP
    mspec = pl.BlockSpec((sg, 2 * n2, 2 * n2), lambda g, b: (g, 0, 0))
    blk = pl.BlockSpec((None, 2, sg, n2, D), lambda g, b: (b, 0, g, 0, 0))
    return pl.pallas_call(
        _fft_step2_kernel,
        grid=(h1 // sg, B),
        in_specs=[mspec, mspec,
                  pl.BlockSpec((None, sg, n2, D), lambda g, b: (0, g, 0, 0)),
                  pl.BlockSpec((None, sg, n2, D), lambda g, b: (1, g, 0, 0)),
                  blk],
        out_specs=blk,
        out_shape=jax.ShapeDtypeStruct((B, 2, h1, n2, D), BF16),
        scratch_shapes=[pltpu.VMEM((sg, 2 * n2, D), BF16)],
        compiler_params=_params("parallel", "parallel"),
        name="fft_step2",
    )(m3, m3i, kf, kf, a)


def _fft_inv_step1_kernel(gb_ref, b_ref, vx_ref, x0_ref, skip_ref, z_ref):
    nb, _, h1, R, D = b_ref.shape
    for i in range(nb):
        y = jnp.dot(gb_ref[...], b_ref[i].reshape(2 * h1 * R, D), preferred_element_type=F32)
        z = (y.reshape(h1, R, D) + vx_ref[i].astype(F32) * skip_ref[...]) * x0_ref[i]
        z_ref[i] = z.astype(z_ref.dtype)


def _fft_inv_step1(gb, bm, vx, x0, skip):
    B, L, D = vx.shape
    R = FFT_ROWS
    h1 = gb.shape[0] // R
    n2 = L // h1
    blk = pl.BlockSpec((FFT_BATCH, h1, R, D), lambda b, c: (b, 0, c, 0))
    z = pl.pallas_call(
        _fft_inv_step1_kernel,
        grid=(B // FFT_BATCH, n2 // R),
        in_specs=[pl.BlockSpec(gb.shape, lambda b, c: (0, 0)),
                  pl.BlockSpec((FFT_BATCH, 2, h1, R, D), lambda b, c: (b, 0, 0, c, 0)),
                  blk, blk, pl.BlockSpec((1, D), lambda b, c: (0, 0))],
        out_specs=blk,
        out_shape=jax.ShapeDtypeStruct((B, h1, n2, D), BF16),
        compiler_params=_params("parallel", "parallel"),
        name="fft_inv_step1",
    )(gb, bm, vx.reshape(B, h1, n2, D), x0.reshape(B, h1, n2, D), skip)
    return z.reshape(B, L, D)


def _rope_tables(L):
    pos = jnp.arange(L, dtype=jnp.int32)
    rows = (pos // GRID_W).astype(F32)
    cols = (pos % GRID_W).astype(F32)
    inv = ROPE_BASE ** (-jnp.arange(ROPE_PAIRS, dtype=F32) / ROPE_PAIRS)
    lane = jnp.arange(LANES)
    within = lane % HEAD_DIM
    axis = within // (2 * ROPE_PAIRS)
    half = (within % (2 * ROPE_PAIRS)) // ROPE_PAIRS
    ang = jnp.where(axis[None, :] == 0, rows[:, None], cols[:, None]) * inv[within % ROPE_PAIRS][None, :]
    cos, sin = jnp.cos(ang), jnp.sin(ang)
    sin_a = jnp.where(half[None, :] == 0, -sin, 0.0)
    sin_b = jnp.where(half[None, :] == 1, sin, 0.0)
    return cos, sin_a, sin_b


def _tile(L, pref):
    return min(L, pref)


def _four_step(L):
    return (2 * L) // FFT_N1 >= 32


def _conv_tables(L):
    return _fft_tables(FFT_N1, 2 * L // FFT_N1) if _four_step(L) else _dft_tables(L, _tile(L, DFT_ROW_TILE))


def _run_group(x, rowmap, shared_mod, mods, p, cache, rope_tabs, dft, kv_dtype):
    B, L, D = x.shape
    tm = _tile(L, ROW_TILE)
    rows = (1, B * L, D) if shared_mod else (B, L, D)
    tmr = _tile(rows[1], ROW_TILE)
    ctx_k, ctx_v = [], []
    for i in range(DEPTH):
        j = i // 2
        sh1, sc1, g1, sh2, sc2, g2 = (mods[i, :, s * D:(s + 1) * D].reshape(MOD_ROWS, 1, D) for s in range(6))
        n1 = p["norm1_g"][i][None]
        if i % 2 == 0:
            lam_init = 0.8 - 0.6 * math.exp(-0.3 * i)
            q, k, v = (a.reshape(B, L, D) for a in
                       _qkv(x.reshape(rows), n1, sh1, sc1, p["attn_w_qkv"][j], rowmap, rope_tabs, kv_dtype, tmr))
            ctx_k.append(k)
            ctx_v.append(v)
            mix = _attention(q, k, v, None if cache is None else cache + (j,),
                             p["attn_lambda"][j].astype(F32), p["attn_subln_g"][j][None],
                             lam_init, _tile(L, ATTN_TILE), _tile(L, ATTN_TILE),
                             N_HEADS if L <= ATTN_TILE else 1)
            w_o, b_o = p["attn_w_o"][j], None
        else:
            vx, x0 = _hyena_in(x, n1, sh1, sc1, p["hy_w_in"][j], p["hy_b_in"][j][None],
                               p["hy_conv_w"][j], p["hy_conv_b"][j][None], rowmap, tm)
            filt = _implicit_filter(L, p["filt_w1"][j], p["filt_b1"][j], p["filt_w2"][j], p["filt_b2"][j],
                                    p["filt_w3"][j], p["filt_b3"][j], p["filt_freq"][j], FILT_COL_TILE)
            skip = p["hy_skip"][j][None]
            if _four_step(L):
                fb, gb, m3, m3i = dft
                kf = _fft_filter_spectrum(m3, _fft_step1(fb, filt))
                bm = _fft_step2(m3, m3i, kf, _fft_step1(fb, vx))
                mix = _fft_inv_step1(gb, bm, vx, x0, skip)
            else:
                fre, fim, gim = dft
                tmf = _tile(L, DFT_ROW_TILE)
                kf = _filter_spectrum(_fwd_dft(fre, fim, filt, None, tmf, F32), tmf)
                y = _fwd_dft(fre, fim, vx, kf, tmf, BF16)
                mix = _inv_dft(fre, gim, y, vx, x0, skip, tm, _tile(L, ROW_TILE))
            w_o, b_o = p["hy_w_out"][j], p["hy_b_out"][j][None]
        final_g = p["final_g"][None] if i == DEPTH - 1 else None
        x = _out_ffn(mix.reshape(rows), w_o, b_o, x.reshape(rows), g1, p["norm2_g"][i][None], sh2, sc2, g2,
                     p["ffn_w_gu"][i], p["ffn_w_down"][i], final_g, rowmap, tmr).reshape(B, L, D)
    return x, ctx_k, ctx_v


def kernel(x_prompt, x_sample, cache_k, cache_v, c, c_ctx, ada_w, ada_b, norm1_g, norm2_g, attn_w_qkv, attn_lambda, attn_subln_g, attn_w_o, hy_w_in, hy_b_in, hy_conv_w, hy_conv_b, filt_w1, filt_b1, filt_w2, filt_b2, filt_w3, filt_b3, filt_freq, hy_skip, hy_w_out, hy_b_out, ffn_w_gu, ffn_w_down, final_g):
    D = D_MODEL
    nb = c.shape[0]
    cmat = jnp.concatenate([c, c_ctx[None], jnp.zeros((MOD_ROWS - nb - 1, D), F32)], axis=0)
    mods = _adaln_all(cmat, ada_w, ada_b)

    p = dict(norm1_g=norm1_g, norm2_g=norm2_g, attn_lambda=attn_lambda, attn_subln_g=attn_subln_g,
             hy_b_in=hy_b_in, hy_conv_w=hy_conv_w, hy_conv_b=hy_conv_b,
             filt_w1=filt_w1, filt_b1=filt_b1, filt_w2=filt_w2, filt_b2=filt_b2, filt_w3=filt_w3,
             filt_b3=filt_b3, filt_freq=filt_freq, hy_skip=hy_skip, hy_b_out=hy_b_out, final_g=final_g,
             attn_w_qkv=attn_w_qkv.astype(BF16), attn_w_o=attn_w_o.astype(BF16),
             hy_w_in=hy_w_in.astype(BF16), hy_w_out=hy_w_out.astype(BF16),
             ffn_w_gu=ffn_w_gu.astype(BF16), ffn_w_down=ffn_w_down.astype(BF16))

    Lp, Ls = x_prompt.shape[1], x_sample.shape[1]
    y_prompt, ctx_k, ctx_v = _run_group(x_prompt, lambda b: CTX_ROW, True, mods, p, None, None,
                                        _conv_tables(Lp), F32)
    cache = tuple(a.reshape(a.shape[0], a.shape[1], a.shape[2], D) for a in (cache_k, cache_v))
    y_sample, _, _ = _run_group(x_sample, lambda b: b, False, mods, p, cache, _rope_tables(Ls),
                                _conv_tables(Ls), BF16)

    Bp = x_prompt.shape[0]
    new_k = jnp.stack(ctx_k, axis=1).reshape(Bp, len(ctx_k), Lp, N_HEADS, 2, HEAD_DIM)
    new_v = jnp.stack(ctx_v, axis=1).reshape(Bp, len(ctx_v), Lp, N_HEADS, V_DIM)
    return y_prompt, y_sample, new_k, new_v
```

```python
import functools
import math

import jax
import jax.numpy as jnp
from jax import lax
from jax.experimental import pallas as pl
from jax.experimental.pallas import tpu as pltpu

D_MODEL = 1024
DEPTH = 4
GRID_W = 64
N_HEADS = 8
HEAD_DIM = 64
V_DIM = 2 * HEAD_DIM
ROPE_PAIRS = HEAD_DIM // 4
ROPE_BASE = 10000.0
FILT_BANDS = 16
FILT_ORDER = 64
FILT_TARGET = 1e-2
FILT_FAST_PCT = 0.3
FILT_SLOW_PCT = 1.5
FILT_EPS = 1e-6
D_FF = 2816
EPS = 1e-6
SUBLN_EPS = 1e-5

LANES = 128
MOD_ROWS = 16
CTX_ROW = 8
VMEM_LIMIT = 56 * 1024 * 1024

ROW_TILE = 1024
ATTN_TILE = 512
DFT_ROW_TILE = 256
FILT_COL_TILE = 256
COL_CHUNK = 512
FFN_CHUNK = 256
assert D_FF % FFN_CHUNK == 0 and D_MODEL % COL_CHUNK == 0

F32 = jnp.float32
BF16 = jnp.bfloat16


def _dot_split(a, b):
    a_hi, b_hi = a.astype(BF16), b.astype(BF16)
    a_lo = (a - a_hi.astype(F32)).astype(BF16)
    b_lo = (b - b_hi.astype(F32)).astype(BF16)
    dot = functools.partial(jnp.dot, preferred_element_type=F32)
    return dot(a_hi, b_hi) + (dot(a_hi, b_lo) + dot(a_lo, b_hi))


def _params(*sem):
    return pltpu.CompilerParams(dimension_semantics=sem, vmem_limit_bytes=VMEM_LIMIT)


def _lane_tile(x, reps):
    return x if reps == 1 else jnp.concatenate([x] * reps, axis=1)


def _norm_mod(x, g, shift, scale):
    ms = jnp.mean(x * x, axis=-1, keepdims=True)
    return (x * lax.rsqrt(ms + EPS) * g) * (1.0 + scale) + shift


def _mod_kernel(c_ref, w_ref, b_ref, o_ref):
    c = c_ref[...]
    a = c * jax.nn.sigmoid(c)
    o_ref[...] = _dot_split(a, w_ref[...]) + b_ref[...]


def _adaln_all(cmat, ada_w, ada_b):
    n6 = 6 * D_MODEL
    tn = 1536
    return pl.pallas_call(
        _mod_kernel,
        grid=(DEPTH, n6 // tn),
        in_specs=[
            pl.BlockSpec((MOD_ROWS, D_MODEL), lambda i, n: (0, 0)),
            pl.BlockSpec((None, D_MODEL, tn), lambda i, n: (i, 0, n)),
            pl.BlockSpec((None, 1, tn), lambda i, n: (i, 0, n)),
        ],
        out_specs=pl.BlockSpec((None, MOD_ROWS, tn), lambda i, n: (i, 0, n)),
        out_shape=jax.ShapeDtypeStruct((DEPTH, MOD_ROWS, n6), F32),
        compiler_params=_params("parallel", "parallel"),
        name="adaln",
    )(cmat, ada_w, ada_b.reshape(DEPTH, 1, n6))


def _rope_chunk(x, cos, sin_a, sin_b):
    return x * cos + pltpu.roll(x, LANES - ROPE_PAIRS, 1) * sin_a + pltpu.roll(x, ROPE_PAIRS, 1) * sin_b


def _qkv_kernel(*refs, rope, tn):
    if rope:
        x_ref, g_ref, sh_ref, sc_ref, w_ref, cos_ref, sa_ref, sb_ref, q_ref, k_ref, v_ref = refs
    else:
        x_ref, g_ref, sh_ref, sc_ref, w_ref, q_ref, k_ref, v_ref = refs
    h = _norm_mod(x_ref[...], g_ref[...], sh_ref[...], sc_ref[...]).astype(BF16)
    outs = (q_ref, k_ref, v_ref)
    qk_scale = HEAD_DIM ** -0.5 * math.log2(math.e)
    for part in range(3):
        for n in range(D_MODEL // tn):
            col = part * D_MODEL + n * tn
            y = jnp.dot(h, w_ref[:, col:col + tn], preferred_element_type=F32)
            for j in range(tn // LANES):
                yj = y[:, j * LANES:(j + 1) * LANES]
                if rope and part < 2:
                    yj = _rope_chunk(yj, cos_ref[...], sa_ref[...], sb_ref[...])
                if part == 0:
                    yj = yj * qk_scale
                lo = n * tn + j * LANES
                outs[part][:, lo:lo + LANES] = yj.astype(outs[part].dtype)


def _qkv(x, g, shift, scale, w, rowmap, rope_tabs, kv_dtype, tm):
    B, L, D = x.shape
    rope = rope_tabs is not None
    row = pl.BlockSpec((None, tm, D), lambda b, i: (b, i, 0))
    vec = pl.BlockSpec((None, 1, D), lambda b, i: (rowmap(b), 0, 0))
    in_specs = [row, pl.BlockSpec((1, D), lambda b, i: (0, 0)), vec, vec,
                pl.BlockSpec((D, 3 * D), lambda b, i: (0, 0))]
    args = [x, g, shift, scale, w]
    if rope:
        tab = pl.BlockSpec((tm, LANES), lambda b, i: (i, 0))
        in_specs += [tab, tab, tab]
        args += list(rope_tabs)
    return pl.pallas_call(
        functools.partial(_qkv_kernel, rope=rope, tn=COL_CHUNK),
        grid=(B, L // tm),
        in_specs=in_specs,
        out_specs=[row, row, row],
        out_shape=[jax.ShapeDtypeStruct((B, L, D), BF16),
                   jax.ShapeDtypeStruct((B, L, D), kv_dtype),
                   jax.ShapeDtypeStruct((B, L, D), kv_dtype)],
        compiler_params=_params("parallel", "parallel"),
        name="qkv",
    )(*args)


def _hyena_in_kernel(xp_ref, x_ref, xn_ref, g_ref, sh_ref, sc_ref, w_ref, b_ref, cw_ref, cb_ref,
                     vx_ref, x0_ref, u_sc, *, tn):
    tm, D = x_ref.shape
    i = pl.program_id(1)
    xa = jnp.concatenate([xp_ref[...], x_ref[...], xn_ref[...]], axis=0)
    h = _norm_mod(xa, g_ref[...], sh_ref[...], sc_ref[...]).astype(BF16)
    keep_top = (i > 0).astype(F32)
    keep_bot = (i < pl.num_programs(1) - 1).astype(F32)

    def part(p, n, slot):
        sl = slice(p * D + n * tn, p * D + (n + 1) * tn)
        u = jnp.dot(h, w_ref[:, sl], preferred_element_type=F32) + b_ref[:, sl]
        u_sc[slot, 0:8] = u[:8] * keep_top
        u_sc[slot, 8:8 + tm] = u[8:8 + tm]
        u_sc[slot, 8 + tm:] = u[8 + tm:] * keep_bot
        w = cw_ref[:, sl]
        return (u_sc[slot, 7:7 + tm] * w[0:1] + u_sc[slot, 8:8 + tm] * w[1:2] + u_sc[slot, 9:9 + tm] * w[2:3]
                + cb_ref[:, sl])

    for n in range(D // tn):
        out = slice(n * tn, (n + 1) * tn)
        x0_ref[:, out] = part(0, n, 0).astype(x0_ref.dtype)
        vx_ref[:, out] = (part(2, n, 1) * part(1, n, 2)).astype(vx_ref.dtype)


def _hyena_in(x, g, shift, scale, w, b, conv_w, conv_b, rowmap, tm):
    B, L, D = x.shape
    N = w.shape[1]
    hb = tm // 8
    tn = COL_CHUNK
    vec = pl.BlockSpec((None, 1, D), lambda b_, i: (rowmap(b_), 0, 0))
    row = pl.BlockSpec((None, tm, D), lambda b_, i: (b_, i, 0))
    const = lambda r: pl.BlockSpec((r, N), lambda b_, i: (0, 0))
    return pl.pallas_call(
        functools.partial(_hyena_in_kernel, tn=tn),
        grid=(B, L // tm),
        in_specs=[pl.BlockSpec((None, 8, D), lambda b_, i: (b_, jnp.maximum(i * hb - 1, 0), 0)),
                  row,
                  pl.BlockSpec((None, 8, D), lambda b_, i: (b_, jnp.minimum((i + 1) * hb, L // 8 - 1), 0)),
                  pl.BlockSpec((1, D), lambda b_, i: (0, 0)), vec, vec,
                  pl.BlockSpec((D, N), lambda b_, i: (0, 0)), const(1), const(3), const(1)],
        out_specs=[row, row],
        out_shape=[jax.ShapeDtypeStruct((B, L, D), BF16), jax.ShapeDtypeStruct((B, L, D), BF16)],
        scratch_shapes=[pltpu.VMEM((3, tm + 16, tn), F32)],
        compiler_params=_params("parallel", "parallel"),
        name="hyena_in",
    )(x, x, x, g, shift, scale, w, b, conv_w, conv_b)


def _attn_kernel(*refs, heads, **static):
    n_scratch = 7
    lam_ref, g_ref = refs[:2]
    for h in range(heads):
        cols = slice(h * LANES, (h + 1) * LANES)
        per_head = [r.at[:, cols] for r in refs[2:len(refs) - n_scratch]]
        _attn_head(lam_ref, g_ref, *per_head, *refs[len(refs) - n_scratch:], **static)


def _attn_head(*refs, tq, tk, lam_init, cached):
    if cached:
        lam_ref, g_ref, q_ref, ck_ref, cv_ref, k_ref, v_ref, o_ref = refs[:8]
    else:
        lam_ref, g_ref, q_ref, k_ref, v_ref, o_ref = refs[:6]
    m_sc, acc_sc, vx_sc, q2_sc, s0_sc, s1_sc, s2_sc = refs[-7:]
    nc = ck_ref.shape[0] // tk if cached else 0
    nk = nc + k_ref.shape[0] // tk
    nq = q_ref.shape[0] // tq
    if cached:
        vx_sc[:nc * tk, :V_DIM] = cv_ref[...].astype(BF16)
    vx_sc[nc * tk:, :V_DIM] = v_ref[...].astype(BF16)
    vx_sc[:, V_DIM:] = jnp.ones((vx_sc.shape[0], V_DIM), BF16)
    lp = lam_ref[...]
    lam = (jnp.exp(jnp.sum(lp[0:1] * lp[1:2], axis=1, keepdims=True))
           - jnp.exp(jnp.sum(lp[2:3] * lp[3:4], axis=1, keepdims=True)) + lam_init)
    lane = lax.broadcasted_iota(jnp.int32, (tq, LANES), 1)

    def slot(j):
        if j == 0:
            return s0_sc
        return s1_sc if j % 2 == 1 else s2_sc

    def key_tile(j):
        if j < nc:
            return ck_ref[j * tk:(j + 1) * tk, :].astype(BF16)
        return k_ref[(j - nc) * tk:(j - nc + 1) * tk, :].astype(BF16)

    def load_q(i):
        q = q_ref[pl.ds(pl.multiple_of(i * tq, tq), tq), :]
        zero = jnp.zeros_like(q)
        q2_sc[:tq] = jnp.where(lane < HEAD_DIM, q, zero)
        q2_sc[tq:] = jnp.where(lane >= HEAD_DIM, q, zero)

    def scores(j):
        s_sc = slot(j)
        s_sc[...] = lax.dot_general(q2_sc[...], key_tile(j), (((1,), (1,)), ((), ())),
                                    preferred_element_type=F32)

    def accumulate(j):
        s_sc = slot(j)
        t = jnp.broadcast_to(jnp.max(s_sc[...], axis=1, keepdims=True), m_sc.shape)
        if j == 0:
            m_new = t
        else:
            m_prev = m_sc[...]
            m_new = jnp.maximum(m_prev, t)
            alpha = jnp.exp2(m_prev - m_new)
        e = jnp.exp2(s_sc[...] - _lane_tile(m_new, tk // LANES))
        pv = jnp.dot(e.astype(BF16), vx_sc[j * tk:(j + 1) * tk, :], preferred_element_type=F32)
        acc_sc[...] = pv if j == 0 else _lane_tile(alpha, 2) * acc_sc[...] + pv
        m_sc[...] = m_new

    def finalize(i):
        o = (acc_sc[:tq, :V_DIM] / acc_sc[:tq, V_DIM:] - lam * (acc_sc[tq:, :V_DIM] / acc_sc[tq:, V_DIM:]))
        ms = jnp.mean(o * o, axis=-1, keepdims=True)
        y = (o * lax.rsqrt(ms + SUBLN_EPS) * g_ref[...]) * (1.0 - lam_init)
        o_ref[pl.ds(pl.multiple_of(i * tq, tq), tq), :] = y.astype(o_ref.dtype)

    def query_tile(i, prefetch):
        for j in range(1, nk):
            accumulate(j - 1)
            scores(j)
        if prefetch:
            load_q(i + 1)
        accumulate(nk - 1)
        if prefetch:
            scores(0)
        finalize(i)

    load_q(0)
    scores(0)

    def body(i, carry):
        query_tile(i, True)
        return carry
    lax.fori_loop(0, nq - 1, body, 0)
    query_tile(nq - 1, False)


def _attention(q, k, v, cache, lam_params, subln_g, lam_init, tq, tk, heads):
    B, Lq, D = q.shape
    Lk = k.shape[1]
    width = heads * LANES
    kv = pl.BlockSpec((None, Lk, width), lambda b, h: (b, 0, h))
    qo = pl.BlockSpec((None, Lq, width), lambda b, h: (b, 0, h))
    in_specs = [pl.BlockSpec((4, HEAD_DIM), lambda b, h: (0, 0)),
                pl.BlockSpec((1, V_DIM), lambda b, h: (0, 0)), qo]
    args = [lam_params, subln_g, q]
    past = 0
    if cache is not None:
        ck, cv, layer = cache
        past = ck.shape[2]
        cspec = pl.BlockSpec((None, None, past, width), lambda b, h: (b, layer, 0, h))
        in_specs += [cspec, cspec]
        args += [ck, cv]
    in_specs += [kv, kv]
    args += [k, v]
    return pl.pallas_call(
        functools.partial(_attn_kernel, heads=heads, tq=tq, tk=tk, lam_init=lam_init,
                          cached=cache is not None),
        grid=(B, N_HEADS // heads),
        in_specs=in_specs,
        out_specs=qo,
        out_shape=jax.ShapeDtypeStruct((B, Lq, D), BF16),
        scratch_shapes=[pltpu.VMEM((2 * tq, LANES), F32),
                        pltpu.VMEM((2 * tq, 2 * V_DIM), F32),
                        pltpu.VMEM((past + Lk, 2 * V_DIM), BF16),
                        pltpu.VMEM((2 * tq, LANES), BF16)]
                       + [pltpu.VMEM((2 * tq, tk), F32)] * 3,
        compiler_params=_params("parallel", "parallel"),
        name="diff_attention",
    )(*args)


def _resident(shape):
    return pl.BlockSpec(shape, lambda b, i: (0,) * len(shape), pipeline_mode=pl.Buffered(1))


def _out_ffn_kernel(*refs, fc, final, bias):
    refs = list(refs)
    o_ref = refs.pop()
    fg_ref = refs.pop() if final else None
    a_ref, wo_ref = refs[:2]
    bo_ref = refs[2] if bias else None
    x_ref, gate1_ref, g_ref, sh_ref, sc_ref, gate2_ref, wgu_ref, wd_ref = refs[2 + bias:]
    y = jnp.dot(a_ref[...], wo_ref[...], preferred_element_type=F32)
    if bias:
        y = y + bo_ref[...]
    x = x_ref[...] + gate1_ref[...] * y
    h = _norm_mod(x, g_ref[...], sh_ref[...], sc_ref[...]).astype(BF16)
    acc = jnp.zeros(x.shape, F32)
    for f in range(D_FF // fc):
        gg = jnp.dot(h, wgu_ref[:, f * fc:(f + 1) * fc], preferred_element_type=F32)
        uu = jnp.dot(h, wgu_ref[:, D_FF + f * fc:D_FF + (f + 1) * fc], preferred_element_type=F32)
        act = (gg * jax.nn.sigmoid(gg)) * uu
        acc = acc + jnp.dot(act.astype(BF16), wd_ref[f * fc:(f + 1) * fc, :], preferred_element_type=F32)
    out = x + gate2_ref[...] * acc
    if final:
        ms = jnp.mean(out * out, axis=-1, keepdims=True)
        out = out * lax.rsqrt(ms + EPS) * fg_ref[...]
    o_ref[...] = out


def _out_ffn(a, w_o, b_o, x, gate1, g, shift, scale, gate2, w_gu, w_down, final_g, rowmap, tm):
    B, L, D = x.shape
    final, bias = final_g is not None, b_o is not None
    row = pl.BlockSpec((None, tm, D), lambda b, i: (b, i, 0))
    vec = pl.BlockSpec((None, 1, D), lambda b, i: (rowmap(b), 0, 0))
    const = pl.BlockSpec((1, D), lambda b, i: (0, 0))
    in_specs = [row, _resident((D, D))] + ([const] if bias else [])
    args = [a, w_o] + ([b_o] if bias else [])
    in_specs += [row, vec, const, vec, vec, vec, _resident((D, 2 * D_FF)), _resident((D_FF, D))]
    args += [x, gate1, g, shift, scale, gate2, w_gu, w_down]
    if final:
        in_specs.append(const)
        args.append(final_g)
    return pl.pallas_call(
        functools.partial(_out_ffn_kernel, fc=FFN_CHUNK, final=final, bias=bias),
        grid=(B, L // tm),
        in_specs=in_specs,
        out_specs=row,
        out_shape=jax.ShapeDtypeStruct((B, L, D), F32),
        compiler_params=_params("parallel", "parallel"),
        name="out_ffn",
    )(*args)


def _filter_kernel(bands_ref, w1t_ref, w1c_ref, w1s_ref, b1_ref, w2_ref, b2_ref, fr_ref,
                   w3f_ref, w3b_ref, b3f_ref, b3b_ref, delta_ref, h_ref, hid_sc):
    L = hid_sc.shape[0]
    pos = lax.broadcasted_iota(jnp.int32, (L, LANES), 0).astype(F32)
    t = pos / max(L - 1, 1)

    @pl.when(pl.program_id(0) == 0)
    def _():
        ang = (2.0 * math.pi * pos / L) * bands_ref[...]
        fr = fr_ref[...]
        pre = (t * w1t_ref[...]
               + _dot_split(jnp.cos(ang), w1c_ref[...])
               - _dot_split(jnp.sin(ang), w1s_ref[...])
               + b1_ref[...])
        hid = jnp.sin(fr * pre)
        hid_sc[...] = jnp.sin(fr * (_dot_split(hid, w2_ref[...])
                                    + b2_ref[...]))

    hid = hid_sc[...]
    window = jnp.exp(-t[:, 0:1] * delta_ref[...])
    hf = (_dot_split(hid, w3f_ref[...]) + b3f_ref[...]) * window
    hb = (_dot_split(hid, w3b_ref[...]) + b3b_ref[...]) * window
    norm = (jnp.sum(jnp.abs(hf), axis=0, keepdims=True) + jnp.sum(jnp.abs(hb), axis=0, keepdims=True)
            + FILT_EPS)
    row = lax.broadcasted_iota(jnp.int32, hb.shape, 0)
    h_ref[0] = (hf / norm).astype(h_ref.dtype)
    h_ref[1] = jnp.where(row == 0, 0.0, hb / norm).astype(h_ref.dtype)


def _pad2(a, rows, cols):
    return jnp.pad(a, ((0, rows - a.shape[0]), (0, cols - a.shape[1])))


def _implicit_filter(L, w1, b1, w2, b2, w3, b3, freq, tc):
    D = D_MODEL
    bands = _pad2(jnp.linspace(1e-4, FILT_BANDS - 1, FILT_BANDS, dtype=F32)[None], 1, LANES)
    deltas = jnp.abs(jnp.linspace(math.log(FILT_TARGET) / FILT_SLOW_PCT,
                                  math.log(FILT_TARGET) / FILT_FAST_PCT, D, dtype=F32))[None]
    w1 = w1.astype(F32)
    args = [bands,
            _pad2(w1[0:1], 1, LANES),
            _pad2(w1[1:1 + FILT_BANDS], LANES, LANES),
            _pad2(w1[1 + FILT_BANDS:], LANES, LANES),
            _pad2(b1[None], 1, LANES), _pad2(w2, LANES, LANES), _pad2(b2[None], 1, LANES),
            _pad2(freq[None], 1, LANES),
            _pad2(w3, LANES, 2 * D), _pad2(w3, LANES, 2 * D), b3[None], b3[None], deltas]
    small = lambda shape: pl.BlockSpec(shape, lambda c: (0, 0))
    nb = D // tc
    in_specs = [small((1, LANES)), small((1, LANES)), small((LANES, LANES)), small((LANES, LANES)),
                small((1, LANES)), small((LANES, LANES)), small((1, LANES)), small((1, LANES)),
                pl.BlockSpec((LANES, tc), lambda c: (0, c)),
                pl.BlockSpec((LANES, tc), lambda c: (0, nb + c)),
                pl.BlockSpec((1, tc), lambda c: (0, c)),
                pl.BlockSpec((1, tc), lambda c: (0, nb + c)),
                pl.BlockSpec((1, tc), lambda c: (0, c))]
    return pl.pallas_call(
        _filter_kernel,
        grid=(nb,),
        in_specs=in_specs,
        out_specs=pl.BlockSpec((2, L, tc), lambda c: (0, 0, c)),
        out_shape=jax.ShapeDtypeStruct((2, L, D), BF16),
        scratch_shapes=[pltpu.VMEM((L, LANES), F32)],
        compiler_params=_params("arbitrary"),
        name="hyena_filter",
    )(*args)


def _dft_gen_kernel(fre_ref, fim_ref, gim_ref, cb_sc, sb_sc, *, L):
    n = 2 * L
    tr = fre_ref.shape[0]
    step = 2.0 * math.pi / n
    i = pl.program_id(0)
    row = lax.broadcasted_iota(jnp.int32, (tr, L), 0)
    col = lax.broadcasted_iota(jnp.int32, (tr, L), 1)

    @pl.when(i == 0)
    def _():
        ang = ((row * col) & (n - 1)).astype(F32) * step
        cb_sc[...] = jnp.cos(ang)
        sb_sc[...] = jnp.sin(ang)

    col1 = lax.broadcasted_iota(jnp.int32, (1, L), 1)
    a = (((i * tr) * col1) & (n - 1)).astype(F32) * step
    ca, sa = jnp.cos(a), jnp.sin(a)
    cb, sb = cb_sc[...], sb_sc[...]
    c = cb * ca - sb * sa
    s = -(sb * ca + cb * sa)
    grow = row + i * tr
    fre_ref[...] = c.astype(fre_ref.dtype)
    fim_ref[...] = jnp.where(grow == 0, jnp.where((col & 1) == 0, 1.0, -1.0), s).astype(fim_ref.dtype)
    gim_ref[...] = jnp.where(col == 0, jnp.where((grow & 1) == 0, 1.0, -1.0), s).astype(gim_ref.dtype)


def _dft_tables(L, tr):
    spec = pl.BlockSpec((tr, L), lambda i: (i, 0))
    shape = jax.ShapeDtypeStruct((L, L), BF16)
    return pl.pallas_call(
        functools.partial(_dft_gen_kernel, L=L),
        grid=(L // tr,),
        out_specs=[spec, spec, spec],
        out_shape=[shape, shape, shape],
        scratch_shapes=[pltpu.VMEM((tr, L), F32), pltpu.VMEM((tr, L), F32)],
        compiler_params=_params("arbitrary"),
        name="dft_tables",
    )()


def _fwd_dft_kernel(*refs, with_filter):
    if with_filter:
        fre_ref, fim_ref, v_ref, kr_ref, ki_ref, y_ref = refs
    else:
        fre_ref, fim_ref, v_ref, y_ref = refs
    v = v_ref[...]
    vr = jnp.dot(fre_ref[...], v, preferred_element_type=F32)
    vi = jnp.dot(fim_ref[...], v, preferred_element_type=F32)
    if with_filter:
        kr, ki = kr_ref[...], ki_ref[...]
        tm = vr.shape[0]
        first = (lax.broadcasted_iota(jnp.int32, vr.shape, 0) + pl.program_id(1) * tm) == 0
        yr = vr * kr - jnp.where(first, 0.0, vi * ki)
        yi = jnp.where(first, vi * ki, vr * ki + vi * kr)
        y_ref[0] = yr.astype(y_ref.dtype)
        y_ref[1] = yi.astype(y_ref.dtype)
    else:
        y_ref[0] = vr.astype(y_ref.dtype)
        y_ref[1] = vi.astype(y_ref.dtype)


def _fwd_dft(fre, fim, v, kf, tm, out_dtype):
    B, L, D = v.shape
    ftile = pl.BlockSpec((tm, L), lambda b, m: (m, 0))
    in_specs = [ftile, ftile, pl.BlockSpec((None, L, D), lambda b, m: (b, 0, 0))]
    args = [fre, fim, v]
    if kf is not None:
        in_specs += [pl.BlockSpec((None, tm, D), lambda b, m: (0, m, 0)),
                     pl.BlockSpec((None, tm, D), lambda b, m: (1, m, 0))]
        args += [kf, kf]
    return pl.pallas_call(
        functools.partial(_fwd_dft_kernel, with_filter=kf is not None),
        grid=(B, L // tm),
        in_specs=in_specs,
        out_specs=pl.BlockSpec((None, 2, tm, D), lambda b, m: (b, 0, m, 0)),
        out_shape=jax.ShapeDtypeStruct((B, 2, L, D), out_dtype),
        compiler_params=_params("parallel", "parallel"),
        name="hyena_fwd_dft",
    )(*args)


def _filter_spectrum_kernel(hf_ref, hb_ref, k_ref, *, n):
    hfr, hfi = hf_ref[0], hf_ref[1]
    hbr, hbi = hb_ref[0], hb_ref[1]
    first = (lax.broadcasted_iota(jnp.int32, hfr.shape, 0) + pl.program_id(0) * hfr.shape[0]) == 0
    wgt = jnp.where(first, 1.0 / n, 2.0 / n)
    k_ref[0] = (hfr + hbr) * wgt
    k_ref[1] = jnp.where(first, hfi + hbi, hfi - hbi) * wgt


def _filter_spectrum(hspec, tm):
    _, _, L, D = hspec.shape
    return pl.pallas_call(
        functools.partial(_filter_spectrum_kernel, n=2 * L),
        grid=(L // tm,),
        in_specs=[pl.BlockSpec((None, 2, tm, D), lambda m: (0, 0, m, 0)),
                  pl.BlockSpec((None, 2, tm, D), lambda m: (1, 0, m, 0))],
        out_specs=pl.BlockSpec((2, tm, D), lambda m: (0, m, 0)),
        out_shape=jax.ShapeDtypeStruct((2, L, D), F32),
        compiler_params=_params("parallel"),
        name="hyena_filter_spectrum",
    )(hspec, hspec)


def _inv_dft_kernel(gre_ref, gim_ref, yr_ref, yi_ref, vx_ref, x0_ref, skip_ref, z_ref, acc_sc):
    kk = pl.program_id(2)

    @pl.when(kk == 0)
    def _():
        acc_sc[...] = jnp.zeros(acc_sc.shape, F32)

    acc_sc[...] += (jnp.dot(gre_ref[...], yr_ref[...], preferred_element_type=F32)
                    + jnp.dot(gim_ref[...], yi_ref[...], preferred_element_type=F32))

    @pl.when(kk == pl.num_programs(2) - 1)
    def _():
        z = (acc_sc[...] + vx_ref[...].astype(F32) * skip_ref[...]) * x0_ref[...]
        z_ref[...] = z.astype(z_ref.dtype)


def _inv_dft(gre, gim, y, vx, x0, skip, tm, tk):
    B, L, D = vx.shape
    gtile = pl.BlockSpec((tm, tk), lambda b, i, k: (i, k))
    row = pl.BlockSpec((None, tm, D), lambda b, i, k: (b, i, 0))
    return pl.pallas_call(
        _inv_dft_kernel,
        grid=(B, L // tm, L // tk),
        in_specs=[gtile, gtile,
                  pl.BlockSpec((None, None, tk, D), lambda b, i, k: (b, 0, k, 0)),
                  pl.BlockSpec((None, None, tk, D), lambda b, i, k: (b, 1, k, 0)),
                  row, row, pl.BlockSpec((1, D), lambda b, i, k: (0, 0))],
        out_specs=row,
        out_shape=jax.ShapeDtypeStruct((B, L, D), BF16),
        scratch_shapes=[pltpu.VMEM((tm, D), F32)],
        compiler_params=_params("parallel", "parallel", "arbitrary"),
        name="hyena_inv_dft",
    )(gre, gim, y, y, vx, x0, skip)


FFT_N1 = 64
FFT_SLOT_GROUP = 8
FFT_ROWS = 16
FFT_BATCH = 2


def _cos_units(idx, n):
    return jnp.cos((idx & (n - 1)).astype(F32) * (2.0 * math.pi / n))


def _fft_step1_tables_kernel(fb_ref, gb_ref, *, n1, n2):
    n = n1 * n2
    h1, quarter, R = n1 // 2, n // 4, FFT_ROWS
    lr, lh = R.bit_length() - 1, h1.bit_length() - 1
    g = pl.program_id(0)

    def entry(part, slot, t1):
        idx = jnp.where((part == 1) & (slot == 0), (n // 2) * t1, n2 * t1 * slot + quarter * part)
        return _cos_units(idx, n)

    rows, cols = fb_ref.shape
    r = lax.broadcasted_iota(jnp.int32, (rows, cols), 0) + g * rows
    c = lax.broadcasted_iota(jnp.int32, (rows, cols), 1)
    val = entry(r >> (lr + lh), (r >> lr) & (h1 - 1), c >> lr)
    fb_ref[...] = jnp.where((r & (R - 1)) == (c & (R - 1)), val, 0.0).astype(fb_ref.dtype)
    rows, cols = gb_ref.shape
    r = lax.broadcasted_iota(jnp.int32, (rows, cols), 0) + g * rows
    c = lax.broadcasted_iota(jnp.int32, (rows, cols), 1)
    slot = (c >> lr) & (h1 - 1)
    val = jnp.where(slot == 0, 1.0, 2.0) * entry(c >> (lr + lh), slot, r >> lr)
    gb_ref[...] = jnp.where((r & (R - 1)) == (c & (R - 1)), val, 0.0).astype(gb_ref.dtype)


def _fft_step2_tables_kernel(m3_ref, m3i_ref, *, n1, n2):
    n = n1 * n2
    quarter, half = n // 4, n2 // 2
    g = pl.program_id(0)
    r = lax.broadcasted_iota(jnp.int32, (2 * n2, 2 * n2), 0)
    c = lax.broadcasted_iota(jnp.int32, (2 * n2, 2 * n2), 1)
    rj, rim = r & (n2 - 1), (r >= n2).astype(jnp.int32)
    cj, cim = c & (n2 - 1), (c >= n2).astype(jnp.int32)
    for si in range(m3_ref.shape[0]):
        s = g * m3_ref.shape[0] + si
        fwd = _cos_units(cj * (s + n1 * rj) + quarter * (rim - cim), n)
        inv = _cos_units(rj * (s + n1 * cj) + quarter * (cim - rim), n)
        if si == 0:
            def slot0(j, t2, t_im, out_im):
                lo = (j < half) & (t_im == 0)
                hi = (j >= half) & (t_im == 1)
                idx0 = jnp.where((out_im == 1) & (j == 0), (n // 2) * t2, t2 * n1 * j + quarter * out_im)
                idxh = t2 * (n1 // 2 + n1 * (j - half)) + quarter * out_im
                return lo, hi, jnp.where(lo, idx0, idxh)
            lo, hi, idx = slot0(rj, cj, cim, rim)
            fwd0 = jnp.where(lo | hi, _cos_units(idx, n), 0.0)
            lo, hi, idx = slot0(cj, rj, rim, cim)
            amp = jnp.where(lo & (cj == 0), 1.0, 2.0)
            inv0 = jnp.where(lo | hi, amp * _cos_units(idx, n), 0.0)
            fwd = jnp.where(g == 0, fwd0, fwd)
            inv = jnp.where(g == 0, inv0, inv)
        m3_ref[si] = fwd.astype(m3_ref.dtype)
        m3i_ref[si] = inv.astype(m3i_ref.dtype)


def _fft_tables(n1, n2):
    h1, sg, R = n1 // 2, FFT_SLOT_GROUP, FFT_ROWS
    steps = 8
    fb, gb = pl.pallas_call(
        functools.partial(_fft_step1_tables_kernel, n1=n1, n2=n2),
        grid=(steps,),
        out_specs=[pl.BlockSpec((2 * h1 * R // steps, h1 * R), lambda g: (g, 0)),
                   pl.BlockSpec((h1 * R // steps, 2 * h1 * R), lambda g: (g, 0))],
        out_shape=[jax.ShapeDtypeStruct((2 * h1 * R, h1 * R), BF16),
                   jax.ShapeDtypeStruct((h1 * R, 2 * h1 * R), BF16)],
        compiler_params=_params("parallel"),
        name="fft_step1_tables",
    )()
    mspec = pl.BlockSpec((sg, 2 * n2, 2 * n2), lambda g: (g, 0, 0))
    m3, m3i = pl.pallas_call(
        functools.partial(_fft_step2_tables_kernel, n1=n1, n2=n2),
        grid=(h1 // sg,),
        out_specs=[mspec, mspec],
        out_shape=[jax.ShapeDtypeStruct((h1, 2 * n2, 2 * n2), BF16)] * 2,
        compiler_params=_params("parallel"),
        name="fft_step2_tables",
    )()
    return fb, gb, m3, m3i


def _fft_step1_kernel(fb_ref, x_ref, a_ref):
    nb, h1, R, D = x_ref.shape
    for i in range(nb):
        a = jnp.dot(fb_ref[...], x_ref[i].reshape(h1 * R, D), preferred_element_type=F32)
        a_ref[i] = a.reshape(2, h1, R, D).astype(a_ref.dtype)


def _fft_step1(fb, x):
    B, L, D = x.shape
    R = FFT_ROWS
    h1 = fb.shape[1] // R
    n2 = L // h1
    return pl.pallas_call(
        _fft_step1_kernel,
        grid=(B // FFT_BATCH, n2 // R),
        in_specs=[pl.BlockSpec(fb.shape, lambda b, c: (0, 0)),
                  pl.BlockSpec((FFT_BATCH, h1, R, D), lambda b, c: (b, 0, c, 0))],
        out_specs=pl.BlockSpec((FFT_BATCH, 2, h1, R, D), lambda b, c: (b, 0, 0, c, 0)),
        out_shape=jax.ShapeDtypeStruct((B, 2, h1, n2, D), BF16),
        compiler_params=_params("parallel", "parallel"),
        name="fft_step1",
    )(fb, x.reshape(B, h1, n2, D))


def _first_bin(shape, g):
    return (lax.broadcasted_iota(jnp.int32, shape, 0) == 0) & (g == 0)


def _fft_filter_spectrum_kernel(m3_ref, a_ref, k_ref, *, n):
    n2 = a_ref.shape[-2]
    g = pl.program_id(0)
    for si in range(m3_ref.shape[0]):
        hf = jnp.dot(m3_ref[si], jnp.concatenate([a_ref[0, 0, si], a_ref[0, 1, si]], axis=0),
                     preferred_element_type=F32)
        hb = jnp.dot(m3_ref[si], jnp.concatenate([a_ref[1, 0, si], a_ref[1, 1, si]], axis=0),
                     preferred_element_type=F32)
        k_ref[0, si] = (hf[:n2] + hb[:n2]) * (1.0 / n)
        ki = hf[n2:] - hb[n2:]
        if si == 0:
            ki = jnp.where(_first_bin(ki.shape, g), hf[n2:] + hb[n2:], ki)
        k_ref[1, si] = ki * (1.0 / n)


def _fft_filter_spectrum(m3, a):
    h1, n2 = m3.shape[0], m3.shape[1] // 2
    D = a.shape[-1]
    sg = FFT_SLOT_GROUP
    return pl.pallas_call(
        functools.partial(_fft_filter_spectrum_kernel, n=2 * h1 * n2),
        grid=(h1 // sg,),
        in_specs=[pl.BlockSpec((sg, 2 * n2, 2 * n2), lambda g: (g, 0, 0)),
                  pl.BlockSpec((2, 2, sg, n2, D), lambda g: (0, 0, g, 0, 0))],
        out_specs=pl.BlockSpec((2, sg, n2, D), lambda g: (0, g, 0, 0)),
        out_shape=jax.ShapeDtypeStruct((2, h1, n2, D), F32),
        compiler_params=_params("parallel"),
        name="fft_filter_spectrum",
    )(m3, a)


def _fft_step2_kernel(m3_ref, m3i_ref, kr_ref, ki_ref, a_ref, b_ref, y_sc):
    n2 = a_ref.shape[-2]
    g = pl.program_id(0)
    sg = m3_ref.shape[0]
    for si in range(sg):
        xh = jnp.dot(m3_ref[si], a_ref[:, si].reshape(2 * n2, a_ref.shape[-1]), preferred_element_type=F32)
        xr, xi = xh[:n2], xh[n2:]
        kr, ki = kr_ref[si], ki_ref[si]
        yr = xr * kr - xi * ki
        yi = xr * ki + xi * kr
        if si == 0:
            first = _first_bin(yr.shape, g)
            yr = jnp.where(first, xr * kr, yr)
            yi = jnp.where(first, xi * ki, yi)
        y_sc[si, :n2] = yr.astype(BF16)
        y_sc[si, n2:] = yi.astype(BF16)
    for si in range(sg):
        bb = jnp.dot(m3i_ref[si], y_sc[si], preferred_element_type=F32)
        b_ref[0, si] = bb[:n2].astype(b_ref.dtype)
        b_ref[1, si] = bb[n2:].astype(b_ref.dtype)


def _fft_step2(m3, m3i, kf, a):
    B, _, h1, n2, D = a.shape
    sg = FFT_SLOT_GROUP
    mspec = pl.BlockSpec((sg, 2 * n2, 2 * n2), lambda g, b: (g, 0, 0))
    blk = pl.BlockSpec((None, 2, sg, n2, D), lambda g, b: (b, 0, g, 0, 0))
    return pl.pallas_call(
        _fft_step2_kernel,
        grid=(h1 // sg, B),
        in_specs=[mspec, mspec,
                  pl.BlockSpec((None, sg, n2, D), lambda g, b: (0, g, 0, 0)),
                  pl.BlockSpec((None, sg, n2, D), lambda g, b: (1, g, 0, 0)),
                  blk],
        out_specs=blk,
        out_shape=jax.ShapeDtypeStruct((B, 2, h1, n2, D), BF16),
        scratch_shapes=[pltpu.VMEM((sg, 2 * n2, D), BF16)],
        compiler_params=_params("parallel", "parallel"),
        name="fft_step2",
    )(m3, m3i, kf, kf, a)


def _fft_inv_step1_kernel(gb_ref, b_ref, vx_ref, x0_ref, skip_ref, z_ref):
    nb, _, h1, R, D = b_ref.shape
    for i in range(nb):
        y = jnp.dot(gb_ref[...], b_ref[i].reshape(2 * h1 * R, D), preferred_element_type=F32)
        z = (y.reshape(h1, R, D) + vx_ref[i].astype(F32) * skip_ref[...]) * x0_ref[i]
        z_ref[i] = z.astype(z_ref.dtype)


def _fft_inv_step1(gb, bm, vx, x0, skip):
    B, L, D = vx.shape
    R = FFT_ROWS
    h1 = gb.shape[0] // R
    n2 = L // h1
    blk = pl.BlockSpec((FFT_BATCH, h1, R, D), lambda b, c: (b, 0, c, 0))
    z = pl.pallas_call(
        _fft_inv_step1_kernel,
        grid=(B // FFT_BATCH, n2 // R),
        in_specs=[pl.BlockSpec(gb.shape, lambda b, c: (0, 0)),
                  pl.BlockSpec((FFT_BATCH, 2, h1, R, D), lambda b, c: (b, 0, 0, c, 0)),
                  blk, blk, pl.BlockSpec((1, D), lambda b, c: (0, 0))],
        out_specs=blk,
        out_shape=jax.ShapeDtypeStruct((B, h1, n2, D), BF16),
        compiler_params=_params("parallel", "parallel"),
        name="fft_inv_step1",
    )(gb, bm, vx.reshape(B, h1, n2, D), x0.reshape(B, h1, n2, D), skip)
    return z.reshape(B, L, D)


def _rope_tables(L):
    pos = jnp.arange(L, dtype=jnp.int32)
    rows = (pos // GRID_W).astype(F32)
    cols = (pos % GRID_W).astype(F32)
    inv = ROPE_BASE ** (-jnp.arange(ROPE_PAIRS, dtype=F32) / ROPE_PAIRS)
    lane = jnp.arange(LANES)
    within = lane % HEAD_DIM
    axis = within // (2 * ROPE_PAIRS)
    half = (within % (2 * ROPE_PAIRS)) // ROPE_PAIRS
    ang = jnp.where(axis[None, :] == 0, rows[:, None], cols[:, None]) * inv[within % ROPE_PAIRS][None, :]
    cos, sin = jnp.cos(ang), jnp.sin(ang)
    sin_a = jnp.where(half[None, :] == 0, -sin, 0.0)
    sin_b = jnp.where(half[None, :] == 1, sin, 0.0)
    return cos, sin_a, sin_b


def _tile(L, pref):
    return min(L, pref)


def _four_step(L):
    return (2 * L) // FFT_N1 >= 32


def _conv_tables(L):
    return _fft_tables(FFT_N1, 2 * L // FFT_N1) if _four_step(L) else _dft_tables(L, _tile(L, DFT_ROW_TILE))


def _run_group(x, rowmap, shared_mod, mods, p, cache, rope_tabs, dft, kv_dtype):
    B, L, D = x.shape
    tm = _tile(L, ROW_TILE)
    rows = (1, B * L, D) if shared_mod else (B, L, D)
    tmr = _tile(rows[1], ROW_TILE)
    ctx_k, ctx_v = [], []
    for i in range(DEPTH):
        j = i // 2
        sh1, sc1, g1, sh2, sc2, g2 = (mods[i, :, s * D:(s + 1) * D].reshape(MOD_ROWS, 1, D) for s in range(6))
        n1 = p["norm1_g"][i][None]
        if i % 2 == 0:
            lam_init = 0.8 - 0.6 * math.exp(-0.3 * i)
            q, k, v = (a.reshape(B, L, D) for a in
                       _qkv(x.reshape(rows), n1, sh1, sc1, p["attn_w_qkv"][j], rowmap, rope_tabs, kv_dtype, tmr))
            ctx_k.append(k)
            ctx_v.append(v)
            mix = _attention(q, k, v, None if cache is None else cache + (j,),
                             p["attn_lambda"][j].astype(F32), p["attn_subln_g"][j][None],
                             lam_init, _tile(L, ATTN_TILE), _tile(L, ATTN_TILE),
                             N_HEADS if L <= ATTN_TILE else 1)
            w_o, b_o = p["attn_w_o"][j], None
        else:
            vx, x0 = _hyena_in(x, n1, sh1, sc1, p["hy_w_in"][j], p["hy_b_in"][j][None],
                               p["hy_conv_w"][j], p["hy_conv_b"][j][None], rowmap, tm)
            filt = _implicit_filter(L, p["filt_w1"][j], p["filt_b1"][j], p["filt_w2"][j], p["filt_b2"][j],
                                    p["filt_w3"][j], p["filt_b3"][j], p["filt_freq"][j], FILT_COL_TILE)
            skip = p["hy_skip"][j][None]
            if _four_step(L):
                fb, gb, m3, m3i = dft
                kf = _fft_filter_spectrum(m3, _fft_step1(fb, filt))
                bm = _fft_step2(m3, m3i, kf, _fft_step1(fb, vx))
                mix = _fft_inv_step1(gb, bm, vx, x0, skip)
            else:
                fre, fim, gim = dft
                tmf = _tile(L, DFT_ROW_TILE)
                kf = _filter_spectrum(_fwd_dft(fre, fim, filt, None, tmf, F32), tmf)
                y = _fwd_dft(fre, fim, vx, kf, tmf, BF16)
                mix = _inv_dft(fre, gim, y, vx, x0, skip, tm, _tile(L, ROW_TILE))
            w_o, b_o = p["hy_w_out"][j], p["hy_b_out"][j][None]
        final_g = p["final_g"][None] if i == DEPTH - 1 else None
        x = _out_ffn(mix.reshape(rows), w_o, b_o, x.reshape(rows), g1, p["norm2_g"][i][None], sh2, sc2, g2,
                     p["ffn_w_gu"][i], p["ffn_w_down"][i], final_g, rowmap, tmr).reshape(B, L, D)
    return x, ctx_k, ctx_v


def kernel(x_prompt, x_sample, cache_k, cache_v, c, c_ctx, ada_w, ada_b, norm1_g, norm2_g, attn_w_qkv, attn_lambda, attn_subln_g, attn_w_o, hy_w_in, hy_b_in, hy_conv_w, hy_conv_b, filt_w1, filt_b1, filt_w2, filt_b2, filt_w3, filt_b3, filt_freq, hy_skip, hy_w_out, hy_b_out, ffn_w_gu, ffn_w_down, final_g):
    D = D_MODEL
    nb = c.shape[0]
    cmat = jnp.concatenate([c, c_ctx[None], jnp.zeros((MOD_ROWS - nb - 1, D), F32)], axis=0)
    mods = _adaln_all(cmat, ada_w, ada_b)

    p = dict(norm1_g=norm1_g, norm2_g=norm2_g, attn_lambda=attn_lambda, attn_subln_g=attn_subln_g,
             hy_b_in=hy_b_in, hy_conv_w=hy_conv_w, hy_conv_b=hy_conv_b,
             filt_w1=filt_w1, filt_b1=filt_b1, filt_w2=filt_w2, filt_b2=filt_b2, filt_w3=filt_w3,
             filt_b3=filt_b3, filt_freq=filt_freq, hy_skip=hy_skip, hy_b_out=hy_b_out, final_g=final_g,
             attn_w_qkv=attn_w_qkv.astype(BF16), attn_w_o=attn_w_o.astype(BF16),
             hy_w_in=hy_w_in.astype(BF16), hy_w_out=hy_w_out.astype(BF16),
             ffn_w_gu=ffn_w_gu.astype(BF16), ffn_w_down=ffn_w_down.astype(BF16))

    Lp, Ls = x_prompt.shape[1], x_sample.shape[1]
    y_prompt, ctx_k, ctx_v = _run_group(x_prompt, lambda b: CTX_ROW, True, mods, p, None, None,
                                        _conv_tables(Lp), F32)
    cache = tuple(a.reshape(a.shape[0], a.shape[1], a.shape[2], D) for a in (cache_k, cache_v))
    y_sample, _, _ = _run_group(x_sample, lambda b: b, False, mods, p, cache, _rope_tables(Ls),
                                _conv_tables(Ls), BF16)

    Bp = x_prompt.shape[0]
    new_k = jnp.stack(ctx_k, axis=1).reshape(Bp, len(ctx_k), Lp, N_HEADS, 2, HEAD_DIM)
    new_v = jnp.stack(ctx_v, axis=1).reshape(Bp, len(ctx_v), Lp, N_HEADS, V_DIM)
    return y_prompt, y_sample, new_k, new_v
```
